```python
import math
import jax, jax.numpy as jnp
from jax import lax
import numpy as np


D_MODEL = 1024
BATCH = 2
SEQ = 16384
DEPTH = 2

N_MIXERS = 4
HEAD_DIM = 64
N_HEADS_GROUP = D_MODEL // (N_MIXERS * HEAD_DIM)
GROUP_WIDTH = N_HEADS_GROUP * HEAD_DIM
MIX_WIDTH = N_MIXERS * GROUP_WIDTH
Q_BLK = 128
GRID_W = 64
NA_ROWS = 8
NA_COLS = 16
DIFF_QK_DIM = HEAD_DIM // 2
DIL_PATTERNS = ((128, 1), (512, 4), (2048, 16))
MLA_Q_RANK = D_MODEL // 4
MLA_KV_RANK = D_MODEL // 8
MLA_NOPE = HEAD_DIM
MLA_ROPE = HEAD_DIM // 2
MLA_V = HEAD_DIM
ROPE_THETA = 10000.0
T5_BUCKETS = 32
T5_MAX_DIST = 1024
T5_HEADS = 2 * N_HEADS_GROUP
D_FF = ((8 * D_MODEL + 3 * 256 - 1) // (3 * 256)) * 256
IN_WIDTH = 9 * GROUP_WIDTH + MLA_Q_RANK + MLA_KV_RANK + MLA_ROPE
EPS = 1e-6

kernel_name = 'hybrid_parallel_heads_encoder'


def rms_norm(x, g):
    xf = x.astype(jnp.float32)
    y = xf * lax.rsqrt(jnp.mean(xf * xf, axis=-1, keepdims=True) + EPS)
    return (y * g.astype(jnp.float32)).astype(x.dtype)


def t5_bucket(rel):
    nb = T5_BUCKETS // 2
    max_exact = nb // 2
    side = jnp.where(rel > 0, nb, 0)
    n = jnp.abs(rel)
    large = max_exact + (jnp.log(jnp.maximum(n, 1).astype(jnp.float32) / max_exact)
                         / math.log(T5_MAX_DIST / max_exact) * (nb - max_exact)).astype(jnp.int32)
    large = jnp.minimum(large, nb - 1)
    return side + jnp.where(n < max_exact, n, large)


def rope_tables(pos):
    half = MLA_ROPE // 2
    inv_freq = ROPE_THETA ** (-jnp.arange(half, dtype=jnp.float32) / half)
    ang = pos.astype(jnp.float32)[:, None] * inv_freq[None, :]
    return jnp.cos(ang), jnp.sin(ang)


def apply_rope(x, cos, sin):
    half = x.shape[-1] // 2
    x1 = x[..., :half].astype(jnp.float32)
    x2 = x[..., half:].astype(jnp.float32)
    return jnp.concatenate([x1 * cos - x2 * sin, x1 * sin + x2 * cos], axis=-1).astype(x.dtype)


def neighborhood_attention(q, k, v, rpb):
    B, S, H, D = q.shape
    rows = S // GRID_W
    wr = min(NA_ROWS, rows)
    scale = D ** -0.5
    kg = k.reshape(B, rows, GRID_W, H, D)
    vg = v.reshape(B, rows, GRID_W, H, D)
    qg = q.reshape(B, rows, GRID_W, H, D).swapaxes(0, 1)
    col = jnp.arange(GRID_W)
    col_idx = jnp.clip(col - NA_COLS // 2, 0, GRID_W - NA_COLS)[:, None] + jnp.arange(NA_COLS)[None, :]
    col_bias_idx = col_idx - col[:, None] + NA_COLS - 1

    def one_row(args):
        q_row, r = args
        r0 = jnp.clip(r - wr // 2, 0, rows - wr)
        k_rows = lax.dynamic_slice_in_dim(kg, r0, wr, axis=1)
        v_rows = lax.dynamic_slice_in_dim(vg, r0, wr, axis=1)
        k_nb = k_rows[:, :, col_idx]
        v_nb = v_rows[:, :, col_idx]
        row_bias_idx = r0 + jnp.arange(wr) - r + NA_ROWS - 1
        bias = rpb[:, row_bias_idx[None, :, None], col_bias_idx[:, None, :]]
        s = jnp.einsum('bqhd,brqchd->bhqrc', q_row, k_nb).astype(jnp.float32) * scale + bias.astype(jnp.float32)
        p = jax.nn.softmax(s.reshape(B, H, GRID_W, wr * NA_COLS), axis=-1).reshape(s.shape)
        return jnp.einsum('bhqrc,brqchd->bqhd', p.astype(v.dtype), v_nb)

    out = lax.map(one_row, (qg, jnp.arange(rows)))
    return out.swapaxes(0, 1).reshape(B, S, H * D)


def diff_attention(q, k, v, lam, lam_init, subln_g, table):
    B, S, H, _, dk = q.shape
    nblk = S // Q_BLK
    scale = dk ** -0.5
    qb = q.reshape(B, nblk, Q_BLK, H, 2, dk).swapaxes(0, 1)
    kpos = jnp.arange(S)

    def one_block(args):
        q_blk, i = args
        qpos = i * Q_BLK + jnp.arange(Q_BLK)
        bias = table[t5_bucket(kpos[None, :] - qpos[:, None])].transpose(2, 0, 1).astype(jnp.float32)
        s = jnp.einsum('bqhcd,bkhcd->bchqk', q_blk, k).astype(jnp.float32) * scale + bias[None, None]
        p = jax.nn.softmax(s, axis=-1)
        a = p[:, 0] - lam * p[:, 1]
        return jnp.einsum('bhqk,bkhd->bqhd', a.astype(v.dtype), v)

    o = lax.map(one_block, (qb, jnp.arange(nblk))).swapaxes(0, 1).reshape(B, S, H, -1)
    o = rms_norm(o, subln_g) * (1.0 - lam_init)
    return o.reshape(B, S, -1)


def dilated_attention(q, k, v, table):
    B, S, H, D = q.shape
    nblk = S // Q_BLK
    scale = D ** -0.5
    qb = q.reshape(B, nblk, Q_BLK, H, D).swapaxes(0, 1)

    def one_block(args):
        q_blk, i = args
        qpos = i * Q_BLK + jnp.arange(Q_BLK)
        outs, lses = [], []
        for window, dil in DIL_PATTERNS:
            m = window // (2 * dil)
            offs = dil * jnp.arange(-m, m + 1)
            kpos = qpos[:, None] + offs[None, :]
            valid = (kpos >= 0) & (kpos < S)
            kidx = jnp.clip(kpos, 0, S - 1)
            k_g = k[:, kidx]
            v_g = v[:, kidx]
            bias = table[t5_bucket(offs)].T.astype(jnp.float32)
            s = jnp.einsum('bqhd,bqkhd->bhqk', q_blk, k_g).astype(jnp.float32) * scale + bias[None, :, None, :]
            s = jnp.where(valid[None, None], s, -jnp.inf)
            s_max = jnp.max(s, axis=-1, keepdims=True)
            e = jnp.exp(s - s_max)
            den = jnp.sum(e, axis=-1)
            o = jnp.einsum('bhqk,bqkhd->bqhd', e, v_g.astype(jnp.float32)) / den.transpose(0, 2, 1)[..., None]
            outs.append(o)
            lses.append(s_max[..., 0] + jnp.log(den))
        wts = jax.nn.softmax(jnp.stack(lses), axis=0)
        out = jnp.sum(jnp.stack(outs) * wts.transpose(0, 1, 3, 2)[..., None], axis=0)
        return out.astype(q.dtype)

    out = lax.map(one_block, (qb, jnp.arange(nblk)))
    return out.swapaxes(0, 1).reshape(B, S, H * D)


def mla_attention(q_nope, q_rope, k_nope, k_rope, v):
    B, S, H, _ = q_nope.shape
    nblk = S // Q_BLK
    scale = (MLA_NOPE + MLA_ROPE) ** -0.5

    def blocks(t):
        return t.reshape(B, nblk, Q_BLK, *t.shape[2:]).swapaxes(0, 1)

    def one_block(args):
        qn, qr = args
        s = jnp.einsum('bqhd,bkhd->bhqk', qn, k_nope) + jnp.einsum('bqhd,bkd->bhqk', qr, k_rope)
        p = jax.nn.softmax(s.astype(jnp.float32) * scale, axis=-1)
        return jnp.einsum('bhqk,bkhd->bqhd', p.astype(v.dtype), v)

    out = lax.map(one_block, (blocks(q_nope), blocks(q_rope)))
    return out.swapaxes(0, 1).reshape(B, S, -1)


def setup_inputs(seed: int = 0) -> dict:
    key = jax.random.key(seed)
    ks = jax.random.split(key, 17)
    H = N_HEADS_GROUP

    def nrm(k, shape, scale):
        return jax.random.normal(k, shape, jnp.float32) * scale

    return {
        'x': nrm(ks[0], (BATCH, SEQ, D_MODEL), 1.0),
        'attn_norm': 1.0 + nrm(ks[1], (DEPTH, D_MODEL), 0.05),
        'w_in': nrm(ks[2], (DEPTH, D_MODEL, IN_WIDTH), D_MODEL ** -0.5),
        'na_rpb': nrm(ks[3], (DEPTH, H, 2 * NA_ROWS - 1, 2 * NA_COLS - 1), 0.1),
        'diff_lambda': nrm(ks[4], (DEPTH, 4, DIFF_QK_DIM), 0.1),
        'diff_subln': 1.0 + nrm(ks[5], (DEPTH, HEAD_DIM), 0.05),
        'mla_q_norm': 1.0 + nrm(ks[6], (DEPTH, MLA_Q_RANK), 0.05),
        'w_uq': nrm(ks[7], (DEPTH, MLA_Q_RANK, H * (MLA_NOPE + MLA_ROPE)), MLA_Q_RANK ** -0.5),
        'mla_kv_norm': 1.0 + nrm(ks[8], (DEPTH, MLA_KV_RANK), 0.05),
        'w_ukv': nrm(ks[9], (DEPTH, MLA_KV_RANK, H * (MLA_NOPE + MLA_V)), MLA_KV_RANK ** -0.5),
        't5_table': nrm(ks[10], (T5_BUCKETS, T5_HEADS), 0.1),
        'w_o': nrm(ks[11], (DEPTH, MIX_WIDTH, D_MODEL), MIX_WIDTH ** -0.5),
        'ffn_norm': 1.0 + nrm(ks[12], (DEPTH, D_MODEL), 0.05),
        'w_gate': nrm(ks[13], (DEPTH, D_MODEL, D_FF), D_MODEL ** -0.5),
        'w_up': nrm(ks[14], (DEPTH, D_MODEL, D_FF), D_MODEL ** -0.5),
        'w_down': nrm(ks[15], (DEPTH, D_FF, D_MODEL), D_FF ** -0.5),
        'final_norm': 1.0 + nrm(ks[16], (D_MODEL,), 0.05),
    }


def reference(x, attn_norm, w_in, na_rpb, diff_lambda, diff_subln, mla_q_norm, w_uq,
              mla_kv_norm, w_ukv, t5_table, w_o, ffn_norm, w_gate, w_up, w_down, final_norm):
    B, S, _ = x.shape
    H, G = N_HEADS_GROUP, GROUP_WIDTH
    cos, sin = rope_tables(jnp.arange(S))
    t5_diff = t5_table[:, :H]
    t5_dil = t5_table[:, H:]
    split_points = np.cumsum([G] * 9 + [MLA_Q_RANK, MLA_KV_RANK]).tolist()
    for l in range(DEPTH):
        h = rms_norm(x, attn_norm[l])
        proj = h @ w_in[l]
        qa, ka, va, qb, kb, vb, qc, kc, vc, cq, ckv, kr = jnp.split(proj, split_points, axis=-1)

        o_a = neighborhood_attention(qa.reshape(B, S, H, HEAD_DIM), ka.reshape(B, S, H, HEAD_DIM),
                                     va.reshape(B, S, H, HEAD_DIM), na_rpb[l])

        lam_init = 0.8 - 0.6 * math.exp(-0.3 * l)
        lq1, lk1, lq2, lk2 = [diff_lambda[l, j].astype(jnp.float32) for j in range(4)]
        lam = jnp.exp(jnp.sum(lq1 * lk1)) - jnp.exp(jnp.sum(lq2 * lk2)) + lam_init
        o_b = diff_attention(qb.reshape(B, S, H, 2, DIFF_QK_DIM), kb.reshape(B, S, H, 2, DIFF_QK_DIM),
                             vb.reshape(B, S, H, HEAD_DIM), lam, lam_init, diff_subln[l], t5_diff)

        o_c = dilated_attention(qc.reshape(B, S, H, HEAD_DIM), kc.reshape(B, S, H, HEAD_DIM),
                                vc.reshape(B, S, H, HEAD_DIM), t5_dil)

        cq = rms_norm(cq, mla_q_norm[l])
        qd = (cq @ w_uq[l]).reshape(B, S, H, MLA_NOPE + MLA_ROPE)
        q_nope = qd[..., :MLA_NOPE]
        q_rope = apply_rope(qd[..., MLA_NOPE:], cos[:, None, :], sin[:, None, :])
        ckv = rms_norm(ckv, mla_kv_norm[l])
        kvd = (ckv @ w_ukv[l]).reshape(B, S, H, MLA_NOPE + MLA_V)
        k_nope = kvd[..., :MLA_NOPE]
        v_d = kvd[..., MLA_NOPE:]
        k_rope = apply_rope(kr, cos, sin)
        o_d = mla_attention(q_nope, q_rope, k_nope, k_rope, v_d)

        mix = jnp.concatenate([o_a, o_b, o_c, o_d], axis=-1)
        x = x + mix @ w_o[l]

        h = rms_norm(x, ffn_norm[l])
        x = x + (jax.nn.silu(h @ w_gate[l]) * (h @ w_up[l])) @ w_down[l]
    return rms_norm(x, final_norm)
```

```python
import functools
import math

import jax
import jax.numpy as jnp
from jax import lax
from jax.experimental import pallas as pl
from jax.experimental.pallas import tpu as pltpu

F32 = jnp.float32
BF16 = jnp.bfloat16

N_HEADS = 4
HEAD_DIM = 64
GROUP = N_HEADS * HEAD_DIM
PAIR = 2 * HEAD_DIM
GRID_W = 64
NA_ROWS = 8
NA_COLS = 16
DIFF_QK = HEAD_DIM // 2
DIL_PATTERNS = ((128, 1), (512, 4), (2048, 16))
MLA_ROPE = HEAD_DIM // 2
MLA_HALF = MLA_ROPE // 2
MLA_QK = HEAD_DIM + MLA_ROPE
ROPE_THETA = 10000.0
T5_BUCKETS = 32
T5_MAX_DIST = 1024
EPS = 1e-6
LOG2E = 1.4426950408889634
NEG = -1e30

V7X_VMEM_LIMIT = 56 * 1024 * 1024

TOK_TILE = 512
FULL_T = 512
BAND_T = 256
NA_GROUP = 8
FF_SPLIT = 2


def _rms(x, g):
    return x * lax.rsqrt(jnp.mean(x * x, axis=-1, keepdims=True) + EPS) * g


def _nt(a, b):
    return lax.dot_general(a, b, (((1,), (1,)), ((), ())), preferred_element_type=F32)


def _flash_step(k2, qz, v_t, m, l, acc, bias=None):
    s_t = jnp.dot(k2, qz, preferred_element_type=F32)
    if bias is not None:
        s_t = s_t + bias
    m_new = jnp.maximum(m, jnp.max(s_t, axis=0, keepdims=True))
    alpha = jnp.exp2(m - m_new)
    p = jnp.exp2(s_t - m_new)
    l = alpha * l + jnp.sum(p, axis=0, keepdims=True)
    acc = alpha * acc + jnp.dot(v_t, p.astype(BF16), preferred_element_type=F32)
    return m_new, l, acc


def _flash_init(tq):
    return (jnp.full((1, tq), NEG, F32), jnp.zeros((1, tq), F32), jnp.zeros((HEAD_DIM, tq), F32))


def _store_t_blocks(ref, val, blk):
    for c in range(val.shape[1] // blk):
        ref[c] = val[:, c * blk:(c + 1) * blk].astype(ref.dtype)


def _proj_kernel(x_ref, g_ref, wnat_ref, wt_ref, gq_ref, wuqt_ref, gkv_ref, wuk_ref, wuvt_ref,
                 place_ref, csn_ref, cost_ref, sint_ref,
                 qa_ref, ka_ref, va_ref, kb_ref, kc_ref, kd_ref,
                 qbt_ref, vbt_ref, qct_ref, vct_ref, qdt_ref, vdt_ref):
    h = _rms(x_ref[...], g_ref[...]).astype(BF16)
    nat = jnp.dot(h, wnat_ref[...], preferred_element_type=F32)
    qa_ref[...] = (nat[:, 0:256] * (HEAD_DIM ** -0.5 * LOG2E)).astype(BF16)
    ka_ref[...] = nat[:, 256:512].astype(BF16)
    va_ref[...] = nat[:, 512:768].astype(BF16)
    kb_ref[...] = nat[:, 768:1024].astype(BF16)
    kc_ref[...] = nat[:, 1024:1280].astype(BF16)
    cq = nat[:, 1280:1536]
    ckv = nat[:, 1536:1664]
    kr2 = nat[:, 1664:1792]

    tr = _nt(wt_ref[...], h)
    _store_t_blocks(qbt_ref, tr[0:256] * (DIFF_QK ** -0.5 * LOG2E), FULL_T)
    _store_t_blocks(vbt_ref, tr[256:512], FULL_T)
    _store_t_blocks(qct_ref, tr[512:768] * (HEAD_DIM ** -0.5 * LOG2E), BAND_T)
    _store_t_blocks(vct_ref, tr[768:1024], BAND_T)

    cqn = _rms(cq, gq_ref[...]).astype(BF16)
    qd_t = _nt(wuqt_ref[...], cqn)
    cos_t = cost_ref[...]
    sin_t = sint_ref[...]
    pieces = []
    for hd in range(N_HEADS):
        b0 = hd * PAIR
        x1 = qd_t[b0 + 64:b0 + 80]
        x2 = qd_t[b0 + 80:b0 + 96]
        pieces += [qd_t[b0:b0 + 64], x1 * cos_t - x2 * sin_t, x1 * sin_t + x2 * cos_t,
                   qd_t[b0 + 96:b0 + 128]]
    qd_t = jnp.concatenate(pieces, axis=0) * (MLA_QK ** -0.5 * LOG2E)
    _store_t_blocks(qdt_ref, qd_t, FULL_T)

    ckvn = _rms(ckv, gkv_ref[...]).astype(BF16)
    k_nope = jnp.dot(ckvn, wuk_ref[...], preferred_element_type=F32)
    t = kr2 * csn_ref[...]
    roped = t + pltpu.roll(t, 96, 1)
    kd = k_nope + jnp.dot(roped.astype(BF16), place_ref[...], preferred_element_type=F32)
    kd_ref[...] = kd.astype(BF16)
    _store_t_blocks(vdt_ref, _nt(wuvt_ref[...], ckvn), FULL_T)


def _proj_call(x, g, p, rope):
    B, S, D = x.shape
    ts = TOK_TILE
    nt = S // ts
    nf, nb = ts // FULL_T, ts // BAND_T

    def full(a):
        return pl.BlockSpec(a.shape, lambda b, t: (0,) * a.ndim)

    nat_spec = lambda w: pl.BlockSpec((None, ts, w), lambda b, t: (b, t, 0))
    tb_spec = lambda n, r, blk: pl.BlockSpec((None, n, r, blk), lambda b, t: (b, t, 0, 0))
    nat_shape = lambda w: jax.ShapeDtypeStruct((B, S, w), BF16)
    tb_shape = lambda r, blk: jax.ShapeDtypeStruct((B, S // blk, r, blk), BF16)

    weights = [g, p['w_nat'], p['w_t'], p['g_q'], p['w_uq_t'], p['g_kv'], p['w_uk'], p['w_uv_t'], p['place']]
    in_specs = ([nat_spec(D)] + [full(a) for a in weights] +
                [pl.BlockSpec((ts, PAIR), lambda b, t: (t, 0)),
                 pl.BlockSpec((None, MLA_HALF, ts), lambda b, t: (t, 0, 0)),
                 pl.BlockSpec((None, MLA_HALF, ts), lambda b, t: (t, 0, 0))])
    out_shape = ([nat_shape(GROUP)] * 5 + [nat_shape(2 * GROUP)] +
                 [tb_shape(GROUP, FULL_T)] * 2 + [tb_shape(GROUP, BAND_T)] * 2 +
                 [tb_shape(2 * GROUP, FULL_T), tb_shape(GROUP, FULL_T)])
    out_specs = ([nat_spec(GROUP)] * 5 + [nat_spec(2 * GROUP)] +
                 [tb_spec(nf, GROUP, FULL_T)] * 2 + [tb_spec(nb, GROUP, BAND_T)] * 2 +
                 [tb_spec(nf, 2 * GROUP, FULL_T), tb_spec(nf, GROUP, FULL_T)])
    return pl.pallas_call(
        _proj_kernel,
        grid=(B, nt),
        in_specs=in_specs,
        out_specs=out_specs,
        out_shape=out_shape,
        compiler_params=pltpu.CompilerParams(
            dimension_semantics=("parallel", "parallel"), vmem_limit_bytes=V7X_VMEM_LIMIT),
        name="proj",
    )(x, *weights, rope['csn'], rope['cos_t'], rope['sin_t'])


def _na_kernel(q_ref, kp_ref, kc_ref, kn_ref, vp_ref, vc_ref, vn_ref, bias_ref, o_ref, kk_ref, vv_ref,
               *, n_rows):
    g = pl.program_id(1)
    blk = NA_GROUP * GRID_W
    win = NA_ROWS * GRID_W
    for c, (kr, vr) in enumerate(((kp_ref, vp_ref), (kc_ref, vc_ref), (kn_ref, vn_ref))):
        kk_ref[c * blk:(c + 1) * blk, :] = kr[...]
        vv_ref[c * blk:(c + 1) * blk, :] = vr[...]
    lane = lax.broadcasted_iota(jnp.int32, (GRID_W, PAIR), 1)

    def row_body(j, carry):
        r = g * NA_GROUP + j
        r0 = jnp.clip(r - NA_ROWS // 2, 0, n_rows - NA_ROWS)
        start = pl.multiple_of((r0 - (g - 1) * NA_GROUP) * GRID_W, GRID_W)
        didx = r - r0
        qs = pl.multiple_of(j * GRID_W, GRID_W)
        q_row = q_ref[pl.ds(qs, GRID_W), :]
        k_win = kk_ref[pl.ds(start, win), :]
        v_win = vv_ref[pl.ds(start, win), :]
        outs = []
        for pr in range(2):
            q2 = q_row[:, pr * PAIR:(pr + 1) * PAIR]
            k2 = k_win[:, pr * PAIR:(pr + 1) * PAIR]
            v2 = v_win[:, pr * PAIR:(pr + 1) * PAIR]
            o_pair = None
            for hh in range(2):
                mine = (lane // HEAD_DIM) == hh
                qz = jnp.where(mine, q2, jnp.zeros_like(q2))
                s = _nt(qz, k2) + bias_ref[didx, 2 * pr + hh]
                m = jnp.max(s, axis=-1, keepdims=True)
                e = jnp.exp2(s - m)
                den = jnp.sum(e, axis=-1, keepdims=True)
                o2 = jnp.dot(e.astype(BF16), v2, preferred_element_type=F32) / den
                o_pair = o2 if o_pair is None else jnp.where(mine, o2, o_pair)
            outs.append(o_pair)
        o_ref[pl.ds(qs, GRID_W), :] = jnp.concatenate(outs, axis=1).astype(o_ref.dtype)
        return carry

    lax.fori_loop(0, NA_GROUP, row_body, 0)


def _na_call(q, k, v, bias):
    B, S, _ = q.shape
    n_rows = S // GRID_W
    blk = NA_GROUP * GRID_W
    ng = S // blk
    cur = pl.BlockSpec((None, blk, GROUP), lambda b, g: (b, g, 0))
    prv = pl.BlockSpec((None, blk, GROUP), lambda b, g: (b, jnp.maximum(g - 1, 0), 0))
    nxt = pl.BlockSpec((None, blk, GROUP), lambda b, g: (b, jnp.minimum(g + 1, ng - 1), 0))
    return pl.pallas_call(
        functools.partial(_na_kernel, n_rows=n_rows),
        grid=(B, ng),
        in_specs=[cur, prv, cur, nxt, prv, cur, nxt,
                  pl.BlockSpec(bias.shape, lambda b, g: (0, 0, 0, 0))],
        out_specs=cur,
        out_shape=jax.ShapeDtypeStruct((B, S, GROUP), BF16),
        scratch_shapes=[pltpu.VMEM((3 * blk, GROUP), BF16), pltpu.VMEM((3 * blk, GROUP), BF16)],
        compiler_params=pltpu.CompilerParams(
            dimension_semantics=("parallel", "parallel"), vmem_limit_bytes=V7X_VMEM_LIMIT),
        name="na_attn",
    )(q, k, k, k, v, v, v, bias)


def _diff_kernel(c_ref, g_ref, qt_ref, k_ref, vt_ref, bias_ref, o_ref, *, nk, lam_scale):
    i = pl.program_id(2)
    tq = qt_ref.shape[1]
    tk = vt_ref.shape[2]
    half = BAND_T
    q_t = qt_ref[...]
    row = lax.broadcasted_iota(jnp.int32, q_t.shape, 0)
    lam = c_ref[0:1, 0:1]
    j_lo = jnp.maximum(i - 2, 0)
    j_hi = jnp.minimum(i + 3, nk)
    heads = []
    for hh in range(2):
        c_lo = c_ref[1 + hh:2 + hh, 0:1]
        c_hi = c_ref[3 + hh:4 + hh, 0:1]
        maps = []
        for c in range(2):
            r_lo = hh * HEAD_DIM + c * DIFF_QK
            qz = jnp.where((row >= r_lo) & (row < r_lo + DIFF_QK), q_t, jnp.zeros_like(q_t))

            def operands(j):
                ks = pl.multiple_of(j * tk, tk)
                return k_ref[pl.ds(ks, tk), :], vt_ref[j, hh * HEAD_DIM:(hh + 1) * HEAD_DIM, :]

            def far(j, carry, qz=qz, operands=operands):
                k2, v_t = operands(j)
                return _flash_step(k2, qz, v_t, *carry)

            def near(j, carry, qz=qz, operands=operands, hh=hh):
                k2, v_t = operands(j)
                e0 = 2 * (j - i) + 4

                def tile(d):
                    return bias_ref[jnp.clip(e0 + d, 0, 8), hh]
                t0 = tile(0)
                bias = jnp.concatenate(
                    [jnp.concatenate([t0, tile(-1)], axis=1),
                     jnp.concatenate([tile(1), t0], axis=1)], axis=0)
                return _flash_step(k2, qz, v_t, *carry, bias=bias)

            m, l, acc = _flash_init(tq)
            m, l, acc = lax.fori_loop(0, j_lo, far, (m - c_lo, l, acc))
            m, l, acc = lax.fori_loop(j_lo, j_hi, near, (m + c_lo, l, acc))
            m, l, acc = lax.fori_loop(j_hi, nk, far, (m - c_hi, l, acc))
            maps.append(acc / l)
        o = maps[0] - lam * maps[1]
        y = o * lax.rsqrt(jnp.mean(o * o, axis=0, keepdims=True) + EPS) * g_ref[:, 0:1]
        heads.append(y * lam_scale)
    del half
    o_ref[...] = jnp.concatenate(heads, axis=0).T.astype(o_ref.dtype)


def _diff_call(consts, g_col, q_t, k, v_t, bias, lam_scale):
    B, nq, _, tq = q_t.shape
    S = k.shape[1]
    nk, _, tk = v_t.shape[1:]
    return pl.pallas_call(
        functools.partial(_diff_kernel, nk=nk, lam_scale=lam_scale),
        grid=(B, 2, nq),
        in_specs=[
            pl.BlockSpec((None, 8, 128), lambda b, p, i: (p, 0, 0)),
            pl.BlockSpec(g_col.shape, lambda b, p, i: (0, 0)),
            pl.BlockSpec((None, None, PAIR, tq), lambda b, p, i: (b, i, p, 0)),
            pl.BlockSpec((None, S, PAIR), lambda b, p, i: (b, 0, p)),
            pl.BlockSpec((None, nk, PAIR, tk), lambda b, p, i: (b, 0, p, 0)),
            pl.BlockSpec((9, 2, BAND_T, BAND_T), lambda b, p, i: (0, p, 0, 0)),
        ],
        out_specs=pl.BlockSpec((None, tq, PAIR), lambda b, p, i: (b, i, p)),
        out_shape=jax.ShapeDtypeStruct((B, S, GROUP), BF16),
        compiler_params=pltpu.CompilerParams(
            dimension_semantics=("parallel", "parallel", "parallel"), vmem_limit_bytes=V7X_VMEM_LIMIT),
        name="diff_attn",
    )(consts, g_col, q_t, k, v_t, bias)


def _dil_kernel(qt_ref, k_ref, vt_ref, bias_ref, o_ref, *, nk):
    i = pl.program_id(2)
    t = qt_ref.shape[1]
    q_t = qt_ref[...]
    row = lax.broadcasted_iota(jnp.int32, q_t.shape, 0)
    j_lo = jnp.maximum(i - 4, 0)
    j_hi = jnp.minimum(i + 5, nk)
    heads = []
    for hh in range(2):
        qz = jnp.where((row // HEAD_DIM) == hh, q_t, jnp.zeros_like(q_t))

        def body(j, carry, qz=qz, hh=hh):
            ks = pl.multiple_of(j * t, t)
            k2 = k_ref[pl.ds(ks, t), :]
            v_t = vt_ref[j, hh * HEAD_DIM:(hh + 1) * HEAD_DIM, :]
            return _flash_step(k2, qz, v_t, *carry, bias=bias_ref[j - i + 4, hh])

        m, l, acc = lax.fori_loop(j_lo, j_hi, body, _flash_init(t))
        heads.append(acc / l)
    o_ref[...] = jnp.concatenate(heads, axis=0).T.astype(o_ref.dtype)


def _dil_call(q_t, k, v_t, bias):
    B, nq, _, t = q_t.shape
    S = k.shape[1]
    nk = v_t.shape[1]
    return pl.pallas_call(
        functools.partial(_dil_kernel, nk=nk),
        grid=(B, 2, nq),
        in_specs=[
            pl.BlockSpec((None, None, PAIR, t), lambda b, p, i: (b, i, p, 0)),
            pl.BlockSpec((None, S, PAIR), lambda b, p, i: (b, 0, p)),
            pl.BlockSpec((None, nk, PAIR, t), lambda b, p, i: (b, 0, p, 0)),
            pl.BlockSpec((9, 2, t, t), lambda b, p, i: (0, p, 0, 0)),
        ],
        out_specs=pl.BlockSpec((None, t, PAIR), lambda b, p, i: (b, i, p)),
        out_shape=jax.ShapeDtypeStruct((B, S, GROUP), BF16),
        compiler_params=pltpu.CompilerParams(
            dimension_semantics=("parallel", "parallel", "parallel"), vmem_limit_bytes=V7X_VMEM_LIMIT),
        name="dil_attn",
    )(q_t, k, v_t, bias)


def _mla_kernel(qt_ref, k_ref, vt_ref, o_ref, *, nk):
    tq = qt_ref.shape[1]
    tk = vt_ref.shape[2]
    heads = []
    for hh in range(2):
        qz = qt_ref[hh * PAIR:(hh + 1) * PAIR, :]

        def body(j, carry, qz=qz, hh=hh):
            ks = pl.multiple_of(j * tk, tk)
            k2 = k_ref[pl.ds(ks, tk), hh * PAIR:(hh + 1) * PAIR]
            v_t = vt_ref[j, hh * HEAD_DIM:(hh + 1) * HEAD_DIM, :]
            return _flash_step(k2, qz, v_t, *carry)

        m, l, acc = lax.fori_loop(0, nk, body, _flash_init(tq))
        heads.append(acc / l)
    o_ref[...] = jnp.concatenate(heads, axis=0).T.astype(o_ref.dtype)


def _mla_call(q_t, k, v_t):
    B, nq, _, tq = q_t.shape
    S = k.shape[1]
    nk, _, tk = v_t.shape[1:]
    return pl.pallas_call(
        functools.partial(_mla_kernel, nk=nk),
        grid=(B, 2, nq),
        in_specs=[
            pl.BlockSpec((None, None, 2 * PAIR, tq), lambda b, p, i: (b, i, p, 0)),
            pl.BlockSpec((None, S, 2 * PAIR), lambda b, p, i: (b, 0, p)),
            pl.BlockSpec((None, nk, PAIR, tk), lambda b, p, i: (b, 0, p, 0)),
        ],
        out_specs=pl.BlockSpec((None, tq, PAIR), lambda b, p, i: (b, i, p)),
        out_shape=jax.ShapeDtypeStruct((B, S, GROUP), BF16),
        compiler_params=pltpu.CompilerParams(
            dimension_semantics=("parallel", "parallel", "parallel"), vmem_limit_bytes=V7X_VMEM_LIMIT),
        name="mla_attn",
    )(q_t, k, v_t)


def _post_kernel(x_ref, oa_ref, ob_ref, oc_ref, od_ref, wo_ref, g_ref, wg_ref, wu_ref, wd_ref, gf_ref,
                 out_ref, acc_ref, h_ref, *, final):
    f = pl.program_id(1)

    @pl.when(f == 0)
    def _():
        x1 = x_ref[...]
        for gi, o_ref in enumerate((oa_ref, ob_ref, oc_ref, od_ref)):
            x1 = x1 + jnp.dot(o_ref[...], wo_ref[gi], preferred_element_type=F32)
        acc_ref[...] = x1
        h_ref[...] = _rms(x1, g_ref[...]).astype(BF16)

    h = h_ref[...]
    gate = jnp.dot(h, wg_ref[...], preferred_element_type=F32)
    up = jnp.dot(h, wu_ref[...], preferred_element_type=F32)
    act = (gate / (1.0 + jnp.exp(-gate))) * up
    acc_ref[...] += jnp.dot(act.astype(BF16), wd_ref[...], preferred_element_type=F32)

    @pl.when(f == pl.num_programs(1) - 1)
    def _():
        y = acc_ref[...]
        out_ref[...] = _rms(y, gf_ref[...]) if final else y


def _post_call(x, o_a, o_b, o_c, o_d, p, g_final, final):
    N, D = x.shape
    ts = TOK_TILE
    d_ff = p['w_gate'].shape[1]
    fc = d_ff // FF_SPLIT
    tok = lambda w: pl.BlockSpec((ts, w), lambda t, f: (t, 0))
    return pl.pallas_call(
        functools.partial(_post_kernel, final=final),
        grid=(N // ts, FF_SPLIT),
        in_specs=[tok(D), tok(GROUP), tok(GROUP), tok(GROUP), tok(GROUP),
                  pl.BlockSpec((4, GROUP, D), lambda t, f: (0, 0, 0)),
                  pl.BlockSpec((1, D), lambda t, f: (0, 0)),
                  pl.BlockSpec((D, fc), lambda t, f: (0, f)),
                  pl.BlockSpec((D, fc), lambda t, f: (0, f)),
                  pl.BlockSpec((fc, D), lambda t, f: (f, 0)),
                  pl.BlockSpec((1, D), lambda t, f: (0, 0))],
        out_specs=tok(D),
        out_shape=jax.ShapeDtypeStruct((N, D), F32),
        scratch_shapes=[pltpu.VMEM((ts, D), F32), pltpu.VMEM((ts, D), BF16)],
        compiler_params=pltpu.CompilerParams(
            dimension_semantics=("parallel", "arbitrary"), vmem_limit_bytes=V7X_VMEM_LIMIT),
        name="post_ffn",
    )(x, o_a, o_b, o_c, o_d, p['w_o'], p['g_ffn'], p['w_gate'], p['w_up'], p['w_down'], g_final)


def _t5_bucket(rel):
    nb = T5_BUCKETS // 2
    max_exact = nb // 2
    side = jnp.where(rel > 0, nb, 0)
    n = jnp.abs(rel)
    large = max_exact + (jnp.log(jnp.maximum(n, 1).astype(F32) / max_exact)
                         / math.log(T5_MAX_DIST / max_exact) * (nb - max_exact)).astype(jnp.int32)
    large = jnp.minimum(large, nb - 1)
    return side + jnp.where(n < max_exact, n, large)


def _toeplitz_tiles(vec_fn, t, n_side):
    e = jnp.arange(-n_side, n_side + 1)[:, None, None]
    rel = e * t + jnp.arange(t)[None, :, None] - jnp.arange(t)[None, None, :]
    return jnp.moveaxis(vec_fn(rel), -1, 1)


def _diff_bias(t5_diff):
    def vec(rel):
        return t5_diff[_t5_bucket(rel)].astype(F32) * LOG2E
    tiles = _toeplitz_tiles(vec, BAND_T, 4)
    far = vec(jnp.array([-2 * T5_MAX_DIST, 2 * T5_MAX_DIST]))
    return tiles, far


def _dil_bias(t5_dil):
    def vec(rel):
        n = jnp.abs(rel)
        mult = jnp.zeros(rel.shape, F32)
        for window, dil in DIL_PATTERNS:
            mult = mult + ((rel % dil == 0) & (n <= window // 2)).astype(F32)
        b = t5_dil[_t5_bucket(rel)].astype(F32) + jnp.log(jnp.maximum(mult, 1.0))[..., None]
        return jnp.where((mult > 0)[..., None], b * LOG2E, NEG)
    return _toeplitz_tiles(vec, BAND_T, 4)


def _na_bias(rpb):
    d = jnp.arange(NA_ROWS)[:, None]
    ridx = jnp.arange(NA_ROWS)[None, :] - d + NA_ROWS - 1
    qc = jnp.arange(GRID_W)[:, None]
    kc = jnp.arange(GRID_W)[None, :]
    c0 = jnp.clip(qc - NA_COLS // 2, 0, GRID_W - NA_COLS)
    valid = (kc >= c0) & (kc < c0 + NA_COLS)
    cidx = jnp.clip(kc - qc + NA_COLS - 1, 0, 2 * NA_COLS - 2)
    vals = rpb.astype(F32)[:, ridx[:, :, None, None], cidx[None, None]]
    vals = jnp.where(valid[None, None, None], vals * LOG2E, NEG)
    vals = vals.transpose(1, 0, 3, 2, 4)
    return vals.reshape(NA_ROWS, N_HEADS, GRID_W, NA_ROWS * GRID_W)


def _rope_tables(S):
    inv_freq = ROPE_THETA ** (-jnp.arange(MLA_HALF, dtype=F32) / MLA_HALF)
    ang = jnp.arange(S).astype(F32)[:, None] * inv_freq[None, :]
    cos, sin = jnp.cos(ang), jnp.sin(ang)
    csn = jnp.concatenate([cos, cos, -sin, sin, jnp.zeros((S, PAIR - 4 * MLA_HALF), F32)], axis=1)
    nt = S // TOK_TILE
    to_t = lambda a: a.reshape(nt, TOK_TILE, MLA_HALF).transpose(0, 2, 1)
    return {'csn': csn, 'cos_t': to_t(cos), 'sin_t': to_t(sin)}


def _layer_weights(w_in, w_uq, w_ukv, w_o, w_gate, w_up, w_down, g_q, g_kv, g_ffn):
    G = GROUP
    col = lambda i: w_in[:, i * G:(i + 1) * G]
    cq, ckv, kr = w_in[:, 9 * G:10 * G], w_in[:, 10 * G:10 * G + 128], w_in[:, 10 * G + 128:]
    kr_swap = jnp.concatenate([kr[:, MLA_HALF:], kr[:, :MLA_HALF]], axis=1)
    kr2 = jnp.concatenate([kr, kr_swap, jnp.zeros((w_in.shape[0], PAIR - 2 * MLA_ROPE), w_in.dtype)], axis=1)
    w_nat = jnp.concatenate([col(0), col(1), col(2), col(4), col(7), cq, ckv, kr2], axis=1)
    w_t = jnp.concatenate([col(3), col(5), col(6), col(8)], axis=1).T
    uq = w_uq.reshape(w_uq.shape[0], N_HEADS, MLA_QK)
    uq = jnp.pad(uq, ((0, 0), (0, 0), (0, PAIR - MLA_QK))).reshape(w_uq.shape[0], N_HEADS * PAIR)
    ukv = w_ukv.reshape(w_ukv.shape[0], N_HEADS, 2 * HEAD_DIM)
    uk = ukv.at[:, :, HEAD_DIM:].set(0.0).reshape(w_ukv.shape[0], N_HEADS * PAIR)
    uv = ukv[:, :, HEAD_DIM:].reshape(w_ukv.shape[0], GROUP)
    place = jnp.zeros((PAIR, N_HEADS * PAIR), F32)
    r = jnp.arange(MLA_ROPE)
    for hd in range(N_HEADS):
        place = place.at[r, hd * PAIR + HEAD_DIM + r].set(1.0)
    b = lambda a: a.astype(BF16)
    row = lambda a: a.astype(F32)[None, :]
    return {
        'w_nat': b(w_nat), 'w_t': b(w_t), 'w_uq_t': b(uq.T), 'w_uk': b(uk), 'w_uv_t': b(uv.T),
        'place': b(place), 'g_q': row(g_q), 'g_kv': row(g_kv), 'g_ffn': row(g_ffn),
        'w_o': b(w_o.reshape(4, GROUP, w_o.shape[1])), 'w_gate': b(w_gate), 'w_up': b(w_up),
        'w_down': b(w_down),
    }


def kernel(x, attn_norm, w_in, na_rpb, diff_lambda, diff_subln, mla_q_norm, w_uq, mla_kv_norm, w_ukv,
           t5_table, w_o, ffn_norm, w_gate, w_up, w_down, final_norm):
    B, S, D = x.shape
    depth = w_in.shape[0]
    rope = _rope_tables(S)
    diff_tiles, diff_far = _diff_bias(t5_table[:, :N_HEADS])
    dil_tiles = _dil_bias(t5_table[:, N_HEADS:])
    g_final = final_norm.astype(F32)[None, :]

    for l in range(depth):
        p = _layer_weights(w_in[l], w_uq[l], w_ukv[l], w_o[l], w_gate[l], w_up[l], w_down[l],
                           mla_q_norm[l], mla_kv_norm[l], ffn_norm[l])
        (qa, ka, va, kb, kc, kd, qb_t, vb_t, qc_t, vc_t, qd_t, vd_t) = _proj_call(
            x, attn_norm[l].astype(F32)[None, :], p, rope)

        o_a = _na_call(qa, ka, va, _na_bias(na_rpb[l]))

        lam_init = 0.8 - 0.6 * math.exp(-0.3 * l)
        lq1, lk1, lq2, lk2 = [diff_lambda[l, j].astype(F32) for j in range(4)]
        lam = jnp.exp(jnp.sum(lq1 * lk1)) - jnp.exp(jnp.sum(lq2 * lk2)) + lam_init
        far = diff_far.reshape(2, 2, 2).transpose(1, 0, 2).reshape(2, 4)
        consts = jnp.concatenate([jnp.broadcast_to(lam, (2, 1)), far, jnp.zeros((2, 3), F32)], axis=1)
        consts = jnp.broadcast_to(consts[:, :, None], (2, 8, 128))
        g_col = jnp.broadcast_to(diff_subln[l].astype(F32)[:, None], (HEAD_DIM, 128))
        o_b = _diff_call(consts, g_col, qb_t, kb, vb_t, diff_tiles, 1.0 - lam_init)

        o_c = _dil_call(qc_t, kc, vc_t, dil_tiles)
        o_d = _mla_call(qd_t, kd, vd_t)

        flat = lambda a: a.reshape(B * S, a.shape[-1])
        x = _post_call(flat(x), flat(o_a), flat(o_b), flat(o_c), flat(o_d), p, g_final,
                       final=(l == depth - 1)).reshape(B, S, D)
    return x
```

```python
import functools
import math

import jax
import jax.numpy as jnp
from jax import lax
from jax.experimental import pallas as pl
from jax.experimental.pallas import tpu as pltpu

F32 = jnp.float32
BF16 = jnp.bfloat16

N_HEADS = 4
HEAD_DIM = 64
GROUP = N_HEADS * HEAD_DIM
PAIR = 2 * HEAD_DIM
GRID_W = 64
NA_ROWS = 8
NA_COLS = 16
DIFF_QK = HEAD_DIM // 2
DIL_PATTERNS = ((128, 1), (512, 4), (2048, 16))
MLA_ROPE = HEAD_DIM // 2
MLA_HALF = MLA_ROPE // 2
MLA_QK = HEAD_DIM + MLA_ROPE
ROPE_THETA = 10000.0
T5_BUCKETS = 32
T5_MAX_DIST = 1024
EPS = 1e-6
LOG2E = 1.4426950408889634
NEG = -1e30
SUM_ROWS = 16

V7X_VMEM_LIMIT = 56 * 1024 * 1024

TOK_TILE = 512
FULL_T = 512
BAND_T = 256
DIL_BAND = -(-max(w // 2 for w, _ in DIL_PATTERNS) // BAND_T)
MLA_Q_BLOCKS = 2
NA_GROUP = 8
FF_SPLIT = 2


def _rms(x, g):
    return x * lax.rsqrt(jnp.mean(x * x, axis=-1, keepdims=True) + EPS) * g


def _nt(a, b):
    return lax.dot_general(a, b, (((1,), (1,)), ((), ())), preferred_element_type=F32)


def _flash_pipelined(streams, ranges, bufs):
    tk, tq = bufs[0][0].shape
    first = ranges[0][0]
    last = ranges[-1][1] - 1
    ones = jnp.ones((SUM_ROWS, tk), BF16)

    def scores(st, j, bias_fn, s_ref):
        s = jnp.dot(st['load_k'](j), st['qz'], preferred_element_type=F32)
        if bias_fn is not None:
            s = s + bias_fn(j)
        s_ref[...] = s
        return jnp.max(s, axis=0, keepdims=True)

    def accumulate(st, buf, j, par):
        acc_ref, stat_ref = buf[4], buf[5]
        v_aug = jnp.concatenate([st['load_vt'](j), ones], axis=0)
        acc_ref[...] = (stat_ref[_ALPHA + par:_ALPHA + par + 1, :] * acc_ref[...]
                        + jnp.dot(v_aug, buf[2 + par][...], preferred_element_type=F32))

    def step(par, r, t):
        for st, buf in zip(streams, bufs):
            stat_ref = buf[5]
            m = stat_ref[_MAX:_MAX + 1, :]
            c = stat_ref[_CONST:_CONST + 1, :]
            m_new = jnp.maximum(m, stat_ref[_CMAX:_CMAX + 1, :] + c)
            buf[3 - par][...] = jnp.exp2((buf[1 - par][...] - (m_new - c)).astype(BF16))
            stat_ref[_ALPHA + 1 - par:_ALPHA + 2 - par, :] = jnp.exp2(m - m_new)
            stat_ref[_MAX:_MAX + 1, :] = m_new
        for st, buf in zip(streams, bufs):
            stat_ref = buf[5]
            stat_ref[_CMAX:_CMAX + 1, :] = scores(st, jnp.minimum(t, last), st['bias_fns'][r], buf[par])
            stat_ref[_CONST:_CONST + 1, :] = jnp.zeros((1, tq), F32) + st['shifts'][r]
        for st, buf in zip(streams, bufs):
            accumulate(st, buf, jnp.maximum(t - 2, first), par)

    for st, buf in zip(streams, bufs):
        acc_ref, stat_ref = buf[4], buf[5]
        cmax = scores(st, first, st.get('first_bias', st['bias_fns'][0]), buf[0])
        buf[3][...] = jnp.zeros_like(buf[3])
        acc_ref[...] = jnp.zeros_like(acc_ref)
        stat_ref[_ALPHA:_ALPHA + 2, :] = jnp.ones((2, tq), F32)
        stat_ref[_MAX:_MAX + 1, :] = jnp.full((1, tq), NEG, F32)
        stat_ref[_CMAX:_CMAX + 1, :] = cmax
        stat_ref[_CONST:_CONST + 1, :] = (jnp.zeros((1, tq), F32)
                                          + (0.0 if 'first_bias' in st else st['shifts'][0]))

    def by_parity(j, fn):
        par = (j - first) % 2
        if isinstance(par, int):
            fn(par)
        else:
            pl.when(par == 0)(functools.partial(fn, 0))
            pl.when(par == 1)(functools.partial(fn, 1))

    for r, (lo, hi) in enumerate(ranges):

        def body(t, c, r=r):
            by_parity(t, lambda par: step(par, r, t))
            return c

        lax.fori_loop(jnp.maximum(lo, first + 1), hi + (r == len(ranges) - 1), body, 0)

    def finish(par):
        for st, buf in zip(streams, bufs):
            accumulate(st, buf, last, par)

    by_parity(last, finish)
    outs = []
    for buf in bufs:
        acc = buf[4][...]
        outs.append(acc[:HEAD_DIM] / acc[HEAD_DIM:HEAD_DIM + 1])
    return outs


_ALPHA, _MAX, _CMAX, _CONST = 0, 2, 3, 4
_BUFS_PER_STREAM = 6


def _flash_scratch(n_streams, tk, tq):
    per_stream = [pltpu.VMEM((tk, tq), F32), pltpu.VMEM((tk, tq), F32),
                  pltpu.VMEM((tk, tq), BF16), pltpu.VMEM((tk, tq), BF16),
                  pltpu.VMEM((HEAD_DIM + SUM_ROWS, tq), F32), pltpu.VMEM((8, tq), F32)]
    return per_stream * n_streams


def _flash_bufs(refs):
    n = _BUFS_PER_STREAM
    return [tuple(refs[n * k:n * k + n]) for k in range(len(refs) // n)]


def _store_t_blocks(ref, val, blk):
    for c in range(val.shape[1] // blk):
        ref[c] = val[:, c * blk:(c + 1) * blk].astype(ref.dtype)


def _proj_kernel(x_ref, g_ref, wnat_ref, wt_ref, gq_ref, wuqt_ref, gkv_ref, wuk_ref, wuvt_ref,
                 place_ref, csn_ref, cost_ref, sint_ref,
                 qa_ref, ka_ref, va_ref, kb_ref, kc_ref, kd_ref,
                 qbt_ref, vbt_ref, qct_ref, vct_ref, qdt_ref, vdt_ref):
    h = _rms(x_ref[...], g_ref[...]).astype(BF16)
    nat = jnp.dot(h, wnat_ref[...], preferred_element_type=F32)
    qa_ref[...] = (nat[:, 0:256] * (HEAD_DIM ** -0.5 * LOG2E)).astype(BF16)
    ka_ref[...] = nat[:, 256:512].astype(BF16)
    va_ref[...] = nat[:, 512:768].astype(BF16)
    kb_ref[...] = nat[:, 768:1024].astype(BF16)
    kc_ref[...] = nat[:, 1024:1280].astype(BF16)
    cq = nat[:, 1280:1536]
    ckv = nat[:, 1536:1664]
    kr2 = nat[:, 1664:1792]

    tr = _nt(wt_ref[...], h)
    _store_t_blocks(qbt_ref, tr[0:256] * (DIFF_QK ** -0.5 * LOG2E), FULL_T)
    _store_t_blocks(vbt_ref, tr[256:512], FULL_T)
    _store_t_blocks(qct_ref, tr[512:768] * (HEAD_DIM ** -0.5 * LOG2E), BAND_T)
    _store_t_blocks(vct_ref, tr[768:1024], BAND_T)

    cqn = _rms(cq, gq_ref[...]).astype(BF16)
    qd_t = _nt(wuqt_ref[...], cqn)
    cos_t = cost_ref[...]
    sin_t = sint_ref[...]
    pieces = []
    for hd in range(N_HEADS):
        b0 = hd * PAIR
        x1 = qd_t[b0 + 64:b0 + 80]
        x2 = qd_t[b0 + 80:b0 + 96]
        pieces += [qd_t[b0:b0 + 64], x1 * cos_t - x2 * sin_t, x1 * sin_t + x2 * cos_t,
                   qd_t[b0 + 96:b0 + 128]]
    qd_t = jnp.concatenate(pieces, axis=0) * (MLA_QK ** -0.5 * LOG2E)
    _store_t_blocks(qdt_ref, qd_t, FULL_T)

    ckvn = _rms(ckv, gkv_ref[...]).astype(BF16)
    k_nope = jnp.dot(ckvn, wuk_ref[...], preferred_element_type=F32)
    t = kr2 * csn_ref[...]
    roped = t + pltpu.roll(t, 96, 1)
    kd = k_nope + jnp.dot(roped.astype(BF16), place_ref[...], preferred_element_type=F32)
    kd_ref[...] = kd.astype(BF16)
    _store_t_blocks(vdt_ref, _nt(wuvt_ref[...], ckvn), FULL_T)


def _proj_call(x, g, p, rope):
    B, S, D = x.shape
    ts = TOK_TILE
    nt = S // ts
    nf, nb = ts // FULL_T, ts // BAND_T

    def full(a):
        return pl.BlockSpec(a.shape, lambda b, t: (0,) * a.ndim)

    nat_spec = lambda w: pl.BlockSpec((None, ts, w), lambda b, t: (b, t, 0))
    tb_spec = lambda n, r, blk: pl.BlockSpec((None, n, r, blk), lambda b, t: (b, t, 0, 0))
    nat_shape = lambda w: jax.ShapeDtypeStruct((B, S, w), BF16)
    tb_shape = lambda r, blk: jax.ShapeDtypeStruct((B, S // blk, r, blk), BF16)

    weights = [g, p['w_nat'], p['w_t'], p['g_q'], p['w_uq_t'], p['g_kv'], p['w_uk'], p['w_uv_t'], p['place']]
    in_specs = ([nat_spec(D)] + [full(a) for a in weights] +
                [pl.BlockSpec((ts, PAIR), lambda b, t: (t, 0)),
                 pl.BlockSpec((None, MLA_HALF, ts), lambda b, t: (t, 0, 0)),
                 pl.BlockSpec((None, MLA_HALF, ts), lambda b, t: (t, 0, 0))])
    out_shape = ([nat_shape(GROUP)] * 5 + [nat_shape(2 * GROUP)] +
                 [tb_shape(GROUP, FULL_T)] * 2 + [tb_shape(GROUP, BAND_T)] * 2 +
                 [tb_shape(2 * GROUP, FULL_T), tb_shape(GROUP, FULL_T)])
    out_specs = ([nat_spec(GROUP)] * 5 + [nat_spec(2 * GROUP)] +
                 [tb_spec(nf, GROUP, FULL_T)] * 2 + [tb_spec(nb, GROUP, BAND_T)] * 2 +
                 [tb_spec(nf, 2 * GROUP, FULL_T), tb_spec(nf, GROUP, FULL_T)])
    return pl.pallas_call(
        _proj_kernel,
        grid=(B, nt),
        in_specs=in_specs,
        out_specs=out_specs,
        out_shape=out_shape,
        compiler_params=pltpu.CompilerParams(
            dimension_semantics=("parallel", "parallel"), vmem_limit_bytes=V7X_VMEM_LIMIT),
        name="proj",
    )(x, *weights, rope['csn'], rope['cos_t'], rope['sin_t'])


def _na_kernel(q_ref, kp_ref, kc_ref, kn_ref, vp_ref, vc_ref, vn_ref, bias_ref, o_ref, kk_ref, vv_ref,
               *, n_rows):
    g = pl.program_id(1)
    blk = NA_GROUP * GRID_W
    win = NA_ROWS * GRID_W
    for c, (kr, vr) in enumerate(((kp_ref, vp_ref), (kc_ref, vc_ref), (kn_ref, vn_ref))):
        kk_ref[c * blk:(c + 1) * blk, :] = kr[...]
        vv_ref[c * blk:(c + 1) * blk, :] = vr[...]
    lane = lax.broadcasted_iota(jnp.int32, (GRID_W, PAIR), 1)

    def row_body(j, carry):
        r = g * NA_GROUP + j
        r0 = jnp.clip(r - NA_ROWS // 2, 0, n_rows - NA_ROWS)
        start = pl.multiple_of((r0 - (g - 1) * NA_GROUP) * GRID_W, GRID_W)
        didx = r - r0
        qs = pl.multiple_of(j * GRID_W, GRID_W)
        q_row = q_ref[pl.ds(qs, GRID_W), :]
        k_win = kk_ref[pl.ds(start, win), :]
        v_win = vv_ref[pl.ds(start, win), :]
        outs = []
        for pr in range(2):
            q2 = q_row[:, pr * PAIR:(pr + 1) * PAIR]
            k2 = k_win[:, pr * PAIR:(pr + 1) * PAIR]
            v2 = v_win[:, pr * PAIR:(pr + 1) * PAIR]
            o_pair = None
            for hh in range(2):
                mine = (lane // HEAD_DIM) == hh
                qz = jnp.where(mine, q2, jnp.zeros_like(q2))
                s = _nt(qz, k2) + bias_ref[didx, 2 * pr + hh]
                m = jnp.max(s, axis=-1, keepdims=True)
                e = jnp.exp2(s - m)
                den = jnp.sum(e, axis=-1, keepdims=True)
                o2 = jnp.dot(e.astype(BF16), v2, preferred_element_type=F32) / den
                o_pair = o2 if o_pair is None else jnp.where(mine, o2, o_pair)
            outs.append(o_pair)
        o_ref[pl.ds(qs, GRID_W), :] = jnp.concatenate(outs, axis=1).astype(o_ref.dtype)
        return carry

    lax.fori_loop(0, NA_GROUP, row_body, 0)


def _na_call(q, k, v, bias):
    B, S, _ = q.shape
    n_rows = S // GRID_W
    blk = NA_GROUP * GRID_W
    ng = S // blk
    cur = pl.BlockSpec((None, blk, GROUP), lambda b, g: (b, g, 0))
    prv = pl.BlockSpec((None, blk, GROUP), lambda b, g: (b, jnp.maximum(g - 1, 0), 0))
    nxt = pl.BlockSpec((None, blk, GROUP), lambda b, g: (b, jnp.minimum(g + 1, ng - 1), 0))
    return pl.pallas_call(
        functools.partial(_na_kernel, n_rows=n_rows),
        grid=(B, ng),
        in_specs=[cur, prv, cur, nxt, prv, cur, nxt,
                  pl.BlockSpec(bias.shape, lambda b, g: (0, 0, 0, 0))],
        out_specs=cur,
        out_shape=jax.ShapeDtypeStruct((B, S, GROUP), BF16),
        scratch_shapes=[pltpu.VMEM((3 * blk, GROUP), BF16), pltpu.VMEM((3 * blk, GROUP), BF16)],
        compiler_params=pltpu.CompilerParams(
            dimension_semantics=("parallel", "parallel"), vmem_limit_bytes=V7X_VMEM_LIMIT),
        name="na_attn",
    )(q, k, k, k, v, v, v, bias)


def _diff_kernel(c_ref, g_ref, qt_ref, k_ref, vt_ref, bias_ref, o_ref, *scratch, nk, lam_scale):
    i = pl.program_id(2)
    tk = vt_ref.shape[2]
    q_t = qt_ref[...]
    row = lax.broadcasted_iota(jnp.int32, q_t.shape, 0)
    lam = c_ref[0:1, 0:1]
    j_lo = jnp.maximum(i - 2, 0)
    j_hi = jnp.minimum(i + 3, nk)
    ranges = ((0, j_lo), (j_lo, j_hi), (j_hi, nk))

    def load_k(j):
        return k_ref[pl.ds(pl.multiple_of(j * tk, tk), tk), :]

    streams = []
    for hh in range(2):

        def load_vt(j, hh=hh):
            return vt_ref[j, hh * HEAD_DIM:(hh + 1) * HEAD_DIM, :]

        def near_bias(j, hh=hh):
            e0 = 2 * (j - i) + 4

            def tile(d):
                return bias_ref[jnp.clip(e0 + d, 0, 8), hh]
            t0 = tile(0)
            return jnp.concatenate(
                [jnp.concatenate([t0, tile(-1)], axis=1),
                 jnp.concatenate([tile(1), t0], axis=1)], axis=0)

        for c in range(2):
            r_lo = hh * HEAD_DIM + c * DIFF_QK
            qz = jnp.where((row >= r_lo) & (row < r_lo + DIFF_QK), q_t, jnp.zeros_like(q_t))
            streams.append(dict(qz=qz, load_k=load_k, load_vt=load_vt,
                                shifts=(c_ref[1 + hh:2 + hh, 0:1], 0.0, c_ref[3 + hh:4 + hh, 0:1]),
                                bias_fns=(None, near_bias, None), first_bias=near_bias))
    maps = _flash_pipelined(streams, ranges, _flash_bufs(scratch))
    heads = []
    for hh in range(2):
        o = maps[2 * hh] - lam * maps[2 * hh + 1]
        y = o * lax.rsqrt(jnp.mean(o * o, axis=0, keepdims=True) + EPS) * g_ref[:, 0:1]
        heads.append(y * lam_scale)
    o_ref[...] = jnp.concatenate(heads, axis=0).T.astype(o_ref.dtype)


def _diff_call(consts, g_col, q_t, k, v_t, bias, lam_scale):
    B, nq, _, tq = q_t.shape
    S = k.shape[1]
    nk, _, tk = v_t.shape[1:]
    return pl.pallas_call(
        functools.partial(_diff_kernel, nk=nk, lam_scale=lam_scale),
        grid=(B, 2, nq),
        in_specs=[
            pl.BlockSpec((None, 8, 128), lambda b, p, i: (p, 0, 0)),
            pl.BlockSpec(g_col.shape, lambda b, p, i: (0, 0)),
            pl.BlockSpec((None, None, PAIR, tq), lambda b, p, i: (b, i, p, 0)),
            pl.BlockSpec((None, S, PAIR), lambda b, p, i: (b, 0, p)),
            pl.BlockSpec((None, nk, PAIR, tk), lambda b, p, i: (b, 0, p, 0)),
            pl.BlockSpec((9, 2, BAND_T, BAND_T), lambda b, p, i: (0, p, 0, 0)),
        ],
        out_specs=pl.BlockSpec((None, tq, PAIR), lambda b, p, i: (b, i, p)),
        out_shape=jax.ShapeDtypeStruct((B, S, GROUP), BF16),
        scratch_shapes=_flash_scratch(4, tk, tq),
        compiler_params=pltpu.CompilerParams(
            dimension_semantics=("parallel", "parallel", "parallel"), vmem_limit_bytes=V7X_VMEM_LIMIT),
        name="diff_attn",
    )(consts, g_col, q_t, k, v_t, bias)


def _dil_kernel(qt_ref, k_ref, vt_ref, bias_ref, o_ref, *scratch, nk):
    i = pl.program_id(2)
    t = qt_ref.shape[1]
    q_t = qt_ref[...]
    row = lax.broadcasted_iota(jnp.int32, q_t.shape, 0)
    j_lo = jnp.maximum(i - DIL_BAND, 0)
    j_hi = jnp.minimum(i + DIL_BAND + 1, nk)

    def load_k(j):
        return k_ref[pl.ds(pl.multiple_of(j * t, t), t), :]

    streams = []
    for hh in range(2):

        def load_vt(j, hh=hh):
            return vt_ref[j, hh * HEAD_DIM:(hh + 1) * HEAD_DIM, :]

        def bias(j, hh=hh):
            return bias_ref[j - i + DIL_BAND, hh]

        qz = jnp.where((row // HEAD_DIM) == hh, q_t, jnp.zeros_like(q_t))
        streams.append(dict(qz=qz, load_k=load_k, load_vt=load_vt, shifts=(0.0,), bias_fns=(bias,)))
    heads = _flash_pipelined(streams, ((j_lo, j_hi),), _flash_bufs(scratch))
    o_ref[...] = jnp.concatenate(heads, axis=0).T.astype(o_ref.dtype)


def _dil_call(q_t, k, v_t, bias):
    B, nq, _, t = q_t.shape
    S = k.shape[1]
    nk = v_t.shape[1]
    return pl.pallas_call(
        functools.partial(_dil_kernel, nk=nk),
        grid=(B, 2, nq),
        in_specs=[
            pl.BlockSpec((None, None, PAIR, t), lambda b, p, i: (b, i, p, 0)),
            pl.BlockSpec((None, S, PAIR), lambda b, p, i: (b, 0, p)),
            pl.BlockSpec((None, nk, PAIR, t), lambda b, p, i: (b, 0, p, 0)),
            pl.BlockSpec((2 * DIL_BAND + 1, 2, t, t), lambda b, p, i: (0, p, 0, 0)),
        ],
        out_specs=pl.BlockSpec((None, t, PAIR), lambda b, p, i: (b, i, p)),
        out_shape=jax.ShapeDtypeStruct((B, S, GROUP), BF16),
        scratch_shapes=_flash_scratch(2, t, t),
        compiler_params=pltpu.CompilerParams(
            dimension_semantics=("parallel", "parallel", "parallel"), vmem_limit_bytes=V7X_VMEM_LIMIT),
        name="dil_attn",
    )(q_t, k, v_t, bias)


def _mla_kernel(qt_ref, k_ref, vt_ref, o_ref, *scratch, nk):
    tk = vt_ref.shape[2]
    tq = qt_ref.shape[2]
    streams = []
    for qi in range(MLA_Q_BLOCKS):
        for hh in range(2):

            def load_k(j, hh=hh):
                return k_ref[pl.ds(pl.multiple_of(j * tk, tk), tk), hh * PAIR:(hh + 1) * PAIR]

            def load_vt(j, hh=hh):
                return vt_ref[j, hh * HEAD_DIM:(hh + 1) * HEAD_DIM, :]

            qz = qt_ref[qi, hh * PAIR:(hh + 1) * PAIR, :]
            streams.append(dict(qz=qz, load_k=load_k, load_vt=load_vt, shifts=(0.0,), bias_fns=(None,)))
    outs = _flash_pipelined(streams, ((0, nk),), _flash_bufs(scratch))
    for qi in range(MLA_Q_BLOCKS):
        o_t = jnp.concatenate(outs[2 * qi:2 * qi + 2], axis=0)
        o_ref[qi * tq:(qi + 1) * tq, :] = o_t.T.astype(o_ref.dtype)


def _mla_call(q_t, k, v_t):
    B, nq, _, tq = q_t.shape
    S = k.shape[1]
    nk, _, tk = v_t.shape[1:]
    nqb = MLA_Q_BLOCKS
    return pl.pallas_call(
        functools.partial(_mla_kernel, nk=nk),
        grid=(B, 2, nq // nqb),
        in_specs=[
            pl.BlockSpec((None, nqb, 2 * PAIR, tq), lambda b, p, i: (b, i, p, 0)),
            pl.BlockSpec((None, S, 2 * PAIR), lambda b, p, i: (b, 0, p)),
            pl.BlockSpec((None, nk, PAIR, tk), lambda b, p, i: (b, 0, p, 0)),
        ],
        out_specs=pl.BlockSpec((None, nqb * tq, PAIR), lambda b, p, i: (b, i, p)),
        out_shape=jax.ShapeDtypeStruct((B, S, GROUP), BF16),
        scratch_shapes=_flash_scratch(2 * nqb, tk, tq),
        compiler_params=pltpu.CompilerParams(
            dimension_semantics=("parallel", "parallel", "parallel"), vmem_limit_bytes=V7X_VMEM_LIMIT),
        name="mla_attn",
    )(q_t, k, v_t)


def _post_kernel(x_ref, oa_ref, ob_ref, oc_ref, od_ref, wo_ref, g_ref, wg_ref, wu_ref, wd_ref, gf_ref,
                 out_ref, acc_ref, h_ref, *, final):
    f = pl.program_id(1)

    @pl.when(f == 0)
    def _():
        x1 = x_ref[...]
        for gi, o_ref in enumerate((oa_ref, ob_ref, oc_ref, od_ref)):
            x1 = x1 + jnp.dot(o_ref[...], wo_ref[gi], preferred_element_type=F32)
        acc_ref[...] = x1
        h_ref[...] = _rms(x1, g_ref[...]).astype(BF16)

    h = h_ref[...]
    gate = jnp.dot(h, wg_ref[...], preferred_element_type=F32)
    up = jnp.dot(h, wu_ref[...], preferred_element_type=F32)
    act = (gate / (1.0 + jnp.exp(-gate))) * up
    acc_ref[...] += jnp.dot(act.astype(BF16), wd_ref[...], preferred_element_type=F32)

    @pl.when(f == pl.num_programs(1) - 1)
    def _():
        y = acc_ref[...]
        out_ref[...] = _rms(y, gf_ref[...]) if final else y


def _post_call(x, o_a, o_b, o_c, o_d, p, g_final, final):
    N, D = x.shape
    ts = TOK_TILE
    d_ff = p['w_gate'].shape[1]
    fc = d_ff // FF_SPLIT
    tok = lambda w: pl.BlockSpec((ts, w), lambda t, f: (t, 0))
    return pl.pallas_call(
        functools.partial(_post_kernel, final=final),
        grid=(N // ts, FF_SPLIT),
        in_specs=[tok(D), tok(GROUP), tok(GROUP), tok(GROUP), tok(GROUP),
                  pl.BlockSpec((4, GROUP, D), lambda t, f: (0, 0, 0)),
                  pl.BlockSpec((1, D), lambda t, f: (0, 0)),
                  pl.BlockSpec((D, fc), lambda t, f: (0, f)),
                  pl.BlockSpec((D, fc), lambda t, f: (0, f)),
                  pl.BlockSpec((fc, D), lambda t, f: (f, 0)),
                  pl.BlockSpec((1, D), lambda t, f: (0, 0))],
        out_specs=tok(D),
        out_shape=jax.ShapeDtypeStruct((N, D), F32),
        scratch_shapes=[pltpu.VMEM((ts, D), F32), pltpu.VMEM((ts, D), BF16)],
        compiler_params=pltpu.CompilerParams(
            dimension_semantics=("parallel", "arbitrary"), vmem_limit_bytes=V7X_VMEM_LIMIT),
        name="post_ffn",
    )(x, o_a, o_b, o_c, o_d, p['w_o'], p['g_ffn'], p['w_gate'], p['w_up'], p['w_down'], g_final)


def _t5_bucket(rel):
    nb = T5_BUCKETS // 2
    max_exact = nb // 2
    side = jnp.where(rel > 0, nb, 0)
    n = jnp.abs(rel)
    large = max_exact + (jnp.log(jnp.maximum(n, 1).astype(F32) / max_exact)
                         / math.log(T5_MAX_DIST / max_exact) * (nb - max_exact)).astype(jnp.int32)
    large = jnp.minimum(large, nb - 1)
    return side + jnp.where(n < max_exact, n, large)


def _lookup(table, idx):
    onehot = idx[..., None, None] == jnp.arange(table.shape[0])[:, None]
    return jnp.sum(jnp.where(onehot, table.astype(F32), 0.0), axis=-2)


def _toeplitz_tiles(vec_fn, t, n_side):
    e = jnp.arange(-n_side, n_side + 1)[:, None, None]
    rel = e * t + jnp.arange(t)[None, :, None] - jnp.arange(t)[None, None, :]
    return jnp.moveaxis(vec_fn(rel), -1, 1)


def _diff_bias(t5_diff):
    def vec(rel):
        return _lookup(t5_diff, _t5_bucket(rel)) * LOG2E
    tiles = _toeplitz_tiles(vec, BAND_T, 4)
    far = vec(jnp.array([-2 * T5_MAX_DIST, 2 * T5_MAX_DIST]))
    return tiles, far


def _dil_bias(t5_dil):
    def vec(rel):
        n = jnp.abs(rel)
        mult = jnp.zeros(rel.shape, F32)
        for window, dil in DIL_PATTERNS:
            mult = mult + ((rel % dil == 0) & (n <= window // 2)).astype(F32)
        b = _lookup(t5_dil, _t5_bucket(rel)) + jnp.log(jnp.maximum(mult, 1.0))[..., None]
        return jnp.where((mult > 0)[..., None], b * LOG2E, NEG)
    return _toeplitz_tiles(vec, BAND_T, DIL_BAND)


def _na_bias(rpb):
    d = jnp.arange(NA_ROWS)[:, None]
    ridx = jnp.arange(NA_ROWS)[None, :] - d + NA_ROWS - 1
    qc = jnp.arange(GRID_W)[:, None]
    kc = jnp.arange(GRID_W)[None, :]
    c0 = jnp.clip(qc - NA_COLS // 2, 0, GRID_W - NA_COLS)
    valid = (kc >= c0) & (kc < c0 + NA_COLS)
    cidx = jnp.clip(kc - qc + NA_COLS - 1, 0, 2 * NA_COLS - 2)
    rows = rpb.astype(F32)[:, ridx]
    vals = _lookup(rows.reshape(-1, rows.shape[-1]).T, cidx)
    vals = jnp.where(valid[:, :, None], vals * LOG2E, NEG)
    vals = vals.reshape(GRID_W, GRID_W, N_HEADS, NA_ROWS, NA_ROWS).transpose(3, 2, 0, 4, 1)
    return vals.reshape(NA_ROWS, N_HEADS, GRID_W, NA_ROWS * GRID_W)


def _rope_tables(S):
    inv_freq = ROPE_THETA ** (-jnp.arange(MLA_HALF, dtype=F32) / MLA_HALF)
    ang = jnp.arange(S).astype(F32)[:, None] * inv_freq[None, :]
    cos, sin = jnp.cos(ang), jnp.sin(ang)
    csn = jnp.concatenate([cos, cos, -sin, sin, jnp.zeros((S, PAIR - 4 * MLA_HALF), F32)], axis=1)
    nt = S // TOK_TILE
    to_t = lambda a: a.reshape(nt, TOK_TILE, MLA_HALF).transpose(0, 2, 1)
    return {'csn': csn, 'cos_t': to_t(cos), 'sin_t': to_t(sin)}


def _layer_weights(w_in, w_uq, w_ukv, w_o, w_gate, w_up, w_down, g_q, g_kv, g_ffn):
    G = GROUP
    col = lambda i: w_in[:, i * G:(i + 1) * G]
    cq, ckv, kr = w_in[:, 9 * G:10 * G], w_in[:, 10 * G:10 * G + 128], w_in[:, 10 * G + 128:]
    kr_swap = jnp.concatenate([kr[:, MLA_HALF:], kr[:, :MLA_HALF]], axis=1)
    kr2 = jnp.concatenate([kr, kr_swap, jnp.zeros((w_in.shape[0], PAIR - 2 * MLA_ROPE), w_in.dtype)], axis=1)
    w_nat = jnp.concatenate([col(0), col(1), col(2), col(4), col(7), cq, ckv, kr2], axis=1)
    w_t = jnp.concatenate([col(3), col(5), col(6), col(8)], axis=1).T
    uq = w_uq.reshape(w_uq.shape[0], N_HEADS, MLA_QK)
    uq = jnp.pad(uq, ((0, 0), (0, 0), (0, PAIR - MLA_QK))).reshape(w_uq.shape[0], N_HEADS * PAIR)
    ukv = w_ukv.reshape(w_ukv.shape[0], N_HEADS, 2 * HEAD_DIM)
    uk = ukv.at[:, :, HEAD_DIM:].set(0.0).reshape(w_ukv.shape[0], N_HEADS * PAIR)
    uv = ukv[:, :, HEAD_DIM:].reshape(w_ukv.shape[0], GROUP)
    place = jnp.zeros((PAIR, N_HEADS * PAIR), F32)
    r = jnp.arange(MLA_ROPE)
    for hd in range(N_HEADS):
        place = place.at[r, hd * PAIR + HEAD_DIM + r].set(1.0)
    b = lambda a: a.astype(BF16)
    row = lambda a: a.astype(F32)[None, :]
    return {
        'w_nat': b(w_nat), 'w_t': b(w_t), 'w_uq_t': b(uq.T), 'w_uk': b(uk), 'w_uv_t': b(uv.T),
        'place': b(place), 'g_q': row(g_q), 'g_kv': row(g_kv), 'g_ffn': row(g_ffn),
        'w_o': b(w_o.reshape(4, GROUP, w_o.shape[1])), 'w_gate': b(w_gate), 'w_up': b(w_up),
        'w_down': b(w_down),
    }


def kernel(x, attn_norm, w_in, na_rpb, diff_lambda, diff_subln, mla_q_norm, w_uq, mla_kv_norm, w_ukv,
           t5_table, w_o, ffn_norm, w_gate, w_up, w_down, final_norm):
    B, S, D = x.shape
    depth = w_in.shape[0]
    rope = _rope_tables(S)
    diff_tiles, diff_far = _diff_bias(t5_table[:, :N_HEADS])
    dil_tiles = _dil_bias(t5_table[:, N_HEADS:])
    g_final = final_norm.astype(F32)[None, :]

    for l in range(depth):
        p = _layer_weights(w_in[l], w_uq[l], w_ukv[l], w_o[l], w_gate[l], w_up[l], w_down[l],
                           mla_q_norm[l], mla_kv_norm[l], ffn_norm[l])
        (qa, ka, va, kb, kc, kd, qb_t, vb_t, qc_t, vc_t, qd_t, vd_t) = _proj_call(
            x, attn_norm[l].astype(F32)[None, :], p, rope)

        o_a = _na_call(qa, ka, va, _na_bias(na_rpb[l]))

        lam_init = 0.8 - 0.6 * math.exp(-0.3 * l)
        lq1, lk1, lq2, lk2 = [diff_lambda[l, j].astype(F32) for j in range(4)]
        lam = jnp.exp(jnp.sum(lq1 * lk1)) - jnp.exp(jnp.sum(lq2 * lk2)) + lam_init
        far = diff_far.reshape(2, 2, 2).transpose(1, 0, 2).reshape(2, 4)
        consts = jnp.concatenate([jnp.broadcast_to(lam, (2, 1)), far, jnp.zeros((2, 3), F32)], axis=1)
        consts = jnp.broadcast_to(consts[:, :, None], (2, 8, 128))
        g_col = jnp.broadcast_to(diff_subln[l].astype(F32)[:, None], (HEAD_DIM, 128))
        o_b = _diff_call(consts, g_col, qb_t, kb, vb_t, diff_tiles, 1.0 - lam_init)

        o_c = _dil_call(qc_t, kc, vc_t, dil_tiles)
        o_d = _mla_call(qd_t, kd, vd_t)

        flat = lambda a: a.reshape(B * S, a.shape[-1])
        x = _post_call(flat(x), flat(o_a), flat(o_b), flat(o_c), flat(o_d), p, g_final,
                       final=(l == depth - 1)).reshape(B, S, D)
    return x
```

```python
import functools
import math

import jax
import jax.numpy as jnp
from jax import lax
from jax.experimental import pallas as pl
from jax.experimental.pallas import tpu as pltpu

F32 = jnp.float32
BF16 = jnp.bfloat16

N_HEADS = 4
HEAD_DIM = 64
GROUP = N_HEADS * HEAD_DIM
PAIR = 2 * HEAD_DIM
GRID_W = 64
NA_ROWS = 8
NA_COLS = 16
DIFF_QK = HEAD_DIM // 2
DIL_PATTERNS = ((128, 1), (512, 4), (2048, 16))
MLA_ROPE = HEAD_DIM // 2
MLA_HALF = MLA_ROPE // 2
MLA_QK = HEAD_DIM + MLA_ROPE
ROPE_THETA = 10000.0
T5_BUCKETS = 32
T5_MAX_DIST = 1024
EPS = 1e-6
LOG2E = 1.4426950408889634
NEG = -1e30
SUM_ROWS = 16
EXP_CHUNK = 64

V7X_VMEM_LIMIT = 56 * 1024 * 1024

TOK_TILE = 512
FULL_T = 512
BAND_T = 256
DIL_BAND = -(-max(w // 2 for w, _ in DIL_PATTERNS) // BAND_T)
DIL_Q_TILES = 2
MLA_Q_BLOCKS = 4
DIFF_Q_BLOCKS = 2
NA_GROUP = 8
FF_SPLIT = 2


def _rms(x, g):
    return x * lax.rsqrt(jnp.mean(x * x, axis=-1, keepdims=True) + EPS) * g


def _nt(a, b):
    return lax.dot_general(a, b, (((1,), (1,)), ((), ())), preferred_element_type=F32)


def _zero_of(x):
    u = lax.bitcast_convert_type(x, jnp.uint32)
    return lax.bitcast_convert_type((u >> 16) >> 16, F32)


def _flash_pipelined(streams, ranges, bufs):
    tk, tq = bufs[0][0].shape
    first = ranges[0][0]
    last = ranges[-1][1] - 1
    ones = jnp.ones((SUM_ROWS, tk), BF16)

    def scores(st, j, bias_fn, s_ref):
        s = jnp.dot(st['load_k'](j), st['qz'], preferred_element_type=F32)
        if bias_fn is not None:
            s = s + bias_fn(j)
        s_ref[...] = s
        return jnp.max(s, axis=0, keepdims=True)

    def accumulate(st, buf, j, par):
        acc_ref, stat_ref = buf[4], buf[5]
        v_aug = jnp.concatenate([st['load_vt'](j), ones], axis=0)
        acc_ref[...] = (stat_ref[_ALPHA + par:_ALPHA + par + 1, :] * acc_ref[...]
                        + jnp.dot(v_aug, buf[2 + par][...], preferred_element_type=F32))

    def step(par, r, t):
        for st, buf in zip(streams, bufs):
            stat_ref = buf[5]
            j = jnp.minimum(t, last)
            s_new = jnp.dot(st['load_k'](j), st['qz'], preferred_element_type=F32)
            if st['bias_fns'][r] is not None:
                s_new = s_new + st['bias_fns'][r](j)
            buf[par][...] = s_new
            cmax_t = jnp.max(s_new, axis=0, keepdims=True)
            m = stat_ref[_MAX:_MAX + 1, :]
            c = stat_ref[_CONST:_CONST + 1, :]
            m_new = jnp.maximum(m, stat_ref[_CMAX:_CMAX + 1, :] + c)
            ref_row = m_new - c
            for lo_row in range(0, tk, EXP_CHUNK):
                rows = slice(lo_row, lo_row + EXP_CHUNK)
                tie = _zero_of(s_new[lo_row:lo_row + 1, :])
                x = buf[1 - par][rows, :] - (ref_row + tie)
                buf[3 - par][rows, :] = jnp.exp2(x.astype(BF16))
            stat_ref[_ALPHA + 1 - par:_ALPHA + 2 - par, :] = jnp.exp2(m - m_new)
            stat_ref[_MAX:_MAX + 1, :] = m_new
            stat_ref[_CMAX:_CMAX + 1, :] = cmax_t
            stat_ref[_CONST:_CONST + 1, :] = jnp.zeros((1, tq), F32) + st['shifts'][r]
            accumulate(st, buf, jnp.maximum(t - 2, first), par)

    for st, buf in zip(streams, bufs):
        acc_ref, stat_ref = buf[4], buf[5]
        cmax = scores(st, first, st.get('first_bias', st['bias_fns'][0]), buf[0])
        buf[3][...] = jnp.zeros_like(buf[3])
        acc_ref[...] = jnp.zeros_like(acc_ref)
        stat_ref[_ALPHA:_ALPHA + 2, :] = jnp.ones((2, tq), F32)
        stat_ref[_MAX:_MAX + 1, :] = jnp.full((1, tq), NEG, F32)
        stat_ref[_CMAX:_CMAX + 1, :] = cmax
        stat_ref[_CONST:_CONST + 1, :] = (jnp.zeros((1, tq), F32)
                                          + (0.0 if 'first_bias' in st else st['shifts'][0]))

    def by_parity(j, fn):
        par = (j - first) % 2
        if isinstance(par, int):
            fn(par)
        else:
            pl.when(par == 0)(functools.partial(fn, 0))
            pl.when(par == 1)(functools.partial(fn, 1))

    for r, (lo, hi) in enumerate(ranges):

        def body(t, c, r=r):
            by_parity(t, lambda par: step(par, r, t))
            return c

        lax.fori_loop(jnp.maximum(lo, first + 1), hi + (r == len(ranges) - 1), body, 0)

    def finish(par):
        for st, buf in zip(streams, bufs):
            accumulate(st, buf, last, par)

    by_parity(last, finish)
    outs = []
    for buf in bufs:
        acc = buf[4][...]
        outs.append(acc[:HEAD_DIM] / acc[HEAD_DIM:HEAD_DIM + 1])
    return outs


_ALPHA, _MAX, _CMAX, _CONST = 0, 2, 3, 4
_BUFS_PER_STREAM = 6


def _flash_scratch(n_streams, tk, tq):
    per_stream = [pltpu.VMEM((tk, tq), F32), pltpu.VMEM((tk, tq), F32),
                  pltpu.VMEM((tk, tq), BF16), pltpu.VMEM((tk, tq), BF16),
                  pltpu.VMEM((HEAD_DIM + SUM_ROWS, tq), F32), pltpu.VMEM((8, tq), F32)]
    return per_stream * n_streams


def _flash_bufs(refs):
    n = _BUFS_PER_STREAM
    return [tuple(refs[n * k:n * k + n]) for k in range(len(refs) // n)]


def _store_t_blocks(ref, val, blk):
    for c in range(val.shape[1] // blk):
        ref[c] = val[:, c * blk:(c + 1) * blk].astype(ref.dtype)


def _proj_kernel(x_ref, g_ref, wnat_ref, wt_ref, gq_ref, wuqt_ref, gkv_ref, wuk_ref, wuvt_ref,
                 place_ref, csn_ref, cost_ref, sint_ref,
                 qa_ref, ka_ref, va_ref, kb_ref, kc_ref, kd_ref,
                 qbt_ref, vbt_ref, qct_ref, vct_ref, qdt_ref, vdt_ref):
    h = _rms(x_ref[...], g_ref[...]).astype(BF16)
    nat = jnp.dot(h, wnat_ref[...], preferred_element_type=F32)
    qa_ref[...] = (nat[:, 0:256] * (HEAD_DIM ** -0.5 * LOG2E)).astype(BF16)
    ka_ref[...] = nat[:, 256:512].astype(BF16)
    va_ref[...] = nat[:, 512:768].astype(BF16)
    kb_ref[...] = nat[:, 768:1024].astype(BF16)
    kc_ref[...] = nat[:, 1024:1280].astype(BF16)
    cq = nat[:, 1280:1536]
    ckv = nat[:, 1536:1664]
    kr2 = nat[:, 1664:1792]

    tr = _nt(wt_ref[...], h)
    _store_t_blocks(qbt_ref, tr[0:256] * (DIFF_QK ** -0.5 * LOG2E), FULL_T)
    _store_t_blocks(vbt_ref, tr[256:512], FULL_T)
    _store_t_blocks(qct_ref, tr[512:768] * (HEAD_DIM ** -0.5 * LOG2E), BAND_T)
    _store_t_blocks(vct_ref, tr[768:1024], BAND_T)

    cqn = _rms(cq, gq_ref[...]).astype(BF16)
    qd_t = _nt(wuqt_ref[...], cqn)
    cos_t = cost_ref[...]
    sin_t = sint_ref[...]
    pieces = []
    for hd in range(N_HEADS):
        b0 = hd * PAIR
        x1 = qd_t[b0 + 64:b0 + 80]
        x2 = qd_t[b0 + 80:b0 + 96]
        pieces += [qd_t[b0:b0 + 64], x1 * cos_t - x2 * sin_t, x1 * sin_t + x2 * cos_t,
                   qd_t[b0 + 96:b0 + 128]]
    qd_t = jnp.concatenate(pieces, axis=0) * (MLA_QK ** -0.5 * LOG2E)
    _store_t_blocks(qdt_ref, qd_t, FULL_T)

    ckvn = _rms(ckv, gkv_ref[...]).astype(BF16)
    k_nope = jnp.dot(ckvn, wuk_ref[...], preferred_element_type=F32)
    t = kr2 * csn_ref[...]
    roped = t + pltpu.roll(t, 96, 1)
    kd = k_nope + jnp.dot(roped.astype(BF16), place_ref[...], preferred_element_type=F32)
    kd_ref[...] = kd.astype(BF16)
    _store_t_blocks(vdt_ref, _nt(wuvt_ref[...], ckvn), FULL_T)


def _proj_call(x, g, p, rope):
    B, S, D = x.shape
    ts = TOK_TILE
    nt = S // ts
    nf, nb = ts // FULL_T, ts // BAND_T

    def full(a):
        return pl.BlockSpec(a.shape, lambda b, t: (0,) * a.ndim)

    nat_spec = lambda w: pl.BlockSpec((None, ts, w), lambda b, t: (b, t, 0))
    tb_spec = lambda n, r, blk: pl.BlockSpec((None, n, r, blk), lambda b, t: (b, t, 0, 0))
    nat_shape = lambda w: jax.ShapeDtypeStruct((B, S, w), BF16)
    tb_shape = lambda r, blk: jax.ShapeDtypeStruct((B, S // blk, r, blk), BF16)

    weights = [g, p['w_nat'], p['w_t'], p['g_q'], p['w_uq_t'], p['g_kv'], p['w_uk'], p['w_uv_t'], p['place']]
    in_specs = ([nat_spec(D)] + [full(a) for a in weights] +
                [pl.BlockSpec((ts, PAIR), lambda b, t: (t, 0)),
                 pl.BlockSpec((None, MLA_HALF, ts), lambda b, t: (t, 0, 0)),
                 pl.BlockSpec((None, MLA_HALF, ts), lambda b, t: (t, 0, 0))])
    out_shape = ([nat_shape(GROUP)] * 5 + [nat_shape(2 * GROUP)] +
                 [tb_shape(GROUP, FULL_T)] * 2 + [tb_shape(GROUP, BAND_T)] * 2 +
                 [tb_shape(2 * GROUP, FULL_T), tb_shape(GROUP, FULL_T)])
    out_specs = ([nat_spec(GROUP)] * 5 + [nat_spec(2 * GROUP)] +
                 [tb_spec(nf, GROUP, FULL_T)] * 2 + [tb_spec(nb, GROUP, BAND_T)] * 2 +
                 [tb_spec(nf, 2 * GROUP, FULL_T), tb_spec(nf, GROUP, FULL_T)])
    return pl.pallas_call(
        _proj_kernel,
        grid=(B, nt),
        in_specs=in_specs,
        out_specs=out_specs,
        out_shape=out_shape,
        compiler_params=pltpu.CompilerParams(
            dimension_semantics=("parallel", "parallel"), vmem_limit_bytes=V7X_VMEM_LIMIT),
        name="proj",
    )(x, *weights, rope['csn'], rope['cos_t'], rope['sin_t'])


def _na_kernel(q_ref, kp_ref, kc_ref, kn_ref, vp_ref, vc_ref, vn_ref, bias_ref, o_ref, kk_ref, vv_ref,
               *, n_rows):
    g = pl.program_id(1)
    blk = NA_GROUP * GRID_W
    win = NA_ROWS * GRID_W
    for c, (kr, vr) in enumerate(((kp_ref, vp_ref), (kc_ref, vc_ref), (kn_ref, vn_ref))):
        kk_ref[c * blk:(c + 1) * blk, :] = kr[...]
        vv_ref[c * blk:(c + 1) * blk, :] = vr[...]
    lane = lax.broadcasted_iota(jnp.int32, (GRID_W, PAIR), 1)

    def row_body(j, carry):
        r = g * NA_GROUP + j
        r0 = jnp.clip(r - NA_ROWS // 2, 0, n_rows - NA_ROWS)
        start = pl.multiple_of((r0 - (g - 1) * NA_GROUP) * GRID_W, GRID_W)
        didx = r - r0
        qs = pl.multiple_of(j * GRID_W, GRID_W)
        q_row = q_ref[pl.ds(qs, GRID_W), :]
        k_win = kk_ref[pl.ds(start, win), :]
        v_win = vv_ref[pl.ds(start, win), :]
        outs = []
        for pr in range(2):
            q2 = q_row[:, pr * PAIR:(pr + 1) * PAIR]
            k2 = k_win[:, pr * PAIR:(pr + 1) * PAIR]
            v2 = v_win[:, pr * PAIR:(pr + 1) * PAIR]
            o_pair = None
            for hh in range(2):
                mine = (lane // HEAD_DIM) == hh
                qz = jnp.where(mine, q2, jnp.zeros_like(q2))
                s = _nt(qz, k2) + bias_ref[didx, 2 * pr + hh]
                m = jnp.max(s, axis=-1, keepdims=True)
                e = jnp.exp2(s - m)
                den = jnp.sum(e, axis=-1, keepdims=True)
                o2 = jnp.dot(e.astype(BF16), v2, preferred_element_type=F32) / den
                o_pair = o2 if o_pair is None else jnp.where(mine, o2, o_pair)
            outs.append(o_pair)
        o_ref[pl.ds(qs, GRID_W), :] = jnp.concatenate(outs, axis=1).astype(o_ref.dtype)
        return carry

    lax.fori_loop(0, NA_GROUP, row_body, 0)


def _na_call(q, k, v, bias):
    B, S, _ = q.shape
    n_rows = S // GRID_W
    blk = NA_GROUP * GRID_W
    ng = S // blk
    cur = pl.BlockSpec((None, blk, GROUP), lambda b, g: (b, g, 0))
    prv = pl.BlockSpec((None, blk, GROUP), lambda b, g: (b, jnp.maximum(g - 1, 0), 0))
    nxt = pl.BlockSpec((None, blk, GROUP), lambda b, g: (b, jnp.minimum(g + 1, ng - 1), 0))
    return pl.pallas_call(
        functools.partial(_na_kernel, n_rows=n_rows),
        grid=(B, ng),
        in_specs=[cur, prv, cur, nxt, prv, cur, nxt,
                  pl.BlockSpec(bias.shape, lambda b, g: (0, 0, 0, 0))],
        out_specs=cur,
        out_shape=jax.ShapeDtypeStruct((B, S, GROUP), BF16),
        scratch_shapes=[pltpu.VMEM((3 * blk, GROUP), BF16), pltpu.VMEM((3 * blk, GROUP), BF16)],
        compiler_params=pltpu.CompilerParams(
            dimension_semantics=("parallel", "parallel"), vmem_limit_bytes=V7X_VMEM_LIMIT),
        name="na_attn",
    )(q, k, k, k, v, v, v, bias)


def _diff_kernel(c_ref, g_ref, qt_ref, k_ref, vt_ref, bias_ref, o_ref, *scratch, nk, lam_scale):
    i0 = pl.program_id(2) * DIFF_Q_BLOCKS
    tk = vt_ref.shape[2]
    tq = qt_ref.shape[2]
    row = lax.broadcasted_iota(jnp.int32, (PAIR, tq), 0)
    lam = c_ref[0:1, 0:1]
    j_lo = jnp.maximum(i0 - 2, 0)
    j_hi = jnp.minimum(i0 + DIFF_Q_BLOCKS + 2, nk)
    ranges = ((0, j_lo), (j_lo, j_hi), (j_hi, nk))

    def load_k(j):
        return k_ref[pl.ds(pl.multiple_of(j * tk, tk), tk), :]

    streams = []
    for qi in range(DIFF_Q_BLOCKS):
        q_t = qt_ref[qi]
        for hh in range(2):

            def load_vt(j, hh=hh):
                return vt_ref[j, hh * HEAD_DIM:(hh + 1) * HEAD_DIM, :]

            def near_bias(j, hh=hh, qi=qi):
                e0 = 2 * (j - i0 - qi) + 4

                def tile(d):
                    return bias_ref[jnp.clip(e0 + d, 0, 8), hh]
                t0 = tile(0)
                return jnp.concatenate(
                    [jnp.concatenate([t0, tile(-1)], axis=1),
                     jnp.concatenate([tile(1), t0], axis=1)], axis=0)

            for c in range(2):
                r_lo = hh * HEAD_DIM + c * DIFF_QK
                qz = jnp.where((row >= r_lo) & (row < r_lo + DIFF_QK), q_t, jnp.zeros_like(q_t))
                streams.append(dict(qz=qz, load_k=load_k, load_vt=load_vt,
                                    shifts=(c_ref[1 + hh:2 + hh, 0:1], 0.0, c_ref[3 + hh:4 + hh, 0:1]),
                                    bias_fns=(None, near_bias, None), first_bias=near_bias))
    maps = _flash_pipelined(streams, ranges, _flash_bufs(scratch))
    for qi in range(DIFF_Q_BLOCKS):
        heads = []
        for hh in range(2):
            k0 = 4 * qi + 2 * hh
            o = maps[k0] - lam * maps[k0 + 1]
            y = o * lax.rsqrt(jnp.mean(o * o, axis=0, keepdims=True) + EPS) * g_ref[:, 0:1]
            heads.append(y * lam_scale)
        o_ref[qi * tq:(qi + 1) * tq, :] = jnp.concatenate(heads, axis=0).T.astype(o_ref.dtype)


def _resident(block_shape, index_map):
    return pl.BlockSpec(block_shape, index_map, pipeline_mode=pl.Buffered(1))


def _diff_call(consts, g_col, q_t, k, v_t, bias, lam_scale):
    B, nq, _, tq = q_t.shape
    S = k.shape[1]
    nk, _, tk = v_t.shape[1:]
    nqb = DIFF_Q_BLOCKS
    return pl.pallas_call(
        functools.partial(_diff_kernel, nk=nk, lam_scale=lam_scale),
        grid=(B, 2, nq // nqb),
        in_specs=[
            pl.BlockSpec((None, 8, 128), lambda b, p, i: (p, 0, 0)),
            pl.BlockSpec(g_col.shape, lambda b, p, i: (0, 0)),
            pl.BlockSpec((None, nqb, PAIR, tq), lambda b, p, i: (b, i, p, 0)),
            _resident((None, S, PAIR), lambda b, p, i: (b, 0, p)),
            _resident((None, nk, PAIR, tk), lambda b, p, i: (b, 0, p, 0)),
            _resident((9, 2, BAND_T, BAND_T), lambda b, p, i: (0, p, 0, 0)),
        ],
        out_specs=pl.BlockSpec((None, nqb * tq, PAIR), lambda b, p, i: (b, i, p)),
        out_shape=jax.ShapeDtypeStruct((B, S, GROUP), BF16),
        scratch_shapes=_flash_scratch(4 * nqb, tk, tq),
        compiler_params=pltpu.CompilerParams(
            dimension_semantics=("parallel", "parallel", "parallel"), vmem_limit_bytes=V7X_VMEM_LIMIT),
        name="diff_attn",
    )(consts, g_col, q_t, k, v_t, bias)


def _dil_kernel(qt_ref, k_ref, vt_ref, bias_ref, o_ref, *scratch, nk):
    i = pl.program_id(1)
    t = BAND_T
    q_t = jnp.concatenate([qt_ref[a] for a in range(DIL_Q_TILES)], axis=1)
    row = lax.broadcasted_iota(jnp.int32, (PAIR, q_t.shape[1]), 0)
    j0 = i * DIL_Q_TILES
    j_lo = jnp.maximum(j0 - DIL_BAND, 0)
    j_hi = jnp.minimum(j0 + DIL_Q_TILES + DIL_BAND, nk)
    streams = []
    for head in range(N_HEADS):
        pr, hh = divmod(head, 2)

        def load_k(j, pr=pr):
            return k_ref[pl.ds(pl.multiple_of(j * t, t), t), pr * PAIR:(pr + 1) * PAIR]

        def load_vt(j, head=head):
            return vt_ref[j, head * HEAD_DIM:(head + 1) * HEAD_DIM, :]

        def bias(j, head=head):
            e0 = j - j0 + DIL_BAND + DIL_Q_TILES - 1
            return jnp.concatenate([bias_ref[e0 - a, head] for a in range(DIL_Q_TILES)], axis=1)

        q_pair = q_t[pr * PAIR:(pr + 1) * PAIR]
        qz = jnp.where((row // HEAD_DIM) == hh, q_pair, jnp.zeros_like(q_pair))
        streams.append(dict(qz=qz, load_k=load_k, load_vt=load_vt, shifts=(0.0,), bias_fns=(bias,)))
    heads = _flash_pipelined(streams, ((j_lo, j_hi),), _flash_bufs(scratch))
    o_ref[...] = jnp.concatenate(heads, axis=0).T.astype(o_ref.dtype)


def _dil_call(q_t, k, v_t, bias):
    B, nq, _, t = q_t.shape
    S = k.shape[1]
    nk = v_t.shape[1]
    nqt = DIL_Q_TILES
    return pl.pallas_call(
        functools.partial(_dil_kernel, nk=nk),
        grid=(B, nq // nqt),
        in_specs=[
            pl.BlockSpec((None, nqt, GROUP, t), lambda b, i: (b, i, 0, 0)),
            _resident((None, S, GROUP), lambda b, i: (b, 0, 0)),
            _resident((None, nk, GROUP, t), lambda b, i: (b, 0, 0, 0)),
            _resident(bias.shape, lambda b, i: (0, 0, 0, 0)),
        ],
        out_specs=pl.BlockSpec((None, nqt * t, GROUP), lambda b, i: (b, i, 0)),
        out_shape=jax.ShapeDtypeStruct((B, S, GROUP), BF16),
        scratch_shapes=_flash_scratch(N_HEADS, t, nqt * t),
        compiler_params=pltpu.CompilerParams(
            dimension_semantics=("parallel", "parallel"), vmem_limit_bytes=V7X_VMEM_LIMIT),
        name="dil_attn",
    )(q_t, k, v_t, bias)


def _mla_kernel(qt_ref, k_ref, vt_ref, o_ref, *scratch, nk):
    tk = vt_ref.shape[2]
    tq = qt_ref.shape[2]
    streams = []
    for qi in range(MLA_Q_BLOCKS):
        for hh in range(2):

            def load_k(j, hh=hh):
                return k_ref[pl.ds(pl.multiple_of(j * tk, tk), tk), hh * PAIR:(hh + 1) * PAIR]

            def load_vt(j, hh=hh):
                return vt_ref[j, hh * HEAD_DIM:(hh + 1) * HEAD_DIM, :]

            qz = qt_ref[qi, hh * PAIR:(hh + 1) * PAIR, :]
            streams.append(dict(qz=qz, load_k=load_k, load_vt=load_vt, shifts=(0.0,), bias_fns=(None,)))
    outs = _flash_pipelined(streams, ((0, nk),), _flash_bufs(scratch))
    for qi in range(MLA_Q_BLOCKS):
        o_t = jnp.concatenate(outs[2 * qi:2 * qi + 2], axis=0)
        o_ref[qi * tq:(qi + 1) * tq, :] = o_t.T.astype(o_ref.dtype)


def _mla_call(q_t, k, v_t):
    B, nq, _, tq = q_t.shape
    S = k.shape[1]
    nk, _, tk = v_t.shape[1:]
    nqb = MLA_Q_BLOCKS
    return pl.pallas_call(
        functools.partial(_mla_kernel, nk=nk),
        grid=(B, 2, nq // nqb),
        in_specs=[
            pl.BlockSpec((None, nqb, 2 * PAIR, tq), lambda b, p, i: (b, i, p, 0)),
            _resident((None, S, 2 * PAIR), lambda b, p, i: (b, 0, p)),
            _resident((None, nk, PAIR, tk), lambda b, p, i: (b, 0, p, 0)),
        ],
        out_specs=pl.BlockSpec((None, nqb * tq, PAIR), lambda b, p, i: (b, i, p)),
        out_shape=jax.ShapeDtypeStruct((B, S, GROUP), BF16),
        scratch_shapes=_flash_scratch(2 * nqb, tk, tq),
        compiler_params=pltpu.CompilerParams(
            dimension_semantics=("parallel", "parallel", "parallel"), vmem_limit_bytes=V7X_VMEM_LIMIT),
        name="mla_attn",
    )(q_t, k, v_t)


def _post_kernel(x_ref, oa_ref, ob_ref, oc_ref, od_ref, wo_ref, g_ref, wg_ref, wu_ref, wd_ref, gf_ref,
                 out_ref, acc_ref, h_ref, *, final):
    f = pl.program_id(1)

    @pl.when(f == 0)
    def _():
        x1 = x_ref[...]
        for gi, o_ref in enumerate((oa_ref, ob_ref, oc_ref, od_ref)):
            x1 = x1 + jnp.dot(o_ref[...], wo_ref[gi], preferred_element_type=F32)
        acc_ref[...] = x1
        h_ref[...] = _rms(x1, g_ref[...]).astype(BF16)

    h = h_ref[...]
    gate = jnp.dot(h, wg_ref[...], preferred_element_type=F32)
    up = jnp.dot(h, wu_ref[...], preferred_element_type=F32)
    act = (gate / (1.0 + jnp.exp(-gate))) * up
    acc_ref[...] += jnp.dot(act.astype(BF16), wd_ref[...], preferred_element_type=F32)

    @pl.when(f == pl.num_programs(1) - 1)
    def _():
        y = acc_ref[...]
        out_ref[...] = _rms(y, gf_ref[...]) if final else y


def _post_call(x, o_a, o_b, o_c, o_d, p, g_final, final):
    N, D = x.shape
    ts = TOK_TILE
    d_ff = p['w_gate'].shape[1]
    fc = d_ff // FF_SPLIT
    tok = lambda w: pl.BlockSpec((ts, w), lambda t, f: (t, 0))
    return pl.pallas_call(
        functools.partial(_post_kernel, final=final),
        grid=(N // ts, FF_SPLIT),
        in_specs=[tok(D), tok(GROUP), tok(GROUP), tok(GROUP), tok(GROUP),
                  pl.BlockSpec((4, GROUP, D), lambda t, f: (0, 0, 0)),
                  pl.BlockSpec((1, D), lambda t, f: (0, 0)),
                  pl.BlockSpec((D, fc), lambda t, f: (0, f)),
                  pl.BlockSpec((D, fc), lambda t, f: (0, f)),
                  pl.BlockSpec((fc, D), lambda t, f: (f, 0)),
                  pl.BlockSpec((1, D), lambda t, f: (0, 0))],
        out_specs=tok(D),
        out_shape=jax.ShapeDtypeStruct((N, D), F32),
        scratch_shapes=[pltpu.VMEM((ts, D), F32), pltpu.VMEM((ts, D), BF16)],
        compiler_params=pltpu.CompilerParams(
            dimension_semantics=("parallel", "arbitrary"), vmem_limit_bytes=V7X_VMEM_LIMIT),
        name="post_ffn",
    )(x, o_a, o_b, o_c, o_d, p['w_o'], p['g_ffn'], p['w_gate'], p['w_up'], p['w_down'], g_final)


def _t5_bucket(rel):
    nb = T5_BUCKETS // 2
    max_exact = nb // 2
    side = jnp.where(rel > 0, nb, 0)
    n = jnp.abs(rel)
    large = max_exact + (jnp.log(jnp.maximum(n, 1).astype(F32) / max_exact)
                         / math.log(T5_MAX_DIST / max_exact) * (nb - max_exact)).astype(jnp.int32)
    large = jnp.minimum(large, nb - 1)
    return side + jnp.where(n < max_exact, n, large)


def _lookup(table, idx):
    onehot = idx[..., None, None] == jnp.arange(table.shape[0])[:, None]
    return jnp.sum(jnp.where(onehot, table.astype(F32), 0.0), axis=-2)


def _toeplitz_tiles(vec_fn, t, n_side):
    e = jnp.arange(-n_side, n_side + 1)[:, None, None]
    rel = e * t + jnp.arange(t)[None, :, None] - jnp.arange(t)[None, None, :]
    return jnp.moveaxis(vec_fn(rel), -1, 1)


def _diff_bias(t5_diff):
    def vec(rel):
        return _lookup(t5_diff, _t5_bucket(rel)) * LOG2E
    tiles = _toeplitz_tiles(vec, BAND_T, 4)
    far = vec(jnp.array([-2 * T5_MAX_DIST, 2 * T5_MAX_DIST]))
    return tiles, far


def _dil_bias(t5_dil):
    def vec(rel):
        n = jnp.abs(rel)
        mult = jnp.zeros(rel.shape, F32)
        for window, dil in DIL_PATTERNS:
            mult = mult + ((rel % dil == 0) & (n <= window // 2)).astype(F32)
        b = _lookup(t5_dil, _t5_bucket(rel)) + jnp.log(jnp.maximum(mult, 1.0))[..., None]
        return jnp.where((mult > 0)[..., None], b * LOG2E, NEG)
    return _toeplitz_tiles(vec, BAND_T, DIL_BAND + DIL_Q_TILES - 1)


def _na_bias(rpb):
    d = jnp.arange(NA_ROWS)[:, None]
    ridx = jnp.arange(NA_ROWS)[None, :] - d + NA_ROWS - 1
    qc = jnp.arange(GRID_W)[:, None]
    kc = jnp.arange(GRID_W)[None, :]
    c0 = jnp.clip(qc - NA_COLS // 2, 0, GRID_W - NA_COLS)
    valid = (kc >= c0) & (kc < c0 + NA_COLS)
    cidx = jnp.clip(kc - qc + NA_COLS - 1, 0, 2 * NA_COLS - 2)
    rows = rpb.astype(F32)[:, ridx]
    vals = _lookup(rows.reshape(-1, rows.shape[-1]).T, cidx)
    vals = jnp.where(valid[:, :, None], vals * LOG2E, NEG)
    vals = vals.reshape(GRID_W, GRID_W, N_HEADS, NA_ROWS, NA_ROWS).transpose(3, 2, 0, 4, 1)
    return vals.reshape(NA_ROWS, N_HEADS, GRID_W, NA_ROWS * GRID_W)


def _rope_tables(S):
    inv_freq = ROPE_THETA ** (-jnp.arange(MLA_HALF, dtype=F32) / MLA_HALF)
    ang = jnp.arange(S).astype(F32)[:, None] * inv_freq[None, :]
    cos, sin = jnp.cos(ang), jnp.sin(ang)
    csn = jnp.concatenate([cos, cos, -sin, sin, jnp.zeros((S, PAIR - 4 * MLA_HALF), F32)], axis=1)
    nt = S // TOK_TILE
    to_t = lambda a: a.reshape(nt, TOK_TILE, MLA_HALF).transpose(0, 2, 1)
    return {'csn': csn, 'cos_t': to_t(cos), 'sin_t': to_t(sin)}


def _layer_weights(w_in, w_uq, w_ukv, w_o, w_gate, w_up, w_down, g_q, g_kv, g_ffn):
    G = GROUP
    col = lambda i: w_in[:, i * G:(i + 1) * G]
    cq, ckv, kr = w_in[:, 9 * G:10 * G], w_in[:, 10 * G:10 * G + 128], w_in[:, 10 * G + 128:]
    kr_swap = jnp.concatenate([kr[:, MLA_HALF:], kr[:, :MLA_HALF]], axis=1)
    kr2 = jnp.concatenate([kr, kr_swap, jnp.zeros((w_in.shape[0], PAIR - 2 * MLA_ROPE), w_in.dtype)], axis=1)
    w_nat = jnp.concatenate([col(0), col(1), col(2), col(4), col(7), cq, ckv, kr2], axis=1)
    w_t = jnp.concatenate([col(3), col(5), col(6), col(8)], axis=1).T
    uq = w_uq.reshape(w_uq.shape[0], N_HEADS, MLA_QK)
    uq = jnp.pad(uq, ((0, 0), (0, 0), (0, PAIR - MLA_QK))).reshape(w_uq.shape[0], N_HEADS * PAIR)
    ukv = w_ukv.reshape(w_ukv.shape[0], N_HEADS, 2 * HEAD_DIM)
    uk = ukv.at[:, :, HEAD_DIM:].set(0.0).reshape(w_ukv.shape[0], N_HEADS * PAIR)
    uv = ukv[:, :, HEAD_DIM:].reshape(w_ukv.shape[0], GROUP)
    place = jnp.zeros((PAIR, N_HEADS * PAIR), F32)
    r = jnp.arange(MLA_ROPE)
    for hd in range(N_HEADS):
        place = place.at[r, hd * PAIR + HEAD_DIM + r].set(1.0)
    b = lambda a: a.astype(BF16)
    row = lambda a: a.astype(F32)[None, :]
    return {
        'w_nat': b(w_nat), 'w_t': b(w_t), 'w_uq_t': b(uq.T), 'w_uk': b(uk), 'w_uv_t': b(uv.T),
        'place': b(place), 'g_q': row(g_q), 'g_kv': row(g_kv), 'g_ffn': row(g_ffn),
        'w_o': b(w_o.reshape(4, GROUP, w_o.shape[1])), 'w_gate': b(w_gate), 'w_up': b(w_up),
        'w_down': b(w_down),
    }


def kernel(x, attn_norm, w_in, na_rpb, diff_lambda, diff_subln, mla_q_norm, w_uq, mla_kv_norm, w_ukv,
           t5_table, w_o, ffn_norm, w_gate, w_up, w_down, final_norm):
    B, S, D = x.shape
    depth = w_in.shape[0]
    rope = _rope_tables(S)
    diff_tiles, diff_far = _diff_bias(t5_table[:, :N_HEADS])
    dil_tiles = _dil_bias(t5_table[:, N_HEADS:])
    g_final = final_norm.astype(F32)[None, :]

    for l in range(depth):
        p = _layer_weights(w_in[l], w_uq[l], w_ukv[l], w_o[l], w_gate[l], w_up[l], w_down[l],
                           mla_q_norm[l], mla_kv_norm[l], ffn_norm[l])
        (qa, ka, va, kb, kc, kd, qb_t, vb_t, qc_t, vc_t, qd_t, vd_t) = _proj_call(
            x, attn_norm[l].astype(F32)[None, :], p, rope)

        o_a = _na_call(qa, ka, va, _na_bias(na_rpb[l]))

        lam_init = 0.8 - 0.6 * math.exp(-0.3 * l)
        lq1, lk1, lq2, lk2 = [diff_lambda[l, j].astype(F32) for j in range(4)]
        lam = jnp.exp(jnp.sum(lq1 * lk1)) - jnp.exp(jnp.sum(lq2 * lk2)) + lam_init
        far = diff_far.reshape(2, 2, 2).transpose(1, 0, 2).reshape(2, 4)
        consts = jnp.concatenate([jnp.broadcast_to(lam, (2, 1)), far, jnp.zeros((2, 3), F32)], axis=1)
        consts = jnp.broadcast_to(consts[:, :, None], (2, 8, 128))
        g_col = jnp.broadcast_to(diff_subln[l].astype(F32)[:, None], (HEAD_DIM, 128))
        o_b = _diff_call(consts, g_col, qb_t, kb, vb_t, diff_tiles, 1.0 - lam_init)

        o_c = _dil_call(qc_t, kc, vc_t, dil_tiles)
        o_d = _mla_call(qd_t, kd, vd_t)

        flat = lambda a: a.reshape(B * S, a.shape[-1])
        x = _post_call(flat(x), flat(o_a), flat(o_b), flat(o_c), flat(o_d), p, g_final,
                       final=(l == depth - 1)).reshape(B, S, D)
    return x
```

```python
import functools
import math

import jax
import jax.numpy as jnp
from jax import lax
from jax.experimental import pallas as pl
from jax.experimental.pallas import tpu as pltpu

F32 = jnp.float32
BF16 = jnp.bfloat16

N_HEADS = 4
HEAD_DIM = 64
GROUP = N_HEADS * HEAD_DIM
PAIR = 2 * HEAD_DIM
GRID_W = 64
NA_ROWS = 8
NA_COLS = 16
DIFF_QK = HEAD_DIM // 2
DIL_PATTERNS = ((128, 1), (512, 4), (2048, 16))
MLA_ROPE = HEAD_DIM // 2
MLA_HALF = MLA_ROPE // 2
MLA_QK = HEAD_DIM + MLA_ROPE
ROPE_THETA = 10000.0
T5_BUCKETS = 32
T5_MAX_DIST = 1024
EPS = 1e-6
LOG2E = 1.4426950408889634
NEG = -1e30
SUM_ROWS = 16
EXP_CHUNK = 32

V7X_VMEM_LIMIT = 56 * 1024 * 1024

TOK_TILE = 512
FULL_T = 512
BAND_T = 256
DIL_BAND = -(-max(w // 2 for w, _ in DIL_PATTERNS) // BAND_T)
DIL_Q_TILES = 2
MLA_Q_BLOCKS = 4
DIFF_Q_BLOCKS = 2
NA_GROUP = 8
NA_ROWS_PER_ITER = 2
FF_SPLIT = 2


def _rms(x, g):
    return x * lax.rsqrt(jnp.mean(x * x, axis=-1, keepdims=True) + EPS) * g


def _nt(a, b):
    return lax.dot_general(a, b, (((1,), (1,)), ((), ())), preferred_element_type=F32)


def _zero_of(x):
    u = lax.bitcast_convert_type(x, jnp.uint32)
    return lax.bitcast_convert_type((u >> 16) >> 16, F32)


def _flash_pipelined(streams, ranges, bufs):
    tk, tq = bufs[0][0].shape
    first = ranges[0][0]
    last = ranges[-1][1] - 1
    ones = jnp.ones((SUM_ROWS, tk), BF16)

    def scores(st, j, bias_fn, s_ref):
        s = jnp.dot(st['load_k'](j), st['qz'], preferred_element_type=F32)
        if bias_fn is not None:
            s = s + bias_fn(j)
        s_ref[...] = s
        return jnp.max(s, axis=0, keepdims=True)

    def accumulate(st, buf, j, par):
        acc_ref, stat_ref = buf[4], buf[5]
        v_aug = jnp.concatenate([st['load_vt'](j), ones], axis=0)
        acc_ref[...] = (stat_ref[_ALPHA + par:_ALPHA + par + 1, :] * acc_ref[...]
                        + jnp.dot(v_aug, buf[2 + par][...], preferred_element_type=F32))

    def step(par, r, t):
        for st, buf in zip(streams, bufs):
            stat_ref = buf[5]
            j = jnp.minimum(t, last)
            s_new = jnp.dot(st['load_k'](j), st['qz'], preferred_element_type=F32)
            if st['bias_fns'][r] is not None:
                s_new = s_new + st['bias_fns'][r](j)
            buf[par][...] = s_new
            cmax_t = jnp.max(s_new, axis=0, keepdims=True)
            m = stat_ref[_MAX:_MAX + 1, :]
            c = stat_ref[_CONST:_CONST + 1, :]
            m_new = jnp.maximum(m, stat_ref[_CMAX:_CMAX + 1, :] + c)
            ref_rows = jnp.broadcast_to(m_new - c, (8, tq))
            for lo_row in range(0, tk, EXP_CHUNK):
                rows = slice(lo_row, lo_row + EXP_CHUNK)
                tie = _zero_of(s_new[lo_row:lo_row + 8, :])
                x = buf[1 - par][rows, :].reshape(EXP_CHUNK // 8, 8, tq) - (ref_rows + tie)[None]
                buf[3 - par][rows, :] = jnp.exp2(x.reshape(EXP_CHUNK, tq).astype(BF16))
            stat_ref[_ALPHA + 1 - par:_ALPHA + 2 - par, :] = jnp.exp2(m - m_new)
            stat_ref[_MAX:_MAX + 1, :] = m_new
            stat_ref[_CMAX:_CMAX + 1, :] = cmax_t
            stat_ref[_CONST:_CONST + 1, :] = jnp.zeros((1, tq), F32) + st['shifts'][r]
            accumulate(st, buf, jnp.maximum(t - 2, first), par)

    for st, buf in zip(streams, bufs):
        acc_ref, stat_ref = buf[4], buf[5]
        cmax = scores(st, first, st.get('first_bias', st['bias_fns'][0]), buf[0])
        buf[3][...] = jnp.zeros_like(buf[3])
        acc_ref[...] = jnp.zeros_like(acc_ref)
        stat_ref[_ALPHA:_ALPHA + 2, :] = jnp.ones((2, tq), F32)
        stat_ref[_MAX:_MAX + 1, :] = jnp.full((1, tq), NEG, F32)
        stat_ref[_CMAX:_CMAX + 1, :] = cmax
        stat_ref[_CONST:_CONST + 1, :] = (jnp.zeros((1, tq), F32)
                                          + (0.0 if 'first_bias' in st else st['shifts'][0]))

    def by_parity(j, fn):
        par = (j - first) % 2
        if isinstance(par, int):
            fn(par)
        else:
            pl.when(par == 0)(functools.partial(fn, 0))
            pl.when(par == 1)(functools.partial(fn, 1))

    for r, (lo, hi) in enumerate(ranges):

        def body(t, c, r=r):
            by_parity(t, lambda par: step(par, r, t))
            return c

        lax.fori_loop(jnp.maximum(lo, first + 1), hi + (r == len(ranges) - 1), body, 0)

    def finish(par):
        for st, buf in zip(streams, bufs):
            accumulate(st, buf, last, par)

    by_parity(last, finish)
    outs = []
    for buf in bufs:
        acc = buf[4][...]
        outs.append(acc[:HEAD_DIM] / acc[HEAD_DIM:HEAD_DIM + 1])
    return outs


_ALPHA, _MAX, _CMAX, _CONST = 0, 2, 3, 4
_BUFS_PER_STREAM = 6


def _flash_scratch(n_streams, tk, tq):
    per_stream = [pltpu.VMEM((tk, tq), F32), pltpu.VMEM((tk, tq), F32),
                  pltpu.VMEM((tk, tq), BF16), pltpu.VMEM((tk, tq), BF16),
                  pltpu.VMEM((HEAD_DIM + SUM_ROWS, tq), F32), pltpu.VMEM((8, tq), F32)]
    return per_stream * n_streams


def _flash_bufs(refs):
    n = _BUFS_PER_STREAM
    return [tuple(refs[n * k:n * k + n]) for k in range(len(refs) // n)]


def _store_t_blocks(ref, val, blk):
    for c in range(val.shape[1] // blk):
        ref[c] = val[:, c * blk:(c + 1) * blk].astype(ref.dtype)


def _proj_kernel(x_ref, g_ref, wnat_ref, wt_ref, gq_ref, wuqt_ref, gkv_ref, wuk_ref, wuvt_ref,
                 place_ref, csn_ref, cost_ref, sint_ref,
                 qa_ref, ka_ref, va_ref, kb_ref, kc_ref, kd_ref,
                 qbt_ref, vbt_ref, qct_ref, vct_ref, qdt_ref, vdt_ref):
    h = _rms(x_ref[...], g_ref[...]).astype(BF16)
    nat = jnp.dot(h, wnat_ref[...], preferred_element_type=F32)
    qa_ref[...] = (nat[:, 0:256] * (HEAD_DIM ** -0.5 * LOG2E)).astype(BF16)
    ka_ref[...] = nat[:, 256:512].astype(BF16)
    va_ref[...] = nat[:, 512:768].astype(BF16)
    kb_ref[...] = nat[:, 768:1024].astype(BF16)
    kc_ref[...] = nat[:, 1024:1280].astype(BF16)
    cq = nat[:, 1280:1536]
    ckv = nat[:, 1536:1664]
    kr2 = nat[:, 1664:1792]

    tr = _nt(wt_ref[...], h)
    _store_t_blocks(qbt_ref, tr[0:256] * (DIFF_QK ** -0.5 * LOG2E), FULL_T)
    _store_t_blocks(vbt_ref, tr[256:512], FULL_T)
    _store_t_blocks(qct_ref, tr[512:768] * (HEAD_DIM ** -0.5 * LOG2E), BAND_T)
    _store_t_blocks(vct_ref, tr[768:1024], BAND_T)

    cqn = _rms(cq, gq_ref[...]).astype(BF16)
    qd_t = _nt(wuqt_ref[...], cqn)
    cos_t = cost_ref[...]
    sin_t = sint_ref[...]
    pieces = []
    for hd in range(N_HEADS):
        b0 = hd * PAIR
        x1 = qd_t[b0 + 64:b0 + 80]
        x2 = qd_t[b0 + 80:b0 + 96]
        pieces += [qd_t[b0:b0 + 64], x1 * cos_t - x2 * sin_t, x1 * sin_t + x2 * cos_t,
                   qd_t[b0 + 96:b0 + 128]]
    qd_t = jnp.concatenate(pieces, axis=0) * (MLA_QK ** -0.5 * LOG2E)
    _store_t_blocks(qdt_ref, qd_t, FULL_T)

    ckvn = _rms(ckv, gkv_ref[...]).astype(BF16)
    k_nope = jnp.dot(ckvn, wuk_ref[...], preferred_element_type=F32)
    t = kr2 * csn_ref[...]
    roped = t + pltpu.roll(t, 96, 1)
    kd = k_nope + jnp.dot(roped.astype(BF16), place_ref[...], preferred_element_type=F32)
    kd_ref[...] = kd.astype(BF16)
    _store_t_blocks(vdt_ref, _nt(wuvt_ref[...], ckvn), FULL_T)


def _proj_call(x, g, p, rope):
    B, S, D = x.shape
    ts = TOK_TILE
    nt = S // ts
    nf, nb = ts // FULL_T, ts // BAND_T

    def full(a):
        return pl.BlockSpec(a.shape, lambda b, t: (0,) * a.ndim)

    nat_spec = lambda w: pl.BlockSpec((None, ts, w), lambda b, t: (b, t, 0))
    tb_spec = lambda n, r, blk: pl.BlockSpec((None, n, r, blk), lambda b, t: (b, t, 0, 0))
    nat_shape = lambda w: jax.ShapeDtypeStruct((B, S, w), BF16)
    tb_shape = lambda r, blk: jax.ShapeDtypeStruct((B, S // blk, r, blk), BF16)

    weights = [g, p['w_nat'], p['w_t'], p['g_q'], p['w_uq_t'], p['g_kv'], p['w_uk'], p['w_uv_t'], p['place']]
    in_specs = ([nat_spec(D)] + [full(a) for a in weights] +
                [pl.BlockSpec((ts, PAIR), lambda b, t: (t, 0)),
                 pl.BlockSpec((None, MLA_HALF, ts), lambda b, t: (t, 0, 0)),
                 pl.BlockSpec((None, MLA_HALF, ts), lambda b, t: (t, 0, 0))])
    out_shape = ([nat_shape(GROUP)] * 5 + [nat_shape(2 * GROUP)] +
                 [tb_shape(GROUP, FULL_T)] * 2 + [tb_shape(GROUP, BAND_T)] * 2 +
                 [tb_shape(2 * GROUP, FULL_T), tb_shape(GROUP, FULL_T)])
    out_specs = ([nat_spec(GROUP)] * 5 + [nat_spec(2 * GROUP)] +
                 [tb_spec(nf, GROUP, FULL_T)] * 2 + [tb_spec(nb, GROUP, BAND_T)] * 2 +
                 [tb_spec(nf, 2 * GROUP, FULL_T), tb_spec(nf, GROUP, FULL_T)])
    return pl.pallas_call(
        _proj_kernel,
        grid=(B, nt),
        in_specs=in_specs,
        out_specs=out_specs,
        out_shape=out_shape,
        compiler_params=pltpu.CompilerParams(
            dimension_semantics=("parallel", "parallel"), vmem_limit_bytes=V7X_VMEM_LIMIT),
        name="proj",
    )(x, *weights, rope['csn'], rope['cos_t'], rope['sin_t'])


def _na_kernel(q_ref, kp_ref, kc_ref, kn_ref, vp_ref, vc_ref, vn_ref, bias_ref, o_ref, kk_ref, vv_ref,
               *, n_rows):
    g = pl.program_id(1)
    blk = NA_GROUP * GRID_W
    win = NA_ROWS * GRID_W
    for c, (kr, vr) in enumerate(((kp_ref, vp_ref), (kc_ref, vc_ref), (kn_ref, vn_ref))):
        kk_ref[c * blk:(c + 1) * blk, :] = kr[...]
        vv_ref[c * blk:(c + 1) * blk, :] = vr[...]
    lane = lax.broadcasted_iota(jnp.int32, (GRID_W, PAIR), 1)

    first_head = lane < HEAD_DIM

    def rows_body(jj, carry):
        v_pairs, scores, probs = [], [], []
        for dj in range(NA_ROWS_PER_ITER):
            j = jj * NA_ROWS_PER_ITER + dj
            r = g * NA_GROUP + j
            r0 = jnp.clip(r - NA_ROWS // 2, 0, n_rows - NA_ROWS)
            start = pl.multiple_of((r0 - (g - 1) * NA_GROUP) * GRID_W, GRID_W)
            q_row = q_ref[pl.ds(pl.multiple_of(j * GRID_W, GRID_W), GRID_W), :]
            k_win = kk_ref[pl.ds(start, win), :]
            v_win = vv_ref[pl.ds(start, win), :]
            for pr in range(2):
                q2 = q_row[:, pr * PAIR:(pr + 1) * PAIR]
                zero = jnp.zeros_like(q2)
                qz = jnp.concatenate([jnp.where(first_head, q2, zero), jnp.where(first_head, zero, q2)], axis=0)
                bias = jnp.concatenate([bias_ref[r - r0, 2 * pr], bias_ref[r - r0, 2 * pr + 1]], axis=0)
                scores.append(_nt(qz, k_win[:, pr * PAIR:(pr + 1) * PAIR]) + bias)
                v_pairs.append(v_win[:, pr * PAIR:(pr + 1) * PAIR])
        for s in scores:
            e = jnp.exp2(s - jnp.max(s, axis=-1, keepdims=True))
            probs.append((e.astype(BF16), jnp.sum(e, axis=-1, keepdims=True)))
        outs = []
        for (e, den), v2 in zip(probs, v_pairs):
            o2 = jnp.dot(e, v2, preferred_element_type=F32) / den
            outs.append(jnp.where(first_head, o2[:GRID_W], o2[GRID_W:]))
        for dj in range(NA_ROWS_PER_ITER):
            qs = pl.multiple_of((jj * NA_ROWS_PER_ITER + dj) * GRID_W, GRID_W)
            o_ref[pl.ds(qs, GRID_W), :] = jnp.concatenate(outs[2 * dj:2 * dj + 2], axis=1).astype(o_ref.dtype)
        return carry

    lax.fori_loop(0, NA_GROUP // NA_ROWS_PER_ITER, rows_body, 0)


def _na_call(q, k, v, bias):
    B, S, _ = q.shape
    n_rows = S // GRID_W
    blk = NA_GROUP * GRID_W
    ng = S // blk
    cur = pl.BlockSpec((None, blk, GROUP), lambda b, g: (b, g, 0))
    prv = pl.BlockSpec((None, blk, GROUP), lambda b, g: (b, jnp.maximum(g - 1, 0), 0))
    nxt = pl.BlockSpec((None, blk, GROUP), lambda b, g: (b, jnp.minimum(g + 1, ng - 1), 0))
    return pl.pallas_call(
        functools.partial(_na_kernel, n_rows=n_rows),
        grid=(B, ng),
        in_specs=[cur, prv, cur, nxt, prv, cur, nxt,
                  pl.BlockSpec(bias.shape, lambda b, g: (0, 0, 0, 0))],
        out_specs=cur,
        out_shape=jax.ShapeDtypeStruct((B, S, GROUP), BF16),
        scratch_shapes=[pltpu.VMEM((3 * blk, GROUP), BF16), pltpu.VMEM((3 * blk, GROUP), BF16)],
        compiler_params=pltpu.CompilerParams(
            dimension_semantics=("parallel", "parallel"), vmem_limit_bytes=V7X_VMEM_LIMIT),
        name="na_attn",
    )(q, k, k, k, v, v, v, bias)


def _diff_kernel(c_ref, g_ref, qt_ref, k_ref, vt_ref, bias_ref, o_ref, *scratch, nk, lam_scale):
    i0 = pl.program_id(2) * DIFF_Q_BLOCKS
    tk = vt_ref.shape[2]
    tq = qt_ref.shape[2]
    row = lax.broadcasted_iota(jnp.int32, (PAIR, tq), 0)
    lam = c_ref[0:1, 0:1]
    j_lo = jnp.maximum(i0 - 2, 0)
    j_hi = jnp.minimum(i0 + DIFF_Q_BLOCKS + 2, nk)
    ranges = ((0, j_lo), (j_lo, j_hi), (j_hi, nk))

    def load_k(j):
        return k_ref[pl.ds(pl.multiple_of(j * tk, tk), tk), :]

    streams = []
    for qi in range(DIFF_Q_BLOCKS):
        q_t = qt_ref[qi]
        for hh in range(2):

            def load_vt(j, hh=hh):
                return vt_ref[j, hh * HEAD_DIM:(hh + 1) * HEAD_DIM, :]

            def near_bias(j, hh=hh, qi=qi):
                e0 = 2 * (j - i0 - qi) + 4

                def tile(d):
                    return bias_ref[jnp.clip(e0 + d, 0, 8), hh]
                t0 = tile(0)
                return jnp.concatenate(
                    [jnp.concatenate([t0, tile(-1)], axis=1),
                     jnp.concatenate([tile(1), t0], axis=1)], axis=0)

            for c in range(2):
                r_lo = hh * HEAD_DIM + c * DIFF_QK
                qz = jnp.where((row >= r_lo) & (row < r_lo + DIFF_QK), q_t, jnp.zeros_like(q_t))
                streams.append(dict(qz=qz, load_k=load_k, load_vt=load_vt,
                                    shifts=(c_ref[1 + hh:2 + hh, 0:1], 0.0, c_ref[3 + hh:4 + hh, 0:1]),
                                    bias_fns=(None, near_bias, None), first_bias=near_bias))
    maps = _flash_pipelined(streams, ranges, _flash_bufs(scratch))
    for qi in range(DIFF_Q_BLOCKS):
        heads = []
        for hh in range(2):
            k0 = 4 * qi + 2 * hh
            o = maps[k0] - lam * maps[k0 + 1]
            y = o * lax.rsqrt(jnp.mean(o * o, axis=0, keepdims=True) + EPS) * g_ref[:, 0:1]
            heads.append(y * lam_scale)
        o_ref[qi * tq:(qi + 1) * tq, :] = jnp.concatenate(heads, axis=0).T.astype(o_ref.dtype)


def _resident(block_shape, index_map):
    return pl.BlockSpec(block_shape, index_map, pipeline_mode=pl.Buffered(1))


def _diff_call(consts, g_col, q_t, k, v_t, bias, lam_scale):
    B, nq, _, tq = q_t.shape
    S = k.shape[1]
    nk, _, tk = v_t.shape[1:]
    nqb = DIFF_Q_BLOCKS
    return pl.pallas_call(
        functools.partial(_diff_kernel, nk=nk, lam_scale=lam_scale),
        grid=(B, 2, nq // nqb),
        in_specs=[
            pl.BlockSpec((None, 8, 128), lambda b, p, i: (p, 0, 0)),
            pl.BlockSpec(g_col.shape, lambda b, p, i: (0, 0)),
            pl.BlockSpec((None, nqb, PAIR, tq), lambda b, p, i: (b, i, p, 0)),
            _resident((None, S, PAIR), lambda b, p, i: (b, 0, p)),
            _resident((None, nk, PAIR, tk), lambda b, p, i: (b, 0, p, 0)),
            _resident((9, 2, BAND_T, BAND_T), lambda b, p, i: (0, p, 0, 0)),
        ],
        out_specs=pl.BlockSpec((None, nqb * tq, PAIR), lambda b, p, i: (b, i, p)),
        out_shape=jax.ShapeDtypeStruct((B, S, GROUP), BF16),
        scratch_shapes=_flash_scratch(4 * nqb, tk, tq),
        compiler_params=pltpu.CompilerParams(
            dimension_semantics=("parallel", "parallel", "parallel"), vmem_limit_bytes=V7X_VMEM_LIMIT),
        name="diff_attn",
    )(consts, g_col, q_t, k, v_t, bias)


def _dil_kernel(qt_ref, k_ref, vt_ref, bias_ref, o_ref, *scratch, nk):
    i = pl.program_id(1)
    t = BAND_T
    q_t = jnp.concatenate([qt_ref[a] for a in range(DIL_Q_TILES)], axis=1)
    row = lax.broadcasted_iota(jnp.int32, (PAIR, q_t.shape[1]), 0)
    j0 = i * DIL_Q_TILES
    j_lo = jnp.maximum(j0 - DIL_BAND, 0)
    j_hi = jnp.minimum(j0 + DIL_Q_TILES + DIL_BAND, nk)
    streams = []
    for head in range(N_HEADS):
        pr, hh = divmod(head, 2)

        def load_k(j, pr=pr):
            return k_ref[pl.ds(pl.multiple_of(j * t, t), t), pr * PAIR:(pr + 1) * PAIR]

        def load_vt(j, head=head):
            return vt_ref[j, head * HEAD_DIM:(head + 1) * HEAD_DIM, :]

        def bias(j, head=head):
            e0 = j - j0 + DIL_BAND + DIL_Q_TILES - 1
            return jnp.concatenate([bias_ref[e0 - a, head] for a in range(DIL_Q_TILES)], axis=1)

        q_pair = q_t[pr * PAIR:(pr + 1) * PAIR]
        qz = jnp.where((row // HEAD_DIM) == hh, q_pair, jnp.zeros_like(q_pair))
        streams.append(dict(qz=qz, load_k=load_k, load_vt=load_vt, shifts=(0.0,), bias_fns=(bias,)))
    heads = _flash_pipelined(streams, ((j_lo, j_hi),), _flash_bufs(scratch))
    o_ref[...] = jnp.concatenate(heads, axis=0).T.astype(o_ref.dtype)


def _dil_call(q_t, k, v_t, bias):
    B, nq, _, t = q_t.shape
    S = k.shape[1]
    nk = v_t.shape[1]
    nqt = DIL_Q_TILES
    return pl.pallas_call(
        functools.partial(_dil_kernel, nk=nk),
        grid=(B, nq // nqt),
        in_specs=[
            pl.BlockSpec((None, nqt, GROUP, t), lambda b, i: (b, i, 0, 0)),
            _resident((None, S, GROUP), lambda b, i: (b, 0, 0)),
            _resident((None, nk, GROUP, t), lambda b, i: (b, 0, 0, 0)),
            _resident(bias.shape, lambda b, i: (0, 0, 0, 0)),
        ],
        out_specs=pl.BlockSpec((None, nqt * t, GROUP), lambda b, i: (b, i, 0)),
        out_shape=jax.ShapeDtypeStruct((B, S, GROUP), BF16),
        scratch_shapes=_flash_scratch(N_HEADS, t, nqt * t),
        compiler_params=pltpu.CompilerParams(
            dimension_semantics=("parallel", "parallel"), vmem_limit_bytes=V7X_VMEM_LIMIT),
        name="dil_attn",
    )(q_t, k, v_t, bias)


def _mla_kernel(qt_ref, k_ref, vt_ref, o_ref, *scratch, nk):
    tk = vt_ref.shape[2]
    tq = qt_ref.shape[2]
    streams = []
    for qi in range(MLA_Q_BLOCKS):
        for hh in range(2):

            def load_k(j, hh=hh):
                return k_ref[pl.ds(pl.multiple_of(j * tk, tk), tk), hh * PAIR:(hh + 1) * PAIR]

            def load_vt(j, hh=hh):
                return vt_ref[j, hh * HEAD_DIM:(hh + 1) * HEAD_DIM, :]

            qz = qt_ref[qi, hh * PAIR:(hh + 1) * PAIR, :]
            streams.append(dict(qz=qz, load_k=load_k, load_vt=load_vt, shifts=(0.0,), bias_fns=(None,)))
    outs = _flash_pipelined(streams, ((0, nk),), _flash_bufs(scratch))
    for qi in range(MLA_Q_BLOCKS):
        o_t = jnp.concatenate(outs[2 * qi:2 * qi + 2], axis=0)
        o_ref[qi * tq:(qi + 1) * tq, :] = o_t.T.astype(o_ref.dtype)


def _mla_call(q_t, k, v_t):
    B, nq, _, tq = q_t.shape
    S = k.shape[1]
    nk, _, tk = v_t.shape[1:]
    nqb = MLA_Q_BLOCKS
    return pl.pallas_call(
        functools.partial(_mla_kernel, nk=nk),
        grid=(B, 2, nq // nqb),
        in_specs=[
            pl.BlockSpec((None, nqb, 2 * PAIR, tq), lambda b, p, i: (b, i, p, 0)),
            _resident((None, S, 2 * PAIR), lambda b, p, i: (b, 0, p)),
            _resident((None, nk, PAIR, tk), lambda b, p, i: (b, 0, p, 0)),
        ],
        out_specs=pl.BlockSpec((None, nqb * tq, PAIR), lambda b, p, i: (b, i, p)),
        out_shape=jax.ShapeDtypeStruct((B, S, GROUP), BF16),
        scratch_shapes=_flash_scratch(2 * nqb, tk, tq),
        compiler_params=pltpu.CompilerParams(
            dimension_semantics=("parallel", "parallel", "parallel"), vmem_limit_bytes=V7X_VMEM_LIMIT),
        name="mla_attn",
    )(q_t, k, v_t)


def _post_kernel(x_ref, oa_ref, ob_ref, oc_ref, od_ref, wo_ref, g_ref, wg_ref, wu_ref, wd_ref, gf_ref,
                 out_ref, acc_ref, h_ref, *, final):
    f = pl.program_id(1)

    @pl.when(f == 0)
    def _():
        x1 = x_ref[...]
        for gi, o_ref in enumerate((oa_ref, ob_ref, oc_ref, od_ref)):
            x1 = x1 + jnp.dot(o_ref[...], wo_ref[gi], preferred_element_type=F32)
        acc_ref[...] = x1
        h_ref[...] = _rms(x1, g_ref[...]).astype(BF16)

    h = h_ref[...]
    gate = jnp.dot(h, wg_ref[...], preferred_element_type=F32)
    up = jnp.dot(h, wu_ref[...], preferred_element_type=F32)
    act = (gate / (1.0 + jnp.exp(-gate))) * up
    acc_ref[...] += jnp.dot(act.astype(BF16), wd_ref[...], preferred_element_type=F32)

    @pl.when(f == pl.num_programs(1) - 1)
    def _():
        y = acc_ref[...]
        out_ref[...] = _rms(y, gf_ref[...]) if final else y


def _post_call(x, o_a, o_b, o_c, o_d, p, g_final, final):
    N, D = x.shape
    ts = TOK_TILE
    d_ff = p['w_gate'].shape[1]
    fc = d_ff // FF_SPLIT
    tok = lambda w: pl.BlockSpec((ts, w), lambda t, f: (t, 0))
    return pl.pallas_call(
        functools.partial(_post_kernel, final=final),
        grid=(N // ts, FF_SPLIT),
        in_specs=[tok(D), tok(GROUP), tok(GROUP), tok(GROUP), tok(GROUP),
                  pl.BlockSpec((4, GROUP, D), lambda t, f: (0, 0, 0)),
                  pl.BlockSpec((1, D), lambda t, f: (0, 0)),
                  pl.BlockSpec((D, fc), lambda t, f: (0, f)),
                  pl.BlockSpec((D, fc), lambda t, f: (0, f)),
                  pl.BlockSpec((fc, D), lambda t, f: (f, 0)),
                  pl.BlockSpec((1, D), lambda t, f: (0, 0))],
        out_specs=tok(D),
        out_shape=jax.ShapeDtypeStruct((N, D), F32),
        scratch_shapes=[pltpu.VMEM((ts, D), F32), pltpu.VMEM((ts, D), BF16)],
        compiler_params=pltpu.CompilerParams(
            dimension_semantics=("parallel", "arbitrary"), vmem_limit_bytes=V7X_VMEM_LIMIT),
        name="post_ffn",
    )(x, o_a, o_b, o_c, o_d, p['w_o'], p['g_ffn'], p['w_gate'], p['w_up'], p['w_down'], g_final)


def _t5_bucket(rel):
    nb = T5_BUCKETS // 2
    max_exact = nb // 2
    side = jnp.where(rel > 0, nb, 0)
    n = jnp.abs(rel)
    large = max_exact + (jnp.log(jnp.maximum(n, 1).astype(F32) / max_exact)
                         / math.log(T5_MAX_DIST / max_exact) * (nb - max_exact)).astype(jnp.int32)
    large = jnp.minimum(large, nb - 1)
    return side + jnp.where(n < max_exact, n, large)


def _lookup(table, idx):
    onehot = idx[..., None, None] == jnp.arange(table.shape[0])[:, None]
    return jnp.sum(jnp.where(onehot, table.astype(F32), 0.0), axis=-2)


def _toeplitz(w, t):
    lead = w.shape[:-1]
    w_pad = jnp.concatenate([w, jnp.zeros(lead + (1,), w.dtype)], axis=-1)
    m = jnp.tile(w_pad, t)[..., :t * (2 * t - 1)].reshape(lead + (t, 2 * t - 1))
    return m[..., t - 1:]


def _toeplitz_tiles(vec_fn, t, n_side):
    reach = n_side * t + t - 1
    w = jnp.moveaxis(vec_fn(jnp.arange(-reach, reach + 1)), -1, 0)
    tiles = [_toeplitz(w[:, (e + n_side) * t:(e + n_side + 2) * t - 1], t) for e in range(-n_side, n_side + 1)]
    return jnp.swapaxes(jnp.stack(tiles), -1, -2)


def _diff_bias(t5_diff):
    def vec(rel):
        return _lookup(t5_diff, _t5_bucket(rel)) * LOG2E
    tiles = _toeplitz_tiles(vec, BAND_T, 4)
    far = vec(jnp.array([-2 * T5_MAX_DIST, 2 * T5_MAX_DIST]))
    return tiles, far


def _dil_bias(t5_dil):
    def vec(rel):
        n = jnp.abs(rel)
        mult = jnp.zeros(rel.shape, F32)
        for window, dil in DIL_PATTERNS:
            mult = mult + ((rel % dil == 0) & (n <= window // 2)).astype(F32)
        b = _lookup(t5_dil, _t5_bucket(rel)) + jnp.log(jnp.maximum(mult, 1.0))[..., None]
        return jnp.where((mult > 0)[..., None], b * LOG2E, NEG)
    return _toeplitz_tiles(vec, BAND_T, DIL_BAND + DIL_Q_TILES - 1)


def _na_bias(rpb):
    d = jnp.arange(NA_ROWS)[:, None]
    ridx = jnp.arange(NA_ROWS)[None, :] - d + NA_ROWS - 1
    qc = jnp.arange(GRID_W)[:, None]
    kc = jnp.arange(GRID_W)[None, :]
    c0 = jnp.clip(qc - NA_COLS // 2, 0, GRID_W - NA_COLS)
    valid = (kc >= c0) & (kc < c0 + NA_COLS)
    rows = rpb.astype(F32)[:, ridx]
    side = GRID_W - NA_COLS
    vals = _toeplitz(jnp.pad(rows, ((0, 0),) * 3 + ((side, side),)), GRID_W)
    vals = jnp.where(valid, vals * LOG2E, NEG)
    vals = vals.transpose(1, 0, 3, 2, 4)
    return vals.reshape(NA_ROWS, N_HEADS, GRID_W, NA_ROWS * GRID_W)


def _rope_tables(S):
    inv_freq = ROPE_THETA ** (-jnp.arange(MLA_HALF, dtype=F32) / MLA_HALF)
    ang = jnp.arange(S).astype(F32)[:, None] * inv_freq[None, :]
    cos, sin = jnp.cos(ang), jnp.sin(ang)
    csn = jnp.concatenate([cos, cos, -sin, sin, jnp.zeros((S, PAIR - 4 * MLA_HALF), F32)], axis=1)
    nt = S // TOK_TILE
    to_t = lambda a: a.reshape(nt, TOK_TILE, MLA_HALF).transpose(0, 2, 1)
    return {'csn': csn, 'cos_t': to_t(cos), 'sin_t': to_t(sin)}


def _layer_weights(w_in, w_uq, w_ukv, w_o, w_gate, w_up, w_down, g_q, g_kv, g_ffn):
    G = GROUP
    col = lambda i: w_in[:, i * G:(i + 1) * G]
    cq, ckv, kr = w_in[:, 9 * G:10 * G], w_in[:, 10 * G:10 * G + 128], w_in[:, 10 * G + 128:]
    kr_swap = jnp.concatenate([kr[:, MLA_HALF:], kr[:, :MLA_HALF]], axis=1)
    kr2 = jnp.concatenate([kr, kr_swap, jnp.zeros((w_in.shape[0], PAIR - 2 * MLA_ROPE), w_in.dtype)], axis=1)
    w_nat = jnp.concatenate([col(0), col(1), col(2), col(4), col(7), cq, ckv, kr2], axis=1)
    w_t = jnp.concatenate([col(3), col(5), col(6), col(8)], axis=1).T
    uq = w_uq.reshape(w_uq.shape[0], N_HEADS, MLA_QK)
    uq = jnp.pad(uq, ((0, 0), (0, 0), (0, PAIR - MLA_QK))).reshape(w_uq.shape[0], N_HEADS * PAIR)
    ukv = w_ukv.reshape(w_ukv.shape[0], N_HEADS, 2 * HEAD_DIM)
    uk = ukv.at[:, :, HEAD_DIM:].set(0.0).reshape(w_ukv.shape[0], N_HEADS * PAIR)
    uv = ukv[:, :, HEAD_DIM:].reshape(w_ukv.shape[0], GROUP)
    place = jnp.zeros((PAIR, N_HEADS * PAIR), F32)
    r = jnp.arange(MLA_ROPE)
    for hd in range(N_HEADS):
        place = place.at[r, hd * PAIR + HEAD_DIM + r].set(1.0)
    b = lambda a: a.astype(BF16)
    row = lambda a: a.astype(F32)[None, :]
    return {
        'w_nat': b(w_nat), 'w_t': b(w_t), 'w_uq_t': b(uq.T), 'w_uk': b(uk), 'w_uv_t': b(uv.T),
        'place': b(place), 'g_q': row(g_q), 'g_kv': row(g_kv), 'g_ffn': row(g_ffn),
        'w_o': b(w_o.reshape(4, GROUP, w_o.shape[1])), 'w_gate': b(w_gate), 'w_up': b(w_up),
        'w_down': b(w_down),
    }


def kernel(x, attn_norm, w_in, na_rpb, diff_lambda, diff_subln, mla_q_norm, w_uq, mla_kv_norm, w_ukv,
           t5_table, w_o, ffn_norm, w_gate, w_up, w_down, final_norm):
    B, S, D = x.shape
    depth = w_in.shape[0]
    rope = _rope_tables(S)
    diff_tiles, diff_far = _diff_bias(t5_table[:, :N_HEADS])
    dil_tiles = _dil_bias(t5_table[:, N_HEADS:])
    g_final = final_norm.astype(F32)[None, :]

    for l in range(depth):
        p = _layer_weights(w_in[l], w_uq[l], w_ukv[l], w_o[l], w_gate[l], w_up[l], w_down[l],
                           mla_q_norm[l], mla_kv_norm[l], ffn_norm[l])
        (qa, ka, va, kb, kc, kd, qb_t, vb_t, qc_t, vc_t, qd_t, vd_t) = _proj_call(
            x, attn_norm[l].astype(F32)[None, :], p, rope)

        o_a = _na_call(qa, ka, va, _na_bias(na_rpb[l]))

        lam_init = 0.8 - 0.6 * math.exp(-0.3 * l)
        lq1, lk1, lq2, lk2 = [diff_lambda[l, j].astype(F32) for j in range(4)]
        lam = jnp.exp(jnp.sum(lq1 * lk1)) - jnp.exp(jnp.sum(lq2 * lk2)) + lam_init
        far = diff_far.reshape(2, 2, 2).transpose(1, 0, 2).reshape(2, 4)
        consts = jnp.concatenate([jnp.broadcast_to(lam, (2, 1)), far, jnp.zeros((2, 3), F32)], axis=1)
        consts = jnp.broadcast_to(consts[:, :, None], (2, 8, 128))
        g_col = jnp.broadcast_to(diff_subln[l].astype(F32)[:, None], (HEAD_DIM, 128))
        o_b = _diff_call(consts, g_col, qb_t, kb, vb_t, diff_tiles, 1.0 - lam_init)

        o_c = _dil_call(qc_t, kc, vc_t, dil_tiles)
        o_d = _mla_call(qd_t, kd, vd_t)

        flat = lambda a: a.reshape(B * S, a.shape[-1])
        x = _post_call(flat(x), flat(o_a), flat(o_b), flat(o_c), flat(o_d), p, g_final,
                       final=(l == depth - 1)).reshape(B, S, D)
    return x
```

```python
import functools
import math

import jax
import jax.numpy as jnp
from jax import lax
from jax.experimental import pallas as pl
from jax.experimental.pallas import tpu as pltpu

F32 = jnp.float32
BF16 = jnp.bfloat16

N_HEADS = 4
HEAD_DIM = 64
GROUP = N_HEADS * HEAD_DIM
PAIR = 2 * HEAD_DIM
GRID_W = 64
NA_ROWS = 8
NA_COLS = 16
DIFF_QK = HEAD_DIM // 2
DIL_PATTERNS = ((128, 1), (512, 4), (2048, 16))
MLA_ROPE = HEAD_DIM // 2
MLA_HALF = MLA_ROPE // 2
MLA_QK = HEAD_DIM + MLA_ROPE
ROPE_THETA = 10000.0
T5_BUCKETS = 32
T5_MAX_DIST = 1024
EPS = 1e-6
LOG2E = 1.4426950408889634
NEG = -1e30
SUM_ROWS = 16
EXP_CHUNK = 32

V7X_VMEM_LIMIT = 56 * 1024 * 1024

TOK_TILE = 512
FULL_T = 512
BAND_T = 256
DIL_BAND = -(-max(w // 2 for w, _ in DIL_PATTERNS) // BAND_T)
DIL_Q_TILES = 4
MLA_Q_BLOCKS = 4
DIFF_Q_BLOCKS = 2
NA_GROUP = 8
NA_ROWS_PER_ITER = 2
FF_SPLIT = 2


def _rms(x, g):
    return x * lax.rsqrt(jnp.mean(x * x, axis=-1, keepdims=True) + EPS) * g


def _nt(a, b):
    return lax.dot_general(a, b, (((1,), (1,)), ((), ())), preferred_element_type=F32)


def _zero_of(x):
    u = lax.bitcast_convert_type(x, jnp.uint32)
    return lax.bitcast_convert_type((u >> 16) >> 16, F32)


def _flash_pipelined(streams, ranges, bufs):
    tk, tq = bufs[0][0].shape
    first = ranges[0][0]
    last = ranges[-1][1] - 1
    ones = jnp.ones((SUM_ROWS, tk), BF16)

    def scores(st, j, bias_fn, s_ref):
        s = jnp.dot(st['load_k'](j), st['qz'], preferred_element_type=F32)
        if bias_fn is not None:
            s = s + bias_fn(j)
        s_ref[...] = s
        return jnp.max(s, axis=0, keepdims=True)

    def accumulate(st, buf, j, par):
        acc_ref, stat_ref = buf[4], buf[5]
        v_aug = jnp.concatenate([st['load_vt'](j), ones], axis=0)
        acc_ref[...] = (stat_ref[_ALPHA + par:_ALPHA + par + 1, :] * acc_ref[...]
                        + jnp.dot(v_aug, buf[2 + par][...], preferred_element_type=F32))

    def softmax_update(buf, par, s_new):
        stat_ref = buf[5]
        m = stat_ref[_MAX:_MAX + 1, :]
        c = stat_ref[_CONST:_CONST + 1, :]
        m_new = jnp.maximum(m, stat_ref[_CMAX:_CMAX + 1, :] + c)
        ref_rows = jnp.broadcast_to(m_new - c, (8, tq))
        for lo_row in range(0, tk, EXP_CHUNK):
            rows = slice(lo_row, lo_row + EXP_CHUNK)
            ref = ref_rows if s_new is None else ref_rows + _zero_of(s_new[lo_row:lo_row + 8, :])
            x = buf[1 - par][rows, :].reshape(EXP_CHUNK // 8, 8, tq) - ref[None]
            buf[3 - par][rows, :] = jnp.exp2(x.reshape(EXP_CHUNK, tq).astype(BF16))
        stat_ref[_ALPHA + 1 - par:_ALPHA + 2 - par, :] = jnp.exp2(m - m_new)
        stat_ref[_MAX:_MAX + 1, :] = m_new

    def step(par, r, t):
        for st, buf in zip(streams, bufs):
            stat_ref = buf[5]
            s_new = jnp.dot(st['load_k'](t), st['qz'], preferred_element_type=F32)
            if st['bias_fns'][r] is not None:
                s_new = s_new + st['bias_fns'][r](t)
            buf[par][...] = s_new
            cmax_t = jnp.max(s_new, axis=0, keepdims=True)
            softmax_update(buf, par, s_new)
            stat_ref[_CMAX:_CMAX + 1, :] = cmax_t
            stat_ref[_CONST:_CONST + 1, :] = jnp.zeros((1, tq), F32) + st['shifts'][r]
            accumulate(st, buf, jnp.maximum(t - 2, first), par)

    def drain(par):
        for st, buf in zip(streams, bufs):
            softmax_update(buf, par, None)
            accumulate(st, buf, jnp.maximum(last - 1, first), par)
            accumulate(st, buf, last, 1 - par)

    for st, buf in zip(streams, bufs):
        acc_ref, stat_ref = buf[4], buf[5]
        cmax = scores(st, first, st.get('first_bias', st['bias_fns'][0]), buf[0])
        buf[3][...] = jnp.zeros_like(buf[3])
        acc_ref[...] = jnp.zeros_like(acc_ref)
        stat_ref[_ALPHA:_ALPHA + 2, :] = jnp.ones((2, tq), F32)
        stat_ref[_MAX:_MAX + 1, :] = jnp.full((1, tq), NEG, F32)
        stat_ref[_CMAX:_CMAX + 1, :] = cmax
        stat_ref[_CONST:_CONST + 1, :] = (jnp.zeros((1, tq), F32)
                                          + (0.0 if 'first_bias' in st else st['shifts'][0]))

    def by_parity(j, fn):
        par = (j - first) % 2
        if isinstance(par, int):
            fn(par)
        else:
            pl.when(par == 0)(functools.partial(fn, 0))
            pl.when(par == 1)(functools.partial(fn, 1))

    for r, (lo, hi) in enumerate(ranges):

        def body(t, c, r=r):
            by_parity(t, lambda par: step(par, r, t))
            return c

        lax.fori_loop(jnp.maximum(lo, first + 1), hi, body, 0)

    by_parity(last + 1, drain)
    outs = []
    for buf in bufs:
        acc = buf[4][...]
        outs.append(acc[:HEAD_DIM] / acc[HEAD_DIM:HEAD_DIM + 1])
    return outs


_ALPHA, _MAX, _CMAX, _CONST = 0, 2, 3, 4
_BUFS_PER_STREAM = 6


def _flash_scratch(n_streams, tk, tq):
    per_stream = [pltpu.VMEM((tk, tq), F32), pltpu.VMEM((tk, tq), F32),
                  pltpu.VMEM((tk, tq), BF16), pltpu.VMEM((tk, tq), BF16),
                  pltpu.VMEM((HEAD_DIM + SUM_ROWS, tq), F32), pltpu.VMEM((8, tq), F32)]
    return per_stream * n_streams


def _flash_bufs(refs):
    n = _BUFS_PER_STREAM
    return [tuple(refs[n * k:n * k + n]) for k in range(len(refs) // n)]


def _store_t_blocks(ref, val, blk):
    for c in range(val.shape[1] // blk):
        ref[c] = val[:, c * blk:(c + 1) * blk].astype(ref.dtype)


def _proj_kernel(x_ref, g_ref, wnat_ref, wt_ref, gq_ref, wuqt_ref, gkv_ref, wuk_ref, wuvt_ref,
                 place_ref, csn_ref, cost_ref, sint_ref,
                 qa_ref, ka_ref, va_ref, kb_ref, kc_ref, kd_ref,
                 qbt_ref, vbt_ref, qct_ref, vct_ref, qdt_ref, vdt_ref):
    h = _rms(x_ref[...], g_ref[...]).astype(BF16)
    nat = jnp.dot(h, wnat_ref[...], preferred_element_type=F32)
    qa_ref[...] = (nat[:, 0:256] * (HEAD_DIM ** -0.5 * LOG2E)).astype(BF16)
    ka_ref[...] = nat[:, 256:512].astype(BF16)
    va_ref[...] = nat[:, 512:768].astype(BF16)
    kb_ref[...] = nat[:, 768:1024].astype(BF16)
    kc_ref[...] = nat[:, 1024:1280].astype(BF16)
    cq = nat[:, 1280:1536]
    ckv = nat[:, 1536:1664]
    kr2 = nat[:, 1664:1792]

    tr = _nt(wt_ref[...], h)
    _store_t_blocks(qbt_ref, tr[0:256] * (DIFF_QK ** -0.5 * LOG2E), FULL_T)
    _store_t_blocks(vbt_ref, tr[256:512], FULL_T)
    _store_t_blocks(qct_ref, tr[512:768] * (HEAD_DIM ** -0.5 * LOG2E), BAND_T)
    _store_t_blocks(vct_ref, tr[768:1024], BAND_T)

    cqn = _rms(cq, gq_ref[...]).astype(BF16)
    qd_t = _nt(wuqt_ref[...], cqn)
    cos_t = cost_ref[...]
    sin_t = sint_ref[...]
    pieces = []
    for hd in range(N_HEADS):
        b0 = hd * PAIR
        x1 = qd_t[b0 + 64:b0 + 80]
        x2 = qd_t[b0 + 80:b0 + 96]
        pieces += [qd_t[b0:b0 + 64], x1 * cos_t - x2 * sin_t, x1 * sin_t + x2 * cos_t,
                   qd_t[b0 + 96:b0 + 128]]
    qd_t = jnp.concatenate(pieces, axis=0) * (MLA_QK ** -0.5 * LOG2E)
    _store_t_blocks(qdt_ref, qd_t, FULL_T)

    ckvn = _rms(ckv, gkv_ref[...]).astype(BF16)
    k_nope = jnp.dot(ckvn, wuk_ref[...], preferred_element_type=F32)
    t = kr2 * csn_ref[...]
    roped = t + pltpu.roll(t, 96, 1)
    kd = k_nope + jnp.dot(roped.astype(BF16), place_ref[...], preferred_element_type=F32)
    kd_ref[...] = kd.astype(BF16)
    _store_t_blocks(vdt_ref, _nt(wuvt_ref[...], ckvn), FULL_T)


def _proj_call(x, g, p, rope):
    B, S, D = x.shape
    ts = TOK_TILE
    nt = S // ts
    nf, nb = ts // FULL_T, ts // BAND_T

    def full(a):
        return pl.BlockSpec(a.shape, lambda b, t: (0,) * a.ndim)

    nat_spec = lambda w: pl.BlockSpec((None, ts, w), lambda b, t: (b, t, 0))
    tb_spec = lambda n, r, blk: pl.BlockSpec((None, n, r, blk), lambda b, t: (b, t, 0, 0))
    nat_shape = lambda w: jax.ShapeDtypeStruct((B, S, w), BF16)
    tb_shape = lambda r, blk: jax.ShapeDtypeStruct((B, S // blk, r, blk), BF16)

    weights = [g, p['w_nat'], p['w_t'], p['g_q'], p['w_uq_t'], p['g_kv'], p['w_uk'], p['w_uv_t'], p['place']]
    in_specs = ([nat_spec(D)] + [full(a) for a in weights] +
                [pl.BlockSpec((ts, PAIR), lambda b, t: (t, 0)),
                 pl.BlockSpec((None, MLA_HALF, ts), lambda b, t: (t, 0, 0)),
                 pl.BlockSpec((None, MLA_HALF, ts), lambda b, t: (t, 0, 0))])
    out_shape = ([nat_shape(GROUP)] * 5 + [nat_shape(2 * GROUP)] +
                 [tb_shape(GROUP, FULL_T)] * 2 + [tb_shape(GROUP, BAND_T)] * 2 +
                 [tb_shape(2 * GROUP, FULL_T), tb_shape(GROUP, FULL_T)])
    out_specs = ([nat_spec(GROUP)] * 5 + [nat_spec(2 * GROUP)] +
                 [tb_spec(nf, GROUP, FULL_T)] * 2 + [tb_spec(nb, GROUP, BAND_T)] * 2 +
                 [tb_spec(nf, 2 * GROUP, FULL_T), tb_spec(nf, GROUP, FULL_T)])
    return pl.pallas_call(
        _proj_kernel,
        grid=(B, nt),
        in_specs=in_specs,
        out_specs=out_specs,
        out_shape=out_shape,
        compiler_params=pltpu.CompilerParams(
            dimension_semantics=("parallel", "parallel"), vmem_limit_bytes=V7X_VMEM_LIMIT),
        name="proj",
    )(x, *weights, rope['csn'], rope['cos_t'], rope['sin_t'])


def _na_kernel(q_ref, kp_ref, kc_ref, kn_ref, vp_ref, vc_ref, vn_ref, bias_ref, o_ref, kk_ref, vv_ref,
               *, n_rows):
    g = pl.program_id(1)
    blk = NA_GROUP * GRID_W
    win = NA_ROWS * GRID_W
    for c, (kr, vr) in enumerate(((kp_ref, vp_ref), (kc_ref, vc_ref), (kn_ref, vn_ref))):
        kk_ref[c * blk:(c + 1) * blk, :] = kr[...]
        vv_ref[c * blk:(c + 1) * blk, :] = vr[...]
    lane = lax.broadcasted_iota(jnp.int32, (GRID_W, PAIR), 1)

    first_head = lane < HEAD_DIM

    def rows_body(jj, carry):
        v_pairs, scores, probs = [], [], []
        for dj in range(NA_ROWS_PER_ITER):
            j = jj * NA_ROWS_PER_ITER + dj
            r = g * NA_GROUP + j
            r0 = jnp.clip(r - NA_ROWS // 2, 0, n_rows - NA_ROWS)
            start = pl.multiple_of((r0 - (g - 1) * NA_GROUP) * GRID_W, GRID_W)
            q_row = q_ref[pl.ds(pl.multiple_of(j * GRID_W, GRID_W), GRID_W), :]
            k_win = kk_ref[pl.ds(start, win), :]
            v_win = vv_ref[pl.ds(start, win), :]
            for pr in range(2):
                q2 = q_row[:, pr * PAIR:(pr + 1) * PAIR]
                zero = jnp.zeros_like(q2)
                qz = jnp.concatenate([jnp.where(first_head, q2, zero), jnp.where(first_head, zero, q2)], axis=0)
                bias = jnp.concatenate([bias_ref[r - r0, 2 * pr], bias_ref[r - r0, 2 * pr + 1]], axis=0)
                scores.append(_nt(qz, k_win[:, pr * PAIR:(pr + 1) * PAIR]) + bias)
                v_pairs.append(v_win[:, pr * PAIR:(pr + 1) * PAIR])
        for s in scores:
            e = jnp.exp2(s - jnp.max(s, axis=-1, keepdims=True))
            probs.append((e.astype(BF16), jnp.sum(e, axis=-1, keepdims=True)))
        outs = []
        for (e, den), v2 in zip(probs, v_pairs):
            o2 = jnp.dot(e, v2, preferred_element_type=F32) / den
            outs.append(jnp.where(first_head, o2[:GRID_W], o2[GRID_W:]))
        for dj in range(NA_ROWS_PER_ITER):
            qs = pl.multiple_of((jj * NA_ROWS_PER_ITER + dj) * GRID_W, GRID_W)
            o_ref[pl.ds(qs, GRID_W), :] = jnp.concatenate(outs[2 * dj:2 * dj + 2], axis=1).astype(o_ref.dtype)
        return carry

    lax.fori_loop(0, NA_GROUP // NA_ROWS_PER_ITER, rows_body, 0)


def _na_call(q, k, v, bias):
    B, S, _ = q.shape
    n_rows = S // GRID_W
    blk = NA_GROUP * GRID_W
    ng = S // blk
    cur = pl.BlockSpec((None, blk, GROUP), lambda b, g: (b, g, 0))
    prv = pl.BlockSpec((None, blk, GROUP), lambda b, g: (b, jnp.maximum(g - 1, 0), 0))
    nxt = pl.BlockSpec((None, blk, GROUP), lambda b, g: (b, jnp.minimum(g + 1, ng - 1), 0))
    return pl.pallas_call(
        functools.partial(_na_kernel, n_rows=n_rows),
        grid=(B, ng),
        in_specs=[cur, prv, cur, nxt, prv, cur, nxt,
                  pl.BlockSpec(bias.shape, lambda b, g: (0, 0, 0, 0))],
        out_specs=cur,
        out_shape=jax.ShapeDtypeStruct((B, S, GROUP), BF16),
        scratch_shapes=[pltpu.VMEM((3 * blk, GROUP), BF16), pltpu.VMEM((3 * blk, GROUP), BF16)],
        compiler_params=pltpu.CompilerParams(
            dimension_semantics=("parallel", "parallel"), vmem_limit_bytes=V7X_VMEM_LIMIT),
        name="na_attn",
    )(q, k, k, k, v, v, v, bias)


def _diff_kernel(c_ref, g_ref, qt_ref, k_ref, vt_ref, bias_ref, o_ref, *scratch, nk, lam_scale):
    i0 = pl.program_id(2) * DIFF_Q_BLOCKS
    tk = vt_ref.shape[2]
    tq = qt_ref.shape[2]
    row = lax.broadcasted_iota(jnp.int32, (PAIR, tq), 0)
    lam = c_ref[0:1, 0:1]
    j_lo = jnp.maximum(i0 - 2, 0)
    j_hi = jnp.minimum(i0 + DIFF_Q_BLOCKS + 2, nk)
    ranges = ((0, j_lo), (j_lo, j_hi), (j_hi, nk))

    def load_k(j):
        return k_ref[pl.ds(pl.multiple_of(j * tk, tk), tk), :]

    streams = []
    for qi in range(DIFF_Q_BLOCKS):
        q_t = qt_ref[qi]
        for hh in range(2):

            def load_vt(j, hh=hh):
                return vt_ref[j, hh * HEAD_DIM:(hh + 1) * HEAD_DIM, :]

            def near_bias(j, hh=hh, qi=qi):
                e0 = 2 * (j - i0 - qi) + 4

                def tile(d):
                    return bias_ref[jnp.clip(e0 + d, 0, 8), hh]
                t0 = tile(0)
                return jnp.concatenate(
                    [jnp.concatenate([t0, tile(-1)], axis=1),
                     jnp.concatenate([tile(1), t0], axis=1)], axis=0)

            for c in range(2):
                r_lo = hh * HEAD_DIM + c * DIFF_QK
                qz = jnp.where((row >= r_lo) & (row < r_lo + DIFF_QK), q_t, jnp.zeros_like(q_t))
                streams.append(dict(qz=qz, load_k=load_k, load_vt=load_vt,
                                    shifts=(c_ref[1 + hh:2 + hh, 0:1], 0.0, c_ref[3 + hh:4 + hh, 0:1]),
                                    bias_fns=(None, near_bias, None), first_bias=near_bias))
    maps = _flash_pipelined(streams, ranges, _flash_bufs(scratch))
    for qi in range(DIFF_Q_BLOCKS):
        heads = []
        for hh in range(2):
            k0 = 4 * qi + 2 * hh
            o = maps[k0] - lam * maps[k0 + 1]
            y = o * lax.rsqrt(jnp.mean(o * o, axis=0, keepdims=True) + EPS) * g_ref[:, 0:1]
            heads.append(y * lam_scale)
        o_ref[qi * tq:(qi + 1) * tq, :] = jnp.concatenate(heads, axis=0).T.astype(o_ref.dtype)


def _resident(block_shape, index_map):
    return pl.BlockSpec(block_shape, index_map, pipeline_mode=pl.Buffered(1))


def _diff_call(consts, g_col, q_t, k, v_t, bias, lam_scale):
    B, nq, _, tq = q_t.shape
    S = k.shape[1]
    nk, _, tk = v_t.shape[1:]
    nqb = DIFF_Q_BLOCKS
    return pl.pallas_call(
        functools.partial(_diff_kernel, nk=nk, lam_scale=lam_scale),
        grid=(B, 2, nq // nqb),
        in_specs=[
            pl.BlockSpec((None, 8, 128), lambda b, p, i: (p, 0, 0)),
            pl.BlockSpec(g_col.shape, lambda b, p, i: (0, 0)),
            pl.BlockSpec((None, nqb, PAIR, tq), lambda b, p, i: (b, i, p, 0)),
            _resident((None, S, PAIR), lambda b, p, i: (b, 0, p)),
            _resident((None, nk, PAIR, tk), lambda b, p, i: (b, 0, p, 0)),
            _resident((9, 2, BAND_T, BAND_T), lambda b, p, i: (0, p, 0, 0)),
        ],
        out_specs=pl.BlockSpec((None, nqb * tq, PAIR), lambda b, p, i: (b, i, p)),
        out_shape=jax.ShapeDtypeStruct((B, S, GROUP), BF16),
        scratch_shapes=_flash_scratch(4 * nqb, tk, tq),
        compiler_params=pltpu.CompilerParams(
            dimension_semantics=("parallel", "parallel", "parallel"), vmem_limit_bytes=V7X_VMEM_LIMIT),
        name="diff_attn",
    )(consts, g_col, q_t, k, v_t, bias)


def _dil_kernel(qt_ref, k_ref, vt_ref, bias_ref, o_ref, *scratch, nk):
    i = pl.program_id(1)
    t = BAND_T
    q_t = jnp.concatenate([qt_ref[a] for a in range(DIL_Q_TILES)], axis=1)
    row = lax.broadcasted_iota(jnp.int32, (PAIR, q_t.shape[1]), 0)
    j0 = i * DIL_Q_TILES
    j_lo = jnp.maximum(j0 - DIL_BAND, 0)
    j_hi = jnp.minimum(j0 + DIL_Q_TILES + DIL_BAND, nk)
    streams = []
    for head in range(N_HEADS):
        pr, hh = divmod(head, 2)

        def load_k(j, pr=pr):
            return k_ref[pl.ds(pl.multiple_of(j * t, t), t), pr * PAIR:(pr + 1) * PAIR]

        def load_vt(j, head=head):
            return vt_ref[j, head * HEAD_DIM:(head + 1) * HEAD_DIM, :]

        def bias(j, head=head):
            e0 = j - j0 + DIL_BAND + DIL_Q_TILES - 1
            return jnp.concatenate([bias_ref[e0 - a, head] for a in range(DIL_Q_TILES)], axis=1)

        q_pair = q_t[pr * PAIR:(pr + 1) * PAIR]
        qz = jnp.where((row // HEAD_DIM) == hh, q_pair, jnp.zeros_like(q_pair))
        streams.append(dict(qz=qz, load_k=load_k, load_vt=load_vt, shifts=(0.0,), bias_fns=(bias,)))
    heads = _flash_pipelined(streams, ((j_lo, j_hi),), _flash_bufs(scratch))
    o_ref[...] = jnp.concatenate(heads, axis=0).T.astype(o_ref.dtype)


def _dil_call(q_t, k, v_t, bias):
    B, nq, _, t = q_t.shape
    S = k.shape[1]
    nk = v_t.shape[1]
    nqt = DIL_Q_TILES
    return pl.pallas_call(
        functools.partial(_dil_kernel, nk=nk),
        grid=(B, nq // nqt),
        in_specs=[
            pl.BlockSpec((None, nqt, GROUP, t), lambda b, i: (b, i, 0, 0)),
            _resident((None, S, GROUP), lambda b, i: (b, 0, 0)),
            _resident((None, nk, GROUP, t), lambda b, i: (b, 0, 0, 0)),
            _resident(bias.shape, lambda b, i: (0, 0, 0, 0)),
        ],
        out_specs=pl.BlockSpec((None, nqt * t, GROUP), lambda b, i: (b, i, 0)),
        out_shape=jax.ShapeDtypeStruct((B, S, GROUP), BF16),
        scratch_shapes=_flash_scratch(N_HEADS, t, nqt * t),
        compiler_params=pltpu.CompilerParams(
            dimension_semantics=("parallel", "parallel"), vmem_limit_bytes=V7X_VMEM_LIMIT),
        name="dil_attn",
    )(q_t, k, v_t, bias)


def _mla_kernel(qt_ref, k_ref, vt_ref, o_ref, *scratch, nk):
    tk = vt_ref.shape[2]
    tq = qt_ref.shape[2]
    streams = []
    for qi in range(MLA_Q_BLOCKS):
        for hh in range(2):

            def load_k(j, hh=hh):
                return k_ref[pl.ds(pl.multiple_of(j * tk, tk), tk), hh * PAIR:(hh + 1) * PAIR]

            def load_vt(j, hh=hh):
                return vt_ref[j, hh * HEAD_DIM:(hh + 1) * HEAD_DIM, :]

            qz = qt_ref[qi, hh * PAIR:(hh + 1) * PAIR, :]
            streams.append(dict(qz=qz, load_k=load_k, load_vt=load_vt, shifts=(0.0,), bias_fns=(None,)))
    outs = _flash_pipelined(streams, ((0, nk),), _flash_bufs(scratch))
    for qi in range(MLA_Q_BLOCKS):
        o_t = jnp.concatenate(outs[2 * qi:2 * qi + 2], axis=0)
        o_ref[qi * tq:(qi + 1) * tq, :] = o_t.T.astype(o_ref.dtype)


def _mla_call(q_t, k, v_t):
    B, nq, _, tq = q_t.shape
    S = k.shape[1]
    nk, _, tk = v_t.shape[1:]
    nqb = MLA_Q_BLOCKS
    return pl.pallas_call(
        functools.partial(_mla_kernel, nk=nk),
        grid=(B, 2, nq // nqb),
        in_specs=[
            pl.BlockSpec((None, nqb, 2 * PAIR, tq), lambda b, p, i: (b, i, p, 0)),
            _resident((None, S, 2 * PAIR), lambda b, p, i: (b, 0, p)),
            _resident((None, nk, PAIR, tk), lambda b, p, i: (b, 0, p, 0)),
        ],
        out_specs=pl.BlockSpec((None, nqb * tq, PAIR), lambda b, p, i: (b, i, p)),
        out_shape=jax.ShapeDtypeStruct((B, S, GROUP), BF16),
        scratch_shapes=_flash_scratch(2 * nqb, tk, tq),
        compiler_params=pltpu.CompilerParams(
            dimension_semantics=("parallel", "parallel", "parallel"), vmem_limit_bytes=V7X_VMEM_LIMIT),
        name="mla_attn",
    )(q_t, k, v_t)


def _post_kernel(x_ref, oa_ref, ob_ref, oc_ref, od_ref, wo_ref, g_ref, wg_ref, wu_ref, wd_ref, gf_ref,
                 out_ref, acc_ref, h_ref, *, final):
    f = pl.program_id(1)

    @pl.when(f == 0)
    def _():
        x1 = x_ref[...]
        for gi, o_ref in enumerate((oa_ref, ob_ref, oc_ref, od_ref)):
            x1 = x1 + jnp.dot(o_ref[...], wo_ref[gi], preferred_element_type=F32)
        acc_ref[...] = x1
        h_ref[...] = _rms(x1, g_ref[...]).astype(BF16)

    h = h_ref[...]
    gate = jnp.dot(h, wg_ref[...], preferred_element_type=F32)
    up = jnp.dot(h, wu_ref[...], preferred_element_type=F32)
    act = (gate / (1.0 + jnp.exp(-gate))) * up
    acc_ref[...] += jnp.dot(act.astype(BF16), wd_ref[...], preferred_element_type=F32)

    @pl.when(f == pl.num_programs(1) - 1)
    def _():
        y = acc_ref[...]
        out_ref[...] = _rms(y, gf_ref[...]) if final else y


def _post_call(x, o_a, o_b, o_c, o_d, p, g_final, final):
    N, D = x.shape
    ts = TOK_TILE
    d_ff = p['w_gate'].shape[1]
    fc = d_ff // FF_SPLIT
    tok = lambda w: pl.BlockSpec((ts, w), lambda t, f: (t, 0))
    return pl.pallas_call(
        functools.partial(_post_kernel, final=final),
        grid=(N // ts, FF_SPLIT),
        in_specs=[tok(D), tok(GROUP), tok(GROUP), tok(GROUP), tok(GROUP),
                  pl.BlockSpec((4, GROUP, D), lambda t, f: (0, 0, 0)),
                  pl.BlockSpec((1, D), lambda t, f: (0, 0)),
                  pl.BlockSpec((D, fc), lambda t, f: (0, f)),
                  pl.BlockSpec((D, fc), lambda t, f: (0, f)),
                  pl.BlockSpec((fc, D), lambda t, f: (f, 0)),
                  pl.BlockSpec((1, D), lambda t, f: (0, 0))],
        out_specs=tok(D),
        out_shape=jax.ShapeDtypeStruct((N, D), F32),
        scratch_shapes=[pltpu.VMEM((ts, D), F32), pltpu.VMEM((ts, D), BF16)],
        compiler_params=pltpu.CompilerParams(
            dimension_semantics=("parallel", "arbitrary"), vmem_limit_bytes=V7X_VMEM_LIMIT),
        name="post_ffn",
    )(x, o_a, o_b, o_c, o_d, p['w_o'], p['g_ffn'], p['w_gate'], p['w_up'], p['w_down'], g_final)


def _t5_bucket(rel):
    nb = T5_BUCKETS // 2
    max_exact = nb // 2
    side = jnp.where(rel > 0, nb, 0)
    n = jnp.abs(rel)
    large = max_exact + (jnp.log(jnp.maximum(n, 1).astype(F32) / max_exact)
                         / math.log(T5_MAX_DIST / max_exact) * (nb - max_exact)).astype(jnp.int32)
    large = jnp.minimum(large, nb - 1)
    return side + jnp.where(n < max_exact, n, large)


def _lookup(table, idx):
    onehot = idx[..., None, None] == jnp.arange(table.shape[0])[:, None]
    return jnp.sum(jnp.where(onehot, table.astype(F32), 0.0), axis=-2)


def _toeplitz(w, t):
    lead = w.shape[:-1]
    w_pad = jnp.concatenate([w, jnp.zeros(lead + (1,), w.dtype)], axis=-1)
    m = jnp.tile(w_pad, t)[..., :t * (2 * t - 1)].reshape(lead + (t, 2 * t - 1))
    return m[..., t - 1:]


def _toeplitz_tiles(vec_fn, t, n_side):
    reach = n_side * t + t - 1
    w = jnp.moveaxis(vec_fn(jnp.arange(-reach, reach + 1)), -1, 0)
    tiles = [_toeplitz(w[:, (e + n_side) * t:(e + n_side + 2) * t - 1], t) for e in range(-n_side, n_side + 1)]
    return jnp.swapaxes(jnp.stack(tiles), -1, -2)


def _diff_bias(t5_diff):
    def vec(rel):
        return _lookup(t5_diff, _t5_bucket(rel)) * LOG2E
    tiles = _toeplitz_tiles(vec, BAND_T, 4)
    far = vec(jnp.array([-2 * T5_MAX_DIST, 2 * T5_MAX_DIST]))
    return tiles, far


def _dil_bias(t5_dil):
    def vec(rel):
        n = jnp.abs(rel)
        mult = jnp.zeros(rel.shape, F32)
        for window, dil in DIL_PATTERNS:
            mult = mult + ((rel % dil == 0) & (n <= window // 2)).astype(F32)
        b = _lookup(t5_dil, _t5_bucket(rel)) + jnp.log(jnp.maximum(mult, 1.0))[..., None]
        return jnp.where((mult > 0)[..., None], b * LOG2E, NEG)
    return _toeplitz_tiles(vec, BAND_T, DIL_BAND + DIL_Q_TILES - 1)


def _na_bias(rpb):
    d = jnp.arange(NA_ROWS)[:, None]
    ridx = jnp.arange(NA_ROWS)[None, :] - d + NA_ROWS - 1
    qc = jnp.arange(GRID_W)[:, None]
    kc = jnp.arange(GRID_W)[None, :]
    c0 = jnp.clip(qc - NA_COLS // 2, 0, GRID_W - NA_COLS)
    valid = (kc >= c0) & (kc < c0 + NA_COLS)
    rows = rpb.astype(F32)[:, ridx]
    side = GRID_W - NA_COLS
    vals = _toeplitz(jnp.pad(rows, ((0, 0),) * 3 + ((side, side),)), GRID_W)
    vals = jnp.where(valid, vals * LOG2E, NEG)
    vals = vals.transpose(1, 0, 3, 2, 4)
    return vals.reshape(NA_ROWS, N_HEADS, GRID_W, NA_ROWS * GRID_W)


def _rope_tables(S):
    inv_freq = ROPE_THETA ** (-jnp.arange(MLA_HALF, dtype=F32) / MLA_HALF)
    ang = jnp.arange(S).astype(F32)[:, None] * inv_freq[None, :]
    cos, sin = jnp.cos(ang), jnp.sin(ang)
    csn = jnp.concatenate([cos, cos, -sin, sin, jnp.zeros((S, PAIR - 4 * MLA_HALF), F32)], axis=1)
    nt = S // TOK_TILE
    to_t = lambda a: a.reshape(nt, TOK_TILE, MLA_HALF).transpose(0, 2, 1)
    return {'csn': csn, 'cos_t': to_t(cos), 'sin_t': to_t(sin)}


def _layer_weights(w_in, w_uq, w_ukv, w_o, w_gate, w_up, w_down, g_q, g_kv, g_ffn):
    G = GROUP
    col = lambda i: w_in[:, i * G:(i + 1) * G]
    cq, ckv, kr = w_in[:, 9 * G:10 * G], w_in[:, 10 * G:10 * G + 128], w_in[:, 10 * G + 128:]
    kr_swap = jnp.concatenate([kr[:, MLA_HALF:], kr[:, :MLA_HALF]], axis=1)
    kr2 = jnp.concatenate([kr, kr_swap, jnp.zeros((w_in.shape[0], PAIR - 2 * MLA_ROPE), w_in.dtype)], axis=1)
    w_nat = jnp.concatenate([col(0), col(1), col(2), col(4), col(7), cq, ckv, kr2], axis=1)
    w_t = jnp.concatenate([col(3), col(5), col(6), col(8)], axis=1).T
    uq = w_uq.reshape(w_uq.shape[0], N_HEADS, MLA_QK)
    uq = jnp.pad(uq, ((0, 0), (0, 0), (0, PAIR - MLA_QK))).reshape(w_uq.shape[0], N_HEADS * PAIR)
    ukv = w_ukv.reshape(w_ukv.shape[0], N_HEADS, 2 * HEAD_DIM)
    uk = ukv.at[:, :, HEAD_DIM:].set(0.0).reshape(w_ukv.shape[0], N_HEADS * PAIR)
    uv = ukv[:, :, HEAD_DIM:].reshape(w_ukv.shape[0], GROUP)
    place = jnp.zeros((PAIR, N_HEADS * PAIR), F32)
    r = jnp.arange(MLA_ROPE)
    for hd in range(N_HEADS):
        place = place.at[r, hd * PAIR + HEAD_DIM + r].set(1.0)
    b = lambda a: a.astype(BF16)
    row = lambda a: a.astype(F32)[None, :]
    return {
        'w_nat': b(w_nat), 'w_t': b(w_t), 'w_uq_t': b(uq.T), 'w_uk': b(uk), 'w_uv_t': b(uv.T),
        'place': b(place), 'g_q': row(g_q), 'g_kv': row(g_kv), 'g_ffn': row(g_ffn),
        'w_o': b(w_o.reshape(4, GROUP, w_o.shape[1])), 'w_gate': b(w_gate), 'w_up': b(w_up),
        'w_down': b(w_down),
    }


def kernel(x, attn_norm, w_in, na_rpb, diff_lambda, diff_subln, mla_q_norm, w_uq, mla_kv_norm, w_ukv,
           t5_table, w_o, ffn_norm, w_gate, w_up, w_down, final_norm):
    B, S, D = x.shape
    depth = w_in.shape[0]
    rope = _rope_tables(S)
    diff_tiles, diff_far = _diff_bias(t5_table[:, :N_HEADS])
    dil_tiles = _dil_bias(t5_table[:, N_HEADS:])
    g_final = final_norm.astype(F32)[None, :]

    for l in range(depth):
        p = _layer_weights(w_in[l], w_uq[l], w_ukv[l], w_o[l], w_gate[l], w_up[l], w_down[l],
                           mla_q_norm[l], mla_kv_norm[l], ffn_norm[l])
        (qa, ka, va, kb, kc, kd, qb_t, vb_t, qc_t, vc_t, qd_t, vd_t) = _proj_call(
            x, attn_norm[l].astype(F32)[None, :], p, rope)

        o_a = _na_call(qa, ka, va, _na_bias(na_rpb[l]))

        lam_init = 0.8 - 0.6 * math.exp(-0.3 * l)
        lq1, lk1, lq2, lk2 = [diff_lambda[l, j].astype(F32) for j in range(4)]
        lam = jnp.exp(jnp.sum(lq1 * lk1)) - jnp.exp(jnp.sum(lq2 * lk2)) + lam_init
        far = diff_far.reshape(2, 2, 2).transpose(1, 0, 2).reshape(2, 4)
        consts = jnp.concatenate([jnp.broadcast_to(lam, (2, 1)), far, jnp.zeros((2, 3), F32)], axis=1)
        consts = jnp.broadcast_to(consts[:, :, None], (2, 8, 128))
        g_col = jnp.broadcast_to(diff_subln[l].astype(F32)[:, None], (HEAD_DIM, 128))
        o_b = _diff_call(consts, g_col, qb_t, kb, vb_t, diff_tiles, 1.0 - lam_init)

        o_c = _dil_call(qc_t, kc, vc_t, dil_tiles)
        o_d = _mla_call(qd_t, kd, vd_t)

        flat = lambda a: a.reshape(B * S, a.shape[-1])
        x = _post_call(flat(x), flat(o_a), flat(o_b), flat(o_c), flat(o_d), p, g_final,
                       final=(l == depth - 1)).reshape(B, S, D)
    return x
```

```python
import functools
import math

import jax
import jax.numpy as jnp
from jax import lax
from jax.experimental import pallas as pl
from jax.experimental.pallas import tpu as pltpu

F32 = jnp.float32
BF16 = jnp.bfloat16

N_HEADS = 4
HEAD_DIM = 64
GROUP = N_HEADS * HEAD_DIM
PAIR = 2 * HEAD_DIM
GRID_W = 64
NA_ROWS = 8
NA_COLS = 16
DIFF_QK = HEAD_DIM // 2
DIL_PATTERNS = ((128, 1), (512, 4), (2048, 16))
MLA_ROPE = HEAD_DIM // 2
MLA_HALF = MLA_ROPE // 2
MLA_QK = HEAD_DIM + MLA_ROPE
ROPE_THETA = 10000.0
T5_BUCKETS = 32
T5_MAX_DIST = 1024
EPS = 1e-6
LOG2E = 1.4426950408889634
NEG = -1e30
SUM_ROWS = 16
EXP_CHUNK = 32

V7X_VMEM_LIMIT = 56 * 1024 * 1024

TOK_TILE = 512
FULL_T = 512
BAND_T = 256
DIL_BAND = -(-max(w // 2 for w, _ in DIL_PATTERNS) // BAND_T)
DIL_Q_TILES = 2
MLA_Q_BLOCKS = 4
DIFF_Q_BLOCKS = 1
NA_GROUP = 8
NA_ROWS_PER_ITER = 2
FF_SPLIT = 2


def _rms(x, g):
    return x * lax.rsqrt(jnp.mean(x * x, axis=-1, keepdims=True) + EPS) * g


def _nt(a, b):
    return lax.dot_general(a, b, (((1,), (1,)), ((), ())), preferred_element_type=F32)


def _zero_of(x):
    u = lax.bitcast_convert_type(x, jnp.uint32)
    return lax.bitcast_convert_type((u >> 16) >> 16, F32)


def _flash_pipelined(streams, ranges, bufs):
    tk, tq = bufs[0][0].shape
    first = ranges[0][0]
    last = ranges[-1][1] - 1
    ones = jnp.ones((SUM_ROWS, tk), BF16)

    def scores(st, j, bias_fn, s_ref):
        s = jnp.dot(st['load_k'](j), st['qz'], preferred_element_type=F32)
        if bias_fn is not None:
            s = s + bias_fn(j)
        s_ref[...] = s
        return jnp.max(s, axis=0, keepdims=True)

    def accumulate(st, buf, j, par):
        acc_ref, stat_ref = buf[4], buf[5]
        v_aug = jnp.concatenate([st['load_vt'](j), ones], axis=0)
        acc_ref[...] = (stat_ref[_ALPHA + par:_ALPHA + par + 1, :] * acc_ref[...]
                        + jnp.dot(v_aug, buf[2 + par][...], preferred_element_type=F32))

    def softmax_update(buf, par, s_new):
        stat_ref = buf[5]
        m = stat_ref[_MAX:_MAX + 1, :]
        c = stat_ref[_CONST:_CONST + 1, :]
        m_new = jnp.maximum(m, stat_ref[_CMAX:_CMAX + 1, :] + c)
        ref_rows = jnp.broadcast_to(m_new - c, (8, tq))
        for lo_row in range(0, tk, EXP_CHUNK):
            rows = slice(lo_row, lo_row + EXP_CHUNK)
            ref = ref_rows if s_new is None else ref_rows + _zero_of(s_new[lo_row:lo_row + 8, :])
            x = buf[1 - par][rows, :].reshape(EXP_CHUNK // 8, 8, tq) - ref[None]
            buf[3 - par][rows, :] = jnp.exp2(x.reshape(EXP_CHUNK, tq).astype(BF16))
        stat_ref[_ALPHA + 1 - par:_ALPHA + 2 - par, :] = jnp.exp2(m - m_new)
        stat_ref[_MAX:_MAX + 1, :] = m_new

    def step(par, r, t):
        for st, buf in zip(streams, bufs):
            stat_ref = buf[5]
            s_new = jnp.dot(st['load_k'](t), st['qz'], preferred_element_type=F32)
            if st['bias_fns'][r] is not None:
                s_new = s_new + st['bias_fns'][r](t)
            buf[par][...] = s_new
            cmax_t = jnp.max(s_new, axis=0, keepdims=True)
            softmax_update(buf, par, s_new)
            stat_ref[_CMAX:_CMAX + 1, :] = cmax_t
            stat_ref[_CONST:_CONST + 1, :] = jnp.zeros((1, tq), F32) + st['shifts'][r]
            accumulate(st, buf, jnp.maximum(t - 2, first), par)

    def drain(par):
        for st, buf in zip(streams, bufs):
            softmax_update(buf, par, None)
            accumulate(st, buf, jnp.maximum(last - 1, first), par)
            accumulate(st, buf, last, 1 - par)

    for st, buf in zip(streams, bufs):
        acc_ref, stat_ref = buf[4], buf[5]
        cmax = scores(st, first, st.get('first_bias', st['bias_fns'][0]), buf[0])
        buf[3][...] = jnp.zeros_like(buf[3])
        acc_ref[...] = jnp.zeros_like(acc_ref)
        stat_ref[_ALPHA:_ALPHA + 2, :] = jnp.ones((2, tq), F32)
        stat_ref[_MAX:_MAX + 1, :] = jnp.full((1, tq), NEG, F32)
        stat_ref[_CMAX:_CMAX + 1, :] = cmax
        stat_ref[_CONST:_CONST + 1, :] = (jnp.zeros((1, tq), F32)
                                          + (0.0 if 'first_bias' in st else st['shifts'][0]))

    def by_parity(j, fn):
        par = (j - first) % 2
        if isinstance(par, int):
            fn(par)
        else:
            pl.when(par == 0)(functools.partial(fn, 0))
            pl.when(par == 1)(functools.partial(fn, 1))

    for r, (lo, hi) in enumerate(ranges):

        def body(t, c, r=r):
            by_parity(t, lambda par: step(par, r, t))
            return c

        lax.fori_loop(jnp.maximum(lo, first + 1), hi, body, 0)

    by_parity(last + 1, drain)
    outs = []
    for buf in bufs:
        acc = buf[4][...]
        outs.append(acc[:HEAD_DIM] / acc[HEAD_DIM:HEAD_DIM + 1])
    return outs


_ALPHA, _MAX, _CMAX, _CONST = 0, 2, 3, 4
_BUFS_PER_STREAM = 6


def _flash_scratch(n_streams, tk, tq):
    per_stream = [pltpu.VMEM((tk, tq), F32), pltpu.VMEM((tk, tq), F32),
                  pltpu.VMEM((tk, tq), BF16), pltpu.VMEM((tk, tq), BF16),
                  pltpu.VMEM((HEAD_DIM + SUM_ROWS, tq), F32), pltpu.VMEM((8, tq), F32)]
    return per_stream * n_streams


def _flash_bufs(refs):
    n = _BUFS_PER_STREAM
    return [tuple(refs[n * k:n * k + n]) for k in range(len(refs) // n)]


def _store_t_blocks(ref, val, blk):
    for c in range(val.shape[1] // blk):
        ref[c] = val[:, c * blk:(c + 1) * blk].astype(ref.dtype)


def _proj_kernel(x_ref, g_ref, wnat_ref, wt_ref, gq_ref, wuqt_ref, gkv_ref, wuk_ref, wuvt_ref,
                 place_ref, csn_ref, cost_ref, sint_ref,
                 qa_ref, ka_ref, va_ref, kb_ref, kc_ref, kd_ref,
                 qbt_ref, vbt_ref, qct_ref, vct_ref, qdt_ref, vdt_ref):
    h = _rms(x_ref[...], g_ref[...]).astype(BF16)
    nat = jnp.dot(h, wnat_ref[...], preferred_element_type=F32)
    qa_ref[...] = (nat[:, 0:256] * (HEAD_DIM ** -0.5 * LOG2E)).astype(BF16)
    ka_ref[...] = nat[:, 256:512].astype(BF16)
    va_ref[...] = nat[:, 512:768].astype(BF16)
    kb_ref[...] = nat[:, 768:1024].astype(BF16)
    kc_ref[...] = nat[:, 1024:1280].astype(BF16)
    cq = nat[:, 1280:1536]
    ckv = nat[:, 1536:1664]
    kr2 = nat[:, 1664:1792]

    tr = _nt(wt_ref[...], h)
    _store_t_blocks(qbt_ref, tr[0:256] * (DIFF_QK ** -0.5 * LOG2E), FULL_T)
    _store_t_blocks(vbt_ref, tr[256:512], FULL_T)
    _store_t_blocks(qct_ref, tr[512:768] * (HEAD_DIM ** -0.5 * LOG2E), BAND_T)
    _store_t_blocks(vct_ref, tr[768:1024], BAND_T)

    cqn = _rms(cq, gq_ref[...]).astype(BF16)
    qd_t = _nt(wuqt_ref[...], cqn)
    cos_t = cost_ref[...]
    sin_t = sint_ref[...]
    pieces = []
    for hd in range(N_HEADS):
        b0 = hd * PAIR
        x1 = qd_t[b0 + 64:b0 + 80]
        x2 = qd_t[b0 + 80:b0 + 96]
        pieces += [qd_t[b0:b0 + 64], x1 * cos_t - x2 * sin_t, x1 * sin_t + x2 * cos_t,
                   qd_t[b0 + 96:b0 + 128]]
    qd_t = jnp.concatenate(pieces, axis=0) * (MLA_QK ** -0.5 * LOG2E)
    _store_t_blocks(qdt_ref, qd_t, FULL_T)

    ckvn = _rms(ckv, gkv_ref[...]).astype(BF16)
    k_nope = jnp.dot(ckvn, wuk_ref[...], preferred_element_type=F32)
    t = kr2 * csn_ref[...]
    roped = t + pltpu.roll(t, 96, 1)
    kd = k_nope + jnp.dot(roped.astype(BF16), place_ref[...], preferred_element_type=F32)
    kd_ref[...] = kd.astype(BF16)
    _store_t_blocks(vdt_ref, _nt(wuvt_ref[...], ckvn), FULL_T)


def _proj_call(x, g, p, rope):
    B, S, D = x.shape
    ts = TOK_TILE
    nt = S // ts
    nf, nb = ts // FULL_T, ts // BAND_T

    def full(a):
        return pl.BlockSpec(a.shape, lambda b, t: (0,) * a.ndim)

    nat_spec = lambda w: pl.BlockSpec((None, ts, w), lambda b, t: (b, t, 0))
    tb_spec = lambda n, r, blk: pl.BlockSpec((None, n, r, blk), lambda b, t: (b, t, 0, 0))
    nat_shape = lambda w: jax.ShapeDtypeStruct((B, S, w), BF16)
    tb_shape = lambda r, blk: jax.ShapeDtypeStruct((B, S // blk, r, blk), BF16)

    weights = [g, p['w_nat'], p['w_t'], p['g_q'], p['w_uq_t'], p['g_kv'], p['w_uk'], p['w_uv_t'], p['place']]
    in_specs = ([nat_spec(D)] + [full(a) for a in weights] +
                [pl.BlockSpec((ts, PAIR), lambda b, t: (t, 0)),
                 pl.BlockSpec((None, MLA_HALF, ts), lambda b, t: (t, 0, 0)),
                 pl.BlockSpec((None, MLA_HALF, ts), lambda b, t: (t, 0, 0))])
    out_shape = ([nat_shape(GROUP)] * 5 + [nat_shape(2 * GROUP)] +
                 [tb_shape(GROUP, FULL_T)] * 2 + [tb_shape(GROUP, BAND_T)] * 2 +
                 [tb_shape(2 * GROUP, FULL_T), tb_shape(GROUP, FULL_T)])
    out_specs = ([nat_spec(GROUP)] * 5 + [nat_spec(2 * GROUP)] +
                 [tb_spec(nf, GROUP, FULL_T)] * 2 + [tb_spec(nb, GROUP, BAND_T)] * 2 +
                 [tb_spec(nf, 2 * GROUP, FULL_T), tb_spec(nf, GROUP, FULL_T)])
    return pl.pallas_call(
        _proj_kernel,
        grid=(B, nt),
        in_specs=in_specs,
        out_specs=out_specs,
        out_shape=out_shape,
        compiler_params=pltpu.CompilerParams(
            dimension_semantics=("parallel", "parallel"), vmem_limit_bytes=V7X_VMEM_LIMIT),
        name="proj",
    )(x, *weights, rope['csn'], rope['cos_t'], rope['sin_t'])


def _na_kernel(q_ref, kp_ref, kc_ref, kn_ref, vp_ref, vc_ref, vn_ref, bias_ref, o_ref, kk_ref, vv_ref,
               *, n_rows):
    g = pl.program_id(1)
    blk = NA_GROUP * GRID_W
    win = NA_ROWS * GRID_W
    for c, (kr, vr) in enumerate(((kp_ref, vp_ref), (kc_ref, vc_ref), (kn_ref, vn_ref))):
        kk_ref[c * blk:(c + 1) * blk, :] = kr[...]
        vv_ref[c * blk:(c + 1) * blk, :] = vr[...]
    lane = lax.broadcasted_iota(jnp.int32, (GRID_W, PAIR), 1)

    first_head = lane < HEAD_DIM

    def rows_body(jj, carry):
        v_pairs, scores, probs = [], [], []
        for dj in range(NA_ROWS_PER_ITER):
            j = jj * NA_ROWS_PER_ITER + dj
            r = g * NA_GROUP + j
            r0 = jnp.clip(r - NA_ROWS // 2, 0, n_rows - NA_ROWS)
            start = pl.multiple_of((r0 - (g - 1) * NA_GROUP) * GRID_W, GRID_W)
            q_row = q_ref[pl.ds(pl.multiple_of(j * GRID_W, GRID_W), GRID_W), :]
            k_win = kk_ref[pl.ds(start, win), :]
            v_win = vv_ref[pl.ds(start, win), :]
            for pr in range(2):
                q2 = q_row[:, pr * PAIR:(pr + 1) * PAIR]
                zero = jnp.zeros_like(q2)
                qz = jnp.concatenate([jnp.where(first_head, q2, zero), jnp.where(first_head, zero, q2)], axis=0)
                bias = jnp.concatenate([bias_ref[r - r0, 2 * pr], bias_ref[r - r0, 2 * pr + 1]], axis=0)
                scores.append(_nt(qz, k_win[:, pr * PAIR:(pr + 1) * PAIR]) + bias)
                v_pairs.append(v_win[:, pr * PAIR:(pr + 1) * PAIR])
        for s in scores:
            e = jnp.exp2(s - jnp.max(s, axis=-1, keepdims=True))
            probs.append((e.astype(BF16), jnp.sum(e, axis=-1, keepdims=True)))
        outs = []
        for (e, den), v2 in zip(probs, v_pairs):
            o2 = jnp.dot(e, v2, preferred_element_type=F32) / den
            outs.append(jnp.where(first_head, o2[:GRID_W], o2[GRID_W:]))
        for dj in range(NA_ROWS_PER_ITER):
            qs = pl.multiple_of((jj * NA_ROWS_PER_ITER + dj) * GRID_W, GRID_W)
            o_ref[pl.ds(qs, GRID_W), :] = jnp.concatenate(outs[2 * dj:2 * dj + 2], axis=1).astype(o_ref.dtype)
        return carry

    lax.fori_loop(0, NA_GROUP // NA_ROWS_PER_ITER, rows_body, 0)


def _na_call(q, k, v, bias):
    B, S, _ = q.shape
    n_rows = S // GRID_W
    blk = NA_GROUP * GRID_W
    ng = S // blk
    cur = pl.BlockSpec((None, blk, GROUP), lambda b, g: (b, g, 0))
    prv = pl.BlockSpec((None, blk, GROUP), lambda b, g: (b, jnp.maximum(g - 1, 0), 0))
    nxt = pl.BlockSpec((None, blk, GROUP), lambda b, g: (b, jnp.minimum(g + 1, ng - 1), 0))
    return pl.pallas_call(
        functools.partial(_na_kernel, n_rows=n_rows),
        grid=(B, ng),
        in_specs=[cur, prv, cur, nxt, prv, cur, nxt,
                  pl.BlockSpec(bias.shape, lambda b, g: (0, 0, 0, 0))],
        out_specs=cur,
        out_shape=jax.ShapeDtypeStruct((B, S, GROUP), BF16),
        scratch_shapes=[pltpu.VMEM((3 * blk, GROUP), BF16), pltpu.VMEM((3 * blk, GROUP), BF16)],
        compiler_params=pltpu.CompilerParams(
            dimension_semantics=("parallel", "parallel"), vmem_limit_bytes=V7X_VMEM_LIMIT),
        name="na_attn",
    )(q, k, k, k, v, v, v, bias)


def _diff_kernel(c_ref, g_ref, qt_ref, k_ref, vt_ref, bias_ref, o_ref, *scratch, nk, lam_scale):
    i0 = pl.program_id(2) * DIFF_Q_BLOCKS
    tk = vt_ref.shape[2]
    tq = qt_ref.shape[2]
    row = lax.broadcasted_iota(jnp.int32, (PAIR, tq), 0)
    lam = c_ref[0:1, 0:1]
    j_lo = jnp.maximum(i0 - 2, 0)
    j_hi = jnp.minimum(i0 + DIFF_Q_BLOCKS + 2, nk)
    ranges = ((0, j_lo), (j_lo, j_hi), (j_hi, nk))

    def load_k(j):
        return k_ref[pl.ds(pl.multiple_of(j * tk, tk), tk), :]

    streams = []
    for qi in range(DIFF_Q_BLOCKS):
        q_t = qt_ref[qi]
        for hh in range(2):

            def load_vt(j, hh=hh):
                return vt_ref[j, hh * HEAD_DIM:(hh + 1) * HEAD_DIM, :]

            def near_bias(j, hh=hh, qi=qi):
                e0 = 2 * (j - i0 - qi) + 4

                def tile(d):
                    return bias_ref[jnp.clip(e0 + d, 0, 8), hh]
                t0 = tile(0)
                return jnp.concatenate(
                    [jnp.concatenate([t0, tile(-1)], axis=1),
                     jnp.concatenate([tile(1), t0], axis=1)], axis=0)

            for c in range(2):
                r_lo = hh * HEAD_DIM + c * DIFF_QK
                qz = jnp.where((row >= r_lo) & (row < r_lo + DIFF_QK), q_t, jnp.zeros_like(q_t))
                streams.append(dict(qz=qz, load_k=load_k, load_vt=load_vt,
                                    shifts=(c_ref[1 + hh:2 + hh, 0:1], 0.0, c_ref[3 + hh:4 + hh, 0:1]),
                                    bias_fns=(None, near_bias, None), first_bias=near_bias))
    maps = _flash_pipelined(streams, ranges, _flash_bufs(scratch))
    for qi in range(DIFF_Q_BLOCKS):
        heads = []
        for hh in range(2):
            k0 = 4 * qi + 2 * hh
            o = maps[k0] - lam * maps[k0 + 1]
            y = o * lax.rsqrt(jnp.mean(o * o, axis=0, keepdims=True) + EPS) * g_ref[:, 0:1]
            heads.append(y * lam_scale)
        o_ref[qi * tq:(qi + 1) * tq, :] = jnp.concatenate(heads, axis=0).T.astype(o_ref.dtype)


def _resident(block_shape, index_map):
    return pl.BlockSpec(block_shape, index_map, pipeline_mode=pl.Buffered(1))


def _diff_call(consts, g_col, q_t, k, v_t, bias, lam_scale):
    B, nq, _, tq = q_t.shape
    S = k.shape[1]
    nk, _, tk = v_t.shape[1:]
    nqb = DIFF_Q_BLOCKS
    return pl.pallas_call(
        functools.partial(_diff_kernel, nk=nk, lam_scale=lam_scale),
        grid=(B, 2, nq // nqb),
        in_specs=[
            pl.BlockSpec((None, 8, 128), lambda b, p, i: (p, 0, 0)),
            pl.BlockSpec(g_col.shape, lambda b, p, i: (0, 0)),
            pl.BlockSpec((None, nqb, PAIR, tq), lambda b, p, i: (b, i, p, 0)),
            _resident((None, S, PAIR), lambda b, p, i: (b, 0, p)),
            _resident((None, nk, PAIR, tk), lambda b, p, i: (b, 0, p, 0)),
            _resident((9, 2, BAND_T, BAND_T), lambda b, p, i: (0, p, 0, 0)),
        ],
        out_specs=pl.BlockSpec((None, nqb * tq, PAIR), lambda b, p, i: (b, i, p)),
        out_shape=jax.ShapeDtypeStruct((B, S, GROUP), BF16),
        scratch_shapes=_flash_scratch(4 * nqb, tk, tq),
        compiler_params=pltpu.CompilerParams(
            dimension_semantics=("parallel", "parallel", "parallel"), vmem_limit_bytes=V7X_VMEM_LIMIT),
        name="diff_attn",
    )(consts, g_col, q_t, k, v_t, bias)


def _dil_kernel(qt_ref, k_ref, vt_ref, bias_ref, o_ref, *scratch, nk):
    i = pl.program_id(1)
    t = BAND_T
    q_t = jnp.concatenate([qt_ref[a] for a in range(DIL_Q_TILES)], axis=1)
    row = lax.broadcasted_iota(jnp.int32, (PAIR, q_t.shape[1]), 0)
    j0 = i * DIL_Q_TILES
    j_lo = jnp.maximum(j0 - DIL_BAND, 0)
    j_hi = jnp.minimum(j0 + DIL_Q_TILES + DIL_BAND, nk)
    streams = []
    for head in range(N_HEADS):
        pr, hh = divmod(head, 2)

        def load_k(j, pr=pr):
            return k_ref[pl.ds(pl.multiple_of(j * t, t), t), pr * PAIR:(pr + 1) * PAIR]

        def load_vt(j, head=head):
            return vt_ref[j, head * HEAD_DIM:(head + 1) * HEAD_DIM, :]

        def bias(j, head=head):
            e0 = j - j0 + DIL_BAND + DIL_Q_TILES - 1
            return jnp.concatenate([bias_ref[e0 - a, head] for a in range(DIL_Q_TILES)], axis=1)

        q_pair = q_t[pr * PAIR:(pr + 1) * PAIR]
        qz = jnp.where((row // HEAD_DIM) == hh, q_pair, jnp.zeros_like(q_pair))
        streams.append(dict(qz=qz, load_k=load_k, load_vt=load_vt, shifts=(0.0,), bias_fns=(bias,)))
    heads = _flash_pipelined(streams, ((j_lo, j_hi),), _flash_bufs(scratch))
    o_ref[...] = jnp.concatenate(heads, axis=0).T.astype(o_ref.dtype)


def _dil_call(q_t, k, v_t, bias):
    B, nq, _, t = q_t.shape
    S = k.shape[1]
    nk = v_t.shape[1]
    nqt = DIL_Q_TILES
    return pl.pallas_call(
        functools.partial(_dil_kernel, nk=nk),
        grid=(B, nq // nqt),
        in_specs=[
            pl.BlockSpec((None, nqt, GROUP, t), lambda b, i: (b, i, 0, 0)),
            _resident((None, S, GROUP), lambda b, i: (b, 0, 0)),
            _resident((None, nk, GROUP, t), lambda b, i: (b, 0, 0, 0)),
            _resident(bias.shape, lambda b, i: (0, 0, 0, 0)),
        ],
        out_specs=pl.BlockSpec((None, nqt * t, GROUP), lambda b, i: (b, i, 0)),
        out_shape=jax.ShapeDtypeStruct((B, S, GROUP), BF16),
        scratch_shapes=_flash_scratch(N_HEADS, t, nqt * t),
        compiler_params=pltpu.CompilerParams(
            dimension_semantics=("parallel", "parallel"), vmem_limit_bytes=V7X_VMEM_LIMIT),
        name="dil_attn",
    )(q_t, k, v_t, bias)


def _mla_kernel(qt_ref, k_ref, vt_ref, o_ref, *scratch, nk):
    tk = vt_ref.shape[2]
    tq = qt_ref.shape[2]
    streams = []
    for qi in range(MLA_Q_BLOCKS):
        for hh in range(2):

            def load_k(j, hh=hh):
                return k_ref[pl.ds(pl.multiple_of(j * tk, tk), tk), hh * PAIR:(hh + 1) * PAIR]

            def load_vt(j, hh=hh):
                return vt_ref[j, hh * HEAD_DIM:(hh + 1) * HEAD_DIM, :]

            qz = qt_ref[qi, hh * PAIR:(hh + 1) * PAIR, :]
            streams.append(dict(qz=qz, load_k=load_k, load_vt=load_vt, shifts=(0.0,), bias_fns=(None,)))
    outs = _flash_pipelined(streams, ((0, nk),), _flash_bufs(scratch))
    for qi in range(MLA_Q_BLOCKS):
        o_t = jnp.concatenate(outs[2 * qi:2 * qi + 2], axis=0)
        o_ref[qi * tq:(qi + 1) * tq, :] = o_t.T.astype(o_ref.dtype)


def _mla_call(q_t, k, v_t):
    B, nq, _, tq = q_t.shape
    S = k.shape[1]
    nk, _, tk = v_t.shape[1:]
    nqb = MLA_Q_BLOCKS
    return pl.pallas_call(
        functools.partial(_mla_kernel, nk=nk),
        grid=(B, 2, nq // nqb),
        in_specs=[
            pl.BlockSpec((None, nqb, 2 * PAIR, tq), lambda b, p, i: (b, i, p, 0)),
            _resident((None, S, 2 * PAIR), lambda b, p, i: (b, 0, p)),
            _resident((None, nk, PAIR, tk), lambda b, p, i: (b, 0, p, 0)),
        ],
        out_specs=pl.BlockSpec((None, nqb * tq, PAIR), lambda b, p, i: (b, i, p)),
        out_shape=jax.ShapeDtypeStruct((B, S, GROUP), BF16),
        scratch_shapes=_flash_scratch(2 * nqb, tk, tq),
        compiler_params=pltpu.CompilerParams(
            dimension_semantics=("parallel", "parallel", "parallel"), vmem_limit_bytes=V7X_VMEM_LIMIT),
        name="mla_attn",
    )(q_t, k, v_t)


def _post_kernel(x_ref, oa_ref, ob_ref, oc_ref, od_ref, wo_ref, g_ref, wg_ref, wu_ref, wd_ref, gf_ref,
                 out_ref, acc_ref, h_ref, *, final):
    f = pl.program_id(1)

    @pl.when(f == 0)
    def _():
        x1 = x_ref[...]
        for gi, o_ref in enumerate((oa_ref, ob_ref, oc_ref, od_ref)):
            x1 = x1 + jnp.dot(o_ref[...], wo_ref[gi], preferred_element_type=F32)
        acc_ref[...] = x1
        h_ref[...] = _rms(x1, g_ref[...]).astype(BF16)

    h = h_ref[...]
    gate = jnp.dot(h, wg_ref[...], preferred_element_type=F32)
    up = jnp.dot(h, wu_ref[...], preferred_element_type=F32)
    act = (gate / (1.0 + jnp.exp(-gate))) * up
    acc_ref[...] += jnp.dot(act.astype(BF16), wd_ref[...], preferred_element_type=F32)

    @pl.when(f == pl.num_programs(1) - 1)
    def _():
        y = acc_ref[...]
        out_ref[...] = _rms(y, gf_ref[...]) if final else y


def _post_call(x, o_a, o_b, o_c, o_d, p, g_final, final):
    N, D = x.shape
    ts = TOK_TILE
    d_ff = p['w_gate'].shape[1]
    fc = d_ff // FF_SPLIT
    tok = lambda w: pl.BlockSpec((ts, w), lambda t, f: (t, 0))
    return pl.pallas_call(
        functools.partial(_post_kernel, final=final),
        grid=(N // ts, FF_SPLIT),
        in_specs=[tok(D), tok(GROUP), tok(GROUP), tok(GROUP), tok(GROUP),
                  pl.BlockSpec((4, GROUP, D), lambda t, f: (0, 0, 0)),
                  pl.BlockSpec((1, D), lambda t, f: (0, 0)),
                  pl.BlockSpec((D, fc), lambda t, f: (0, f)),
                  pl.BlockSpec((D, fc), lambda t, f: (0, f)),
                  pl.BlockSpec((fc, D), lambda t, f: (f, 0)),
                  pl.BlockSpec((1, D), lambda t, f: (0, 0))],
        out_specs=tok(D),
        out_shape=jax.ShapeDtypeStruct((N, D), F32),
        scratch_shapes=[pltpu.VMEM((ts, D), F32), pltpu.VMEM((ts, D), BF16)],
        compiler_params=pltpu.CompilerParams(
            dimension_semantics=("parallel", "arbitrary"), vmem_limit_bytes=V7X_VMEM_LIMIT),
        name="post_ffn",
    )(x, o_a, o_b, o_c, o_d, p['w_o'], p['g_ffn'], p['w_gate'], p['w_up'], p['w_down'], g_final)


def _t5_bucket(rel):
    nb = T5_BUCKETS // 2
    max_exact = nb // 2
    side = jnp.where(rel > 0, nb, 0)
    n = jnp.abs(rel)
    large = max_exact + (jnp.log(jnp.maximum(n, 1).astype(F32) / max_exact)
                         / math.log(T5_MAX_DIST / max_exact) * (nb - max_exact)).astype(jnp.int32)
    large = jnp.minimum(large, nb - 1)
    return side + jnp.where(n < max_exact, n, large)


def _lookup(table, idx):
    onehot = idx[..., None, None] == jnp.arange(table.shape[0])[:, None]
    return jnp.sum(jnp.where(onehot, table.astype(F32), 0.0), axis=-2)


def _toeplitz(w, t):
    lead = w.shape[:-1]
    w_pad = jnp.concatenate([w, jnp.zeros(lead + (1,), w.dtype)], axis=-1)
    m = jnp.tile(w_pad, t)[..., :t * (2 * t - 1)].reshape(lead + (t, 2 * t - 1))
    return m[..., t - 1:]


def _toeplitz_tiles(vec_fn, t, n_side):
    reach = n_side * t + t - 1
    w = jnp.moveaxis(vec_fn(jnp.arange(-reach, reach + 1)), -1, 0)
    tiles = [_toeplitz(w[:, (e + n_side) * t:(e + n_side + 2) * t - 1], t) for e in range(-n_side, n_side + 1)]
    return jnp.swapaxes(jnp.stack(tiles), -1, -2)


def _diff_bias(t5_diff):
    def vec(rel):
        return _lookup(t5_diff, _t5_bucket(rel)) * LOG2E
    tiles = _toeplitz_tiles(vec, BAND_T, 4)
    far = vec(jnp.array([-2 * T5_MAX_DIST, 2 * T5_MAX_DIST]))
    return tiles, far


def _dil_bias(t5_dil):
    def vec(rel):
        n = jnp.abs(rel)
        mult = jnp.zeros(rel.shape, F32)
        for window, dil in DIL_PATTERNS:
            mult = mult + ((rel % dil == 0) & (n <= window // 2)).astype(F32)
        b = _lookup(t5_dil, _t5_bucket(rel)) + jnp.log(jnp.maximum(mult, 1.0))[..., None]
        return jnp.where((mult > 0)[..., None], b * LOG2E, NEG)
    return _toeplitz_tiles(vec, BAND_T, DIL_BAND + DIL_Q_TILES - 1)


def _na_bias(rpb):
    d = jnp.arange(NA_ROWS)[:, None]
    ridx = jnp.arange(NA_ROWS)[None, :] - d + NA_ROWS - 1
    qc = jnp.arange(GRID_W)[:, None]
    kc = jnp.arange(GRID_W)[None, :]
    c0 = jnp.clip(qc - NA_COLS // 2, 0, GRID_W - NA_COLS)
    valid = (kc >= c0) & (kc < c0 + NA_COLS)
    rows = rpb.astype(F32)[:, ridx]
    side = GRID_W - NA_COLS
    vals = _toeplitz(jnp.pad(rows, ((0, 0),) * 3 + ((side, side),)), GRID_W)
    vals = jnp.where(valid, vals * LOG2E, NEG)
    vals = vals.transpose(1, 0, 3, 2, 4)
    return vals.reshape(NA_ROWS, N_HEADS, GRID_W, NA_ROWS * GRID_W)


def _rope_tables(S):
    inv_freq = ROPE_THETA ** (-jnp.arange(MLA_HALF, dtype=F32) / MLA_HALF)
    ang = jnp.arange(S).astype(F32)[:, None] * inv_freq[None, :]
    cos, sin = jnp.cos(ang), jnp.sin(ang)
    csn = jnp.concatenate([cos, cos, -sin, sin, jnp.zeros((S, PAIR - 4 * MLA_HALF), F32)], axis=1)
    nt = S // TOK_TILE
    to_t = lambda a: a.reshape(nt, TOK_TILE, MLA_HALF).transpose(0, 2, 1)
    return {'csn': csn, 'cos_t': to_t(cos), 'sin_t': to_t(sin)}


def _layer_weights(w_in, w_uq, w_ukv, w_o, w_gate, w_up, w_down, g_q, g_kv, g_ffn):
    G = GROUP
    col = lambda i: w_in[:, i * G:(i + 1) * G]
    cq, ckv, kr = w_in[:, 9 * G:10 * G], w_in[:, 10 * G:10 * G + 128], w_in[:, 10 * G + 128:]
    kr_swap = jnp.concatenate([kr[:, MLA_HALF:], kr[:, :MLA_HALF]], axis=1)
    kr2 = jnp.concatenate([kr, kr_swap, jnp.zeros((w_in.shape[0], PAIR - 2 * MLA_ROPE), w_in.dtype)], axis=1)
    w_nat = jnp.concatenate([col(0), col(1), col(2), col(4), col(7), cq, ckv, kr2], axis=1)
    w_t = jnp.concatenate([col(3), col(5), col(6), col(8)], axis=1).T
    uq = w_uq.reshape(w_uq.shape[0], N_HEADS, MLA_QK)
    uq = jnp.pad(uq, ((0, 0), (0, 0), (0, PAIR - MLA_QK))).reshape(w_uq.shape[0], N_HEADS * PAIR)
    ukv = w_ukv.reshape(w_ukv.shape[0], N_HEADS, 2 * HEAD_DIM)
    uk = ukv.at[:, :, HEAD_DIM:].set(0.0).reshape(w_ukv.shape[0], N_HEADS * PAIR)
    uv = ukv[:, :, HEAD_DIM:].reshape(w_ukv.shape[0], GROUP)
    place = jnp.zeros((PAIR, N_HEADS * PAIR), F32)
    r = jnp.arange(MLA_ROPE)
    for hd in range(N_HEADS):
        place = place.at[r, hd * PAIR + HEAD_DIM + r].set(1.0)
    b = lambda a: a.astype(BF16)
    row = lambda a: a.astype(F32)[None, :]
    return {
        'w_nat': b(w_nat), 'w_t': b(w_t), 'w_uq_t': b(uq.T), 'w_uk': b(uk), 'w_uv_t': b(uv.T),
        'place': b(place), 'g_q': row(g_q), 'g_kv': row(g_kv), 'g_ffn': row(g_ffn),
        'w_o': b(w_o.reshape(4, GROUP, w_o.shape[1])), 'w_gate': b(w_gate), 'w_up': b(w_up),
        'w_down': b(w_down),
    }


def kernel(x, attn_norm, w_in, na_rpb, diff_lambda, diff_subln, mla_q_norm, w_uq, mla_kv_norm, w_ukv,
           t5_table, w_o, ffn_norm, w_gate, w_up, w_down, final_norm):
    B, S, D = x.shape
    depth = w_in.shape[0]
    rope = _rope_tables(S)
    diff_tiles, diff_far = _diff_bias(t5_table[:, :N_HEADS])
    dil_tiles = _dil_bias(t5_table[:, N_HEADS:])
    g_final = final_norm.astype(F32)[None, :]

    for l in range(depth):
        p = _layer_weights(w_in[l], w_uq[l], w_ukv[l], w_o[l], w_gate[l], w_up[l], w_down[l],
                           mla_q_norm[l], mla_kv_norm[l], ffn_norm[l])
        (qa, ka, va, kb, kc, kd, qb_t, vb_t, qc_t, vc_t, qd_t, vd_t) = _proj_call(
            x, attn_norm[l].astype(F32)[None, :], p, rope)

        o_a = _na_call(qa, ka, va, _na_bias(na_rpb[l]))

        lam_init = 0.8 - 0.6 * math.exp(-0.3 * l)
        lq1, lk1, lq2, lk2 = [diff_lambda[l, j].astype(F32) for j in range(4)]
        lam = jnp.exp(jnp.sum(lq1 * lk1)) - jnp.exp(jnp.sum(lq2 * lk2)) + lam_init
        far = diff_far.reshape(2, 2, 2).transpose(1, 0, 2).reshape(2, 4)
        consts = jnp.concatenate([jnp.broadcast_to(lam, (2, 1)), far, jnp.zeros((2, 3), F32)], axis=1)
        consts = jnp.broadcast_to(consts[:, :, None], (2, 8, 128))
        g_col = jnp.broadcast_to(diff_subln[l].astype(F32)[:, None], (HEAD_DIM, 128))
        o_b = _diff_call(consts, g_col, qb_t, kb, vb_t, diff_tiles, 1.0 - lam_init)

        o_c = _dil_call(qc_t, kc, vc_t, dil_tiles)
        o_d = _mla_call(qd_t, kd, vd_t)

        flat = lambda a: a.reshape(B * S, a.shape[-1])
        x = _post_call(flat(x), flat(o_a), flat(o_b), flat(o_c), flat(o_d), p, g_final,
                       final=(l == depth - 1)).reshape(B, S, D)
    return x
```

```python
import functools
import math

import jax
import jax.numpy as jnp
from jax import lax
from jax.experimental import pallas as pl
from jax.experimental.pallas import tpu as pltpu

F32 = jnp.float32
BF16 = jnp.bfloat16

N_HEADS = 4
HEAD_DIM = 64
GROUP = N_HEADS * HEAD_DIM
PAIR = 2 * HEAD_DIM
GRID_W = 64
NA_ROWS = 8
NA_COLS = 16
DIFF_QK = HEAD_DIM // 2
DIL_PATTERNS = ((128, 1), (512, 4), (2048, 16))
MLA_ROPE = HEAD_DIM // 2
MLA_HALF = MLA_ROPE // 2
MLA_QK = HEAD_DIM + MLA_ROPE
ROPE_THETA = 10000.0
T5_BUCKETS = 32
T5_MAX_DIST = 1024
EPS = 1e-6
LOG2E = 1.4426950408889634
NEG = -1e30
SUM_ROWS = 16
EXP_CHUNK = 32
LANE_SKEW = 128

V7X_VMEM_LIMIT = 56 * 1024 * 1024

TOK_TILE = 512
FULL_T = 512
BAND_T = 256
DIL_BAND = -(-max(w // 2 for w, _ in DIL_PATTERNS) // BAND_T)
DIL_Q_TILES = 2
MLA_Q_BLOCKS = 4
DIFF_Q_BLOCKS = 1
NA_GROUP = 8
NA_ROWS_PER_ITER = 2
FF_SPLIT = 2


def _rms(x, g):
    return x * lax.rsqrt(jnp.mean(x * x, axis=-1, keepdims=True) + EPS) * g


def _nt(a, b):
    return lax.dot_general(a, b, (((1,), (1,)), ((), ())), preferred_element_type=F32)


def _zero_of(x):
    u = lax.bitcast_convert_type(x, jnp.uint32)
    return lax.bitcast_convert_type((u >> 16) >> 16, F32)


def _flash_pipelined(streams, ranges, bufs):
    tk, tq = bufs[0][0].shape
    first = ranges[0][0]
    last = ranges[-1][1] - 1
    ones = jnp.ones((SUM_ROWS, tk), BF16)

    def scores(st, j, bias_fn, s_ref):
        s = jnp.dot(st['load_k'](j), st['qz'], preferred_element_type=F32)
        if bias_fn is not None:
            s = s + bias_fn(j)
        s_ref[...] = s
        return jnp.max(s, axis=0, keepdims=True)

    def accumulate(st, buf, j, par):
        acc_ref, stat_ref = buf[4], buf[5]
        v_aug = jnp.concatenate([st['load_vt'](j), ones], axis=0)
        acc_ref[...] = (stat_ref[_ALPHA + par:_ALPHA + par + 1, :] * acc_ref[...]
                        + jnp.dot(v_aug, buf[2 + par][...], preferred_element_type=F32))

    def softmax_update(buf, par, s_new):
        stat_ref = buf[5]
        m = stat_ref[_MAX:_MAX + 1, :]
        c = stat_ref[_CONST:_CONST + 1, :]
        m_new = jnp.maximum(m, stat_ref[_CMAX:_CMAX + 1, :] + c)
        ref_rows = jnp.broadcast_to(m_new - c, (8, tq))
        for lo_row in range(0, tk, EXP_CHUNK):
            rows = slice(lo_row, lo_row + EXP_CHUNK)
            ref = ref_rows if s_new is None else ref_rows + _zero_of(s_new[lo_row:lo_row + 8, :])
            x = buf[1 - par][rows, :].reshape(EXP_CHUNK // 8, 8, tq) - ref[None]
            buf[3 - par][rows, :] = jnp.exp2(x.reshape(EXP_CHUNK, tq).astype(BF16))
        stat_ref[_ALPHA + 1 - par:_ALPHA + 2 - par, :] = jnp.exp2(m - m_new)
        stat_ref[_MAX:_MAX + 1, :] = m_new

    def step(par, r, t):
        for st, buf in zip(streams, bufs):
            stat_ref = buf[5]
            s_new = jnp.dot(st['load_k'](t), st['qz'], preferred_element_type=F32)
            if st['bias_fns'][r] is not None:
                s_new = s_new + st['bias_fns'][r](t)
            buf[par][...] = s_new
            cmax_t = jnp.max(s_new, axis=0, keepdims=True)
            softmax_update(buf, par, s_new)
            stat_ref[_CMAX:_CMAX + 1, :] = cmax_t
            stat_ref[_CONST:_CONST + 1, :] = jnp.zeros((1, tq), F32) + st['shifts'][r]
            accumulate(st, buf, jnp.maximum(t - 2, first), par)

    def drain(par):
        for st, buf in zip(streams, bufs):
            softmax_update(buf, par, None)
            accumulate(st, buf, jnp.maximum(last - 1, first), par)
            accumulate(st, buf, last, 1 - par)

    for st, buf in zip(streams, bufs):
        acc_ref, stat_ref = buf[4], buf[5]
        cmax = scores(st, first, st.get('first_bias', st['bias_fns'][0]), buf[0])
        buf[3][...] = jnp.zeros(buf[3].shape, buf[3].dtype)
        acc_ref[...] = jnp.zeros_like(acc_ref)
        stat_ref[_ALPHA:_ALPHA + 2, :] = jnp.ones((2, tq), F32)
        stat_ref[_MAX:_MAX + 1, :] = jnp.full((1, tq), NEG, F32)
        stat_ref[_CMAX:_CMAX + 1, :] = cmax
        stat_ref[_CONST:_CONST + 1, :] = (jnp.zeros((1, tq), F32)
                                          + (0.0 if 'first_bias' in st else st['shifts'][0]))

    def by_parity(j, fn):
        par = (j - first) % 2
        if isinstance(par, int):
            fn(par)
        else:
            pl.when(par == 0)(functools.partial(fn, 0))
            pl.when(par == 1)(functools.partial(fn, 1))

    for r, (lo, hi) in enumerate(ranges):

        def body(t, c, r=r):
            by_parity(t, lambda par: step(par, r, t))
            return c

        lax.fori_loop(jnp.maximum(lo, first + 1), hi, body, 0)

    by_parity(last + 1, drain)
    outs = []
    for buf in bufs:
        acc = buf[4][...]
        outs.append(acc[:HEAD_DIM] / acc[HEAD_DIM:HEAD_DIM + 1])
    return outs


_ALPHA, _MAX, _CMAX, _CONST = 0, 2, 3, 4
_BUFS_PER_STREAM = 6


def _flash_scratch(n_streams, tk, tq):
    wide = tq + LANE_SKEW
    per_stream = [pltpu.VMEM((tk, wide), F32), pltpu.VMEM((tk, wide), F32),
                  pltpu.VMEM((tk, wide), BF16), pltpu.VMEM((tk, wide), BF16),
                  pltpu.VMEM((HEAD_DIM + SUM_ROWS, tq), F32), pltpu.VMEM((8, tq), F32)]
    return per_stream * n_streams


def _flash_bufs(refs):
    n = _BUFS_PER_STREAM
    out = []
    for k in range(len(refs) // n):
        r = refs[n * k:n * k + n]
        tq = r[4].shape[1]
        tiles = [r[b].at[:, (b % 2) * LANE_SKEW:(b % 2) * LANE_SKEW + tq] for b in range(4)]
        out.append(tuple(tiles) + tuple(r[4:]))
    return out


def _store_t_blocks(ref, val, blk):
    for c in range(val.shape[1] // blk):
        ref[c] = val[:, c * blk:(c + 1) * blk].astype(ref.dtype)


def _proj_kernel(x_ref, g_ref, wnat_ref, wt_ref, gq_ref, wuqt_ref, gkv_ref, wuk_ref, wuvt_ref,
                 place_ref, csn_ref, cost_ref, sint_ref,
                 qa_ref, ka_ref, va_ref, kb_ref, kc_ref, kd_ref,
                 qbt_ref, vbt_ref, qct_ref, vct_ref, qdt_ref, vdt_ref):
    h = _rms(x_ref[...], g_ref[...]).astype(BF16)
    nat = jnp.dot(h, wnat_ref[...], preferred_element_type=F32)
    qa_ref[...] = (nat[:, 0:256] * (HEAD_DIM ** -0.5 * LOG2E)).astype(BF16)
    ka_ref[...] = nat[:, 256:512].astype(BF16)
    va_ref[...] = nat[:, 512:768].astype(BF16)
    kb_ref[...] = nat[:, 768:1024].astype(BF16)
    kc_ref[...] = nat[:, 1024:1280].astype(BF16)
    cq = nat[:, 1280:1536]
    ckv = nat[:, 1536:1664]
    kr2 = nat[:, 1664:1792]

    tr = _nt(wt_ref[...], h)
    _store_t_blocks(qbt_ref, tr[0:256] * (DIFF_QK ** -0.5 * LOG2E), FULL_T)
    _store_t_blocks(vbt_ref, tr[256:512], FULL_T)
    _store_t_blocks(qct_ref, tr[512:768] * (HEAD_DIM ** -0.5 * LOG2E), BAND_T)
    _store_t_blocks(vct_ref, tr[768:1024], BAND_T)

    cqn = _rms(cq, gq_ref[...]).astype(BF16)
    qd_t = _nt(wuqt_ref[...], cqn)
    cos_t = cost_ref[...]
    sin_t = sint_ref[...]
    pieces = []
    for hd in range(N_HEADS):
        b0 = hd * PAIR
        x1 = qd_t[b0 + 64:b0 + 80]
        x2 = qd_t[b0 + 80:b0 + 96]
        pieces += [qd_t[b0:b0 + 64], x1 * cos_t - x2 * sin_t, x1 * sin_t + x2 * cos_t,
                   qd_t[b0 + 96:b0 + 128]]
    qd_t = jnp.concatenate(pieces, axis=0) * (MLA_QK ** -0.5 * LOG2E)
    _store_t_blocks(qdt_ref, qd_t, FULL_T)

    ckvn = _rms(ckv, gkv_ref[...]).astype(BF16)
    k_nope = jnp.dot(ckvn, wuk_ref[...], preferred_element_type=F32)
    t = kr2 * csn_ref[...]
    roped = t + pltpu.roll(t, 96, 1)
    kd = k_nope + jnp.dot(roped.astype(BF16), place_ref[...], preferred_element_type=F32)
    kd_ref[...] = kd.astype(BF16)
    _store_t_blocks(vdt_ref, _nt(wuvt_ref[...], ckvn), FULL_T)


def _proj_call(x, g, p, rope):
    B, S, D = x.shape
    ts = TOK_TILE
    nt = S // ts
    nf, nb = ts // FULL_T, ts // BAND_T

    def full(a):
        return pl.BlockSpec(a.shape, lambda b, t: (0,) * a.ndim)

    nat_spec = lambda w: pl.BlockSpec((None, ts, w), lambda b, t: (b, t, 0))
    tb_spec = lambda n, r, blk: pl.BlockSpec((None, n, r, blk), lambda b, t: (b, t, 0, 0))
    nat_shape = lambda w: jax.ShapeDtypeStruct((B, S, w), BF16)
    tb_shape = lambda r, blk: jax.ShapeDtypeStruct((B, S // blk, r, blk), BF16)

    weights = [g, p['w_nat'], p['w_t'], p['g_q'], p['w_uq_t'], p['g_kv'], p['w_uk'], p['w_uv_t'], p['place']]
    in_specs = ([nat_spec(D)] + [full(a) for a in weights] +
                [pl.BlockSpec((ts, PAIR), lambda b, t: (t, 0)),
                 pl.BlockSpec((None, MLA_HALF, ts), lambda b, t: (t, 0, 0)),
                 pl.BlockSpec((None, MLA_HALF, ts), lambda b, t: (t, 0, 0))])
    out_shape = ([nat_shape(GROUP)] * 5 + [nat_shape(2 * GROUP)] +
                 [tb_shape(GROUP, FULL_T)] * 2 + [tb_shape(GROUP, BAND_T)] * 2 +
                 [tb_shape(2 * GROUP, FULL_T), tb_shape(GROUP, FULL_T)])
    out_specs = ([nat_spec(GROUP)] * 5 + [nat_spec(2 * GROUP)] +
                 [tb_spec(nf, GROUP, FULL_T)] * 2 + [tb_spec(nb, GROUP, BAND_T)] * 2 +
                 [tb_spec(nf, 2 * GROUP, FULL_T), tb_spec(nf, GROUP, FULL_T)])
    return pl.pallas_call(
        _proj_kernel,
        grid=(B, nt),
        in_specs=in_specs,
        out_specs=out_specs,
        out_shape=out_shape,
        compiler_params=pltpu.CompilerParams(
            dimension_semantics=("parallel", "parallel"), vmem_limit_bytes=V7X_VMEM_LIMIT),
        name="proj",
    )(x, *weights, rope['csn'], rope['cos_t'], rope['sin_t'])


def _na_kernel(q_ref, kp_ref, kc_ref, kn_ref, vp_ref, vc_ref, vn_ref, bias_ref, o_ref, kk_ref, vv_ref,
               *, n_rows):
    g = pl.program_id(1)
    blk = NA_GROUP * GRID_W
    win = NA_ROWS * GRID_W
    for c, (kr, vr) in enumerate(((kp_ref, vp_ref), (kc_ref, vc_ref), (kn_ref, vn_ref))):
        kk_ref[c * blk:(c + 1) * blk, :] = kr[...]
        vv_ref[c * blk:(c + 1) * blk, :] = vr[...]
    lane = lax.broadcasted_iota(jnp.int32, (GRID_W, PAIR), 1)

    first_head = lane < HEAD_DIM

    def rows_body(jj, carry):
        v_pairs, scores, probs = [], [], []
        for dj in range(NA_ROWS_PER_ITER):
            j = jj * NA_ROWS_PER_ITER + dj
            r = g * NA_GROUP + j
            r0 = jnp.clip(r - NA_ROWS // 2, 0, n_rows - NA_ROWS)
            start = pl.multiple_of((r0 - (g - 1) * NA_GROUP) * GRID_W, GRID_W)
            q_row = q_ref[pl.ds(pl.multiple_of(j * GRID_W, GRID_W), GRID_W), :]
            k_win = kk_ref[pl.ds(start, win), :]
            v_win = vv_ref[pl.ds(start, win), :]
            for pr in range(2):
                q2 = q_row[:, pr * PAIR:(pr + 1) * PAIR]
                zero = jnp.zeros_like(q2)
                qz = jnp.concatenate([jnp.where(first_head, q2, zero), jnp.where(first_head, zero, q2)], axis=0)
                bias = jnp.concatenate([bias_ref[r - r0, 2 * pr], bias_ref[r - r0, 2 * pr + 1]], axis=0)
                scores.append(_nt(qz, k_win[:, pr * PAIR:(pr + 1) * PAIR]) + bias)
                v_pairs.append(v_win[:, pr * PAIR:(pr + 1) * PAIR])
        for s in scores:
            e = jnp.exp2(s - jnp.max(s, axis=-1, keepdims=True))
            probs.append((e.astype(BF16), jnp.sum(e, axis=-1, keepdims=True)))
        outs = []
        for (e, den), v2 in zip(probs, v_pairs):
            o2 = jnp.dot(e, v2, preferred_element_type=F32) / den
            outs.append(jnp.where(first_head, o2[:GRID_W], o2[GRID_W:]))
        for dj in range(NA_ROWS_PER_ITER):
            qs = pl.multiple_of((jj * NA_ROWS_PER_ITER + dj) * GRID_W, GRID_W)
            o_ref[pl.ds(qs, GRID_W), :] = jnp.concatenate(outs[2 * dj:2 * dj + 2], axis=1).astype(o_ref.dtype)
        return carry

    lax.fori_loop(0, NA_GROUP // NA_ROWS_PER_ITER, rows_body, 0)


def _na_call(q, k, v, bias):
    B, S, _ = q.shape
    n_rows = S // GRID_W
    blk = NA_GROUP * GRID_W
    ng = S // blk
    cur = pl.BlockSpec((None, blk, GROUP), lambda b, g: (b, g, 0))
    prv = pl.BlockSpec((None, blk, GROUP), lambda b, g: (b, jnp.maximum(g - 1, 0), 0))
    nxt = pl.BlockSpec((None, blk, GROUP), lambda b, g: (b, jnp.minimum(g + 1, ng - 1), 0))
    return pl.pallas_call(
        functools.partial(_na_kernel, n_rows=n_rows),
        grid=(B, ng),
        in_specs=[cur, prv, cur, nxt, prv, cur, nxt,
                  pl.BlockSpec(bias.shape, lambda b, g: (0, 0, 0, 0))],
        out_specs=cur,
        out_shape=jax.ShapeDtypeStruct((B, S, GROUP), BF16),
        scratch_shapes=[pltpu.VMEM((3 * blk, GROUP), BF16), pltpu.VMEM((3 * blk, GROUP), BF16)],
        compiler_params=pltpu.CompilerParams(
            dimension_semantics=("parallel", "parallel"), vmem_limit_bytes=V7X_VMEM_LIMIT),
        name="na_attn",
    )(q, k, k, k, v, v, v, bias)


def _diff_kernel(c_ref, g_ref, qt_ref, k_ref, vt_ref, bias_ref, o_ref, *scratch, nk, lam_scale):
    i0 = pl.program_id(2) * DIFF_Q_BLOCKS
    tk = vt_ref.shape[2]
    tq = qt_ref.shape[2]
    row = lax.broadcasted_iota(jnp.int32, (PAIR, tq), 0)
    lam = c_ref[0:1, 0:1]
    j_lo = jnp.maximum(i0 - 2, 0)
    j_hi = jnp.minimum(i0 + DIFF_Q_BLOCKS + 2, nk)
    ranges = ((0, j_lo), (j_lo, j_hi), (j_hi, nk))

    def load_k(j):
        return k_ref[pl.ds(pl.multiple_of(j * tk, tk), tk), :]

    streams = []
    for qi in range(DIFF_Q_BLOCKS):
        q_t = qt_ref[qi]
        for hh in range(2):

            def load_vt(j, hh=hh):
                return vt_ref[j, hh * HEAD_DIM:(hh + 1) * HEAD_DIM, :]

            def near_bias(j, hh=hh, qi=qi):
                e0 = 2 * (j - i0 - qi) + 4

                def tile(d):
                    return bias_ref[jnp.clip(e0 + d, 0, 8), hh]
                t0 = tile(0)
                return jnp.concatenate(
                    [jnp.concatenate([t0, tile(-1)], axis=1),
                     jnp.concatenate([tile(1), t0], axis=1)], axis=0)

            for c in range(2):
                r_lo = hh * HEAD_DIM + c * DIFF_QK
                qz = jnp.where((row >= r_lo) & (row < r_lo + DIFF_QK), q_t, jnp.zeros_like(q_t))
                streams.append(dict(qz=qz, load_k=load_k, load_vt=load_vt,
                                    shifts=(c_ref[1 + hh:2 + hh, 0:1], 0.0, c_ref[3 + hh:4 + hh, 0:1]),
                                    bias_fns=(None, near_bias, None), first_bias=near_bias))
    maps = _flash_pipelined(streams, ranges, _flash_bufs(scratch))
    for qi in range(DIFF_Q_BLOCKS):
        heads = []
        for hh in range(2):
            k0 = 4 * qi + 2 * hh
            o = maps[k0] - lam * maps[k0 + 1]
            y = o * lax.rsqrt(jnp.mean(o * o, axis=0, keepdims=True) + EPS) * g_ref[:, 0:1]
            heads.append(y * lam_scale)
        o_ref[qi * tq:(qi + 1) * tq, :] = jnp.concatenate(heads, axis=0).T.astype(o_ref.dtype)


def _resident(block_shape, index_map):
    return pl.BlockSpec(block_shape, index_map, pipeline_mode=pl.Buffered(1))


def _diff_call(consts, g_col, q_t, k, v_t, bias, lam_scale):
    B, nq, _, tq = q_t.shape
    S = k.shape[1]
    nk, _, tk = v_t.shape[1:]
    nqb = DIFF_Q_BLOCKS
    return pl.pallas_call(
        functools.partial(_diff_kernel, nk=nk, lam_scale=lam_scale),
        grid=(B, 2, nq // nqb),
        in_specs=[
            pl.BlockSpec((None, 8, 128), lambda b, p, i: (p, 0, 0)),
            pl.BlockSpec(g_col.shape, lambda b, p, i: (0, 0)),
            pl.BlockSpec((None, nqb, PAIR, tq), lambda b, p, i: (b, i, p, 0)),
            _resident((None, S, PAIR), lambda b, p, i: (b, 0, p)),
            _resident((None, nk, PAIR, tk), lambda b, p, i: (b, 0, p, 0)),
            _resident((9, 2, BAND_T, BAND_T), lambda b, p, i: (0, p, 0, 0)),
        ],
        out_specs=pl.BlockSpec((None, nqb * tq, PAIR), lambda b, p, i: (b, i, p)),
        out_shape=jax.ShapeDtypeStruct((B, S, GROUP), BF16),
        scratch_shapes=_flash_scratch(4 * nqb, tk, tq),
        compiler_params=pltpu.CompilerParams(
            dimension_semantics=("parallel", "parallel", "parallel"), vmem_limit_bytes=V7X_VMEM_LIMIT),
        name="diff_attn",
    )(consts, g_col, q_t, k, v_t, bias)


def _dil_kernel(qt_ref, k_ref, vt_ref, bias_ref, o_ref, *scratch, nk):
    i = pl.program_id(1)
    t = BAND_T
    q_t = jnp.concatenate([qt_ref[a] for a in range(DIL_Q_TILES)], axis=1)
    row = lax.broadcasted_iota(jnp.int32, (PAIR, q_t.shape[1]), 0)
    j0 = i * DIL_Q_TILES
    j_lo = jnp.maximum(j0 - DIL_BAND, 0)
    j_hi = jnp.minimum(j0 + DIL_Q_TILES + DIL_BAND, nk)
    streams = []
    for head in range(N_HEADS):
        pr, hh = divmod(head, 2)

        def load_k(j, pr=pr):
            return k_ref[pl.ds(pl.multiple_of(j * t, t), t), pr * PAIR:(pr + 1) * PAIR]

        def load_vt(j, head=head):
            return vt_ref[j, head * HEAD_DIM:(head + 1) * HEAD_DIM, :]

        def bias(j, head=head):
            e0 = j - j0 + DIL_BAND + DIL_Q_TILES - 1
            return jnp.concatenate([bias_ref[e0 - a, head] for a in range(DIL_Q_TILES)], axis=1)

        q_pair = q_t[pr * PAIR:(pr + 1) * PAIR]
        qz = jnp.where((row // HEAD_DIM) == hh, q_pair, jnp.zeros_like(q_pair))
        streams.append(dict(qz=qz, load_k=load_k, load_vt=load_vt, shifts=(0.0,), bias_fns=(bias,)))
    heads = _flash_pipelined(streams, ((j_lo, j_hi),), _flash_bufs(scratch))
    o_ref[...] = jnp.concatenate(heads, axis=0).T.astype(o_ref.dtype)


def _dil_call(q_t, k, v_t, bias):
    B, nq, _, t = q_t.shape
    S = k.shape[1]
    nk = v_t.shape[1]
    nqt = DIL_Q_TILES
    return pl.pallas_call(
        functools.partial(_dil_kernel, nk=nk),
        grid=(B, nq // nqt),
        in_specs=[
            pl.BlockSpec((None, nqt, GROUP, t), lambda b, i: (b, i, 0, 0)),
            _resident((None, S, GROUP), lambda b, i: (b, 0, 0)),
            _resident((None, nk, GROUP, t), lambda b, i: (b, 0, 0, 0)),
            _resident(bias.shape, lambda b, i: (0, 0, 0, 0)),
        ],
        out_specs=pl.BlockSpec((None, nqt * t, GROUP), lambda b, i: (b, i, 0)),
        out_shape=jax.ShapeDtypeStruct((B, S, GROUP), BF16),
        scratch_shapes=_flash_scratch(N_HEADS, t, nqt * t),
        compiler_params=pltpu.CompilerParams(
            dimension_semantics=("parallel", "parallel"), vmem_limit_bytes=V7X_VMEM_LIMIT),
        name="dil_attn",
    )(q_t, k, v_t, bias)


def _mla_kernel(qt_ref, k_ref, vt_ref, o_ref, *scratch, nk):
    tk = vt_ref.shape[2]
    tq = qt_ref.shape[2]
    streams = []
    for qi in range(MLA_Q_BLOCKS):
        for hh in range(2):

            def load_k(j, hh=hh):
                return k_ref[pl.ds(pl.multiple_of(j * tk, tk), tk), hh * PAIR:(hh + 1) * PAIR]

            def load_vt(j, hh=hh):
                return vt_ref[j, hh * HEAD_DIM:(hh + 1) * HEAD_DIM, :]

            qz = qt_ref[qi, hh * PAIR:(hh + 1) * PAIR, :]
            streams.append(dict(qz=qz, load_k=load_k, load_vt=load_vt, shifts=(0.0,), bias_fns=(None,)))
    outs = _flash_pipelined(streams, ((0, nk),), _flash_bufs(scratch))
    for qi in range(MLA_Q_BLOCKS):
        o_t = jnp.concatenate(outs[2 * qi:2 * qi + 2], axis=0)
        o_ref[qi * tq:(qi + 1) * tq, :] = o_t.T.astype(o_ref.dtype)


def _mla_call(q_t, k, v_t):
    B, nq, _, tq = q_t.shape
    S = k.shape[1]
    nk, _, tk = v_t.shape[1:]
    nqb = MLA_Q_BLOCKS
    return pl.pallas_call(
        functools.partial(_mla_kernel, nk=nk),
        grid=(B, 2, nq // nqb),
        in_specs=[
            pl.BlockSpec((None, nqb, 2 * PAIR, tq), lambda b, p, i: (b, i, p, 0)),
            _resident((None, S, 2 * PAIR), lambda b, p, i: (b, 0, p)),
            _resident((None, nk, PAIR, tk), lambda b, p, i: (b, 0, p, 0)),
        ],
        out_specs=pl.BlockSpec((None, nqb * tq, PAIR), lambda b, p, i: (b, i, p)),
        out_shape=jax.ShapeDtypeStruct((B, S, GROUP), BF16),
        scratch_shapes=_flash_scratch(2 * nqb, tk, tq),
        compiler_params=pltpu.CompilerParams(
            dimension_semantics=("parallel", "parallel", "parallel"), vmem_limit_bytes=V7X_VMEM_LIMIT),
        name="mla_attn",
    )(q_t, k, v_t)


def _post_kernel(x_ref, oa_ref, ob_ref, oc_ref, od_ref, wo_ref, g_ref, wg_ref, wu_ref, wd_ref, gf_ref,
                 out_ref, acc_ref, h_ref, *, final):
    f = pl.program_id(1)

    @pl.when(f == 0)
    def _():
        x1 = x_ref[...]
        for gi, o_ref in enumerate((oa_ref, ob_ref, oc_ref, od_ref)):
            x1 = x1 + jnp.dot(o_ref[...], wo_ref[gi], preferred_element_type=F32)
        acc_ref[...] = x1
        h_ref[...] = _rms(x1, g_ref[...]).astype(BF16)

    h = h_ref[...]
    gate = jnp.dot(h, wg_ref[...], preferred_element_type=F32)
    up = jnp.dot(h, wu_ref[...], preferred_element_type=F32)
    act = (gate / (1.0 + jnp.exp(-gate))) * up
    acc_ref[...] += jnp.dot(act.astype(BF16), wd_ref[...], preferred_element_type=F32)

    @pl.when(f == pl.num_programs(1) - 1)
    def _():
        y = acc_ref[...]
        out_ref[...] = _rms(y, gf_ref[...]) if final else y


def _post_call(x, o_a, o_b, o_c, o_d, p, g_final, final):
    N, D = x.shape
    ts = TOK_TILE
    d_ff = p['w_gate'].shape[1]
    fc = d_ff // FF_SPLIT
    tok = lambda w: pl.BlockSpec((ts, w), lambda t, f: (t, 0))
    return pl.pallas_call(
        functools.partial(_post_kernel, final=final),
        grid=(N // ts, FF_SPLIT),
        in_specs=[tok(D), tok(GROUP), tok(GROUP), tok(GROUP), tok(GROUP),
                  pl.BlockSpec((4, GROUP, D), lambda t, f: (0, 0, 0)),
                  pl.BlockSpec((1, D), lambda t, f: (0, 0)),
                  pl.BlockSpec((D, fc), lambda t, f: (0, f)),
                  pl.BlockSpec((D, fc), lambda t, f: (0, f)),
                  pl.BlockSpec((fc, D), lambda t, f: (f, 0)),
                  pl.BlockSpec((1, D), lambda t, f: (0, 0))],
        out_specs=tok(D),
        out_shape=jax.ShapeDtypeStruct((N, D), F32),
        scratch_shapes=[pltpu.VMEM((ts, D), F32), pltpu.VMEM((ts, D), BF16)],
        compiler_params=pltpu.CompilerParams(
            dimension_semantics=("parallel", "arbitrary"), vmem_limit_bytes=V7X_VMEM_LIMIT),
        name="post_ffn",
    )(x, o_a, o_b, o_c, o_d, p['w_o'], p['g_ffn'], p['w_gate'], p['w_up'], p['w_down'], g_final)


def _t5_bucket(rel):
    nb = T5_BUCKETS // 2
    max_exact = nb // 2
    side = jnp.where(rel > 0, nb, 0)
    n = jnp.abs(rel)
    large = max_exact + (jnp.log(jnp.maximum(n, 1).astype(F32) / max_exact)
                         / math.log(T5_MAX_DIST / max_exact) * (nb - max_exact)).astype(jnp.int32)
    large = jnp.minimum(large, nb - 1)
    return side + jnp.where(n < max_exact, n, large)


def _lookup(table, idx):
    onehot = idx[..., None, None] == jnp.arange(table.shape[0])[:, None]
    return jnp.sum(jnp.where(onehot, table.astype(F32), 0.0), axis=-2)


def _toeplitz(w, t):
    lead = w.shape[:-1]
    w_pad = jnp.concatenate([w, jnp.zeros(lead + (1,), w.dtype)], axis=-1)
    m = jnp.tile(w_pad, t)[..., :t * (2 * t - 1)].reshape(lead + (t, 2 * t - 1))
    return m[..., t - 1:]


def _toeplitz_tiles(vec_fn, t, n_side):
    reach = n_side * t + t - 1
    w = jnp.moveaxis(vec_fn(jnp.arange(-reach, reach + 1)), -1, 0)
    tiles = [_toeplitz(w[:, (e + n_side) * t:(e + n_side + 2) * t - 1], t) for e in range(-n_side, n_side + 1)]
    return jnp.swapaxes(jnp.stack(tiles), -1, -2)


def _diff_bias(t5_diff):
    def vec(rel):
        return _lookup(t5_diff, _t5_bucket(rel)) * LOG2E
    tiles = _toeplitz_tiles(vec, BAND_T, 4)
    far = vec(jnp.array([-2 * T5_MAX_DIST, 2 * T5_MAX_DIST]))
    return tiles, far


def _dil_bias(t5_dil):
    def vec(rel):
        n = jnp.abs(rel)
        mult = jnp.zeros(rel.shape, F32)
        for window, dil in DIL_PATTERNS:
            mult = mult + ((rel % dil == 0) & (n <= window // 2)).astype(F32)
        b = _lookup(t5_dil, _t5_bucket(rel)) + jnp.log(jnp.maximum(mult, 1.0))[..., None]
        return jnp.where((mult > 0)[..., None], b * LOG2E, NEG)
    return _toeplitz_tiles(vec, BAND_T, DIL_BAND + DIL_Q_TILES - 1)


def _na_bias(rpb):
    d = jnp.arange(NA_ROWS)[:, None]
    ridx = jnp.arange(NA_ROWS)[None, :] - d + NA_ROWS - 1
    qc = jnp.arange(GRID_W)[:, None]
    kc = jnp.arange(GRID_W)[None, :]
    c0 = jnp.clip(qc - NA_COLS // 2, 0, GRID_W - NA_COLS)
    valid = (kc >= c0) & (kc < c0 + NA_COLS)
    rows = rpb.astype(F32)[:, ridx]
    side = GRID_W - NA_COLS
    vals = _toeplitz(jnp.pad(rows, ((0, 0),) * 3 + ((side, side),)), GRID_W)
    vals = jnp.where(valid, vals * LOG2E, NEG)
    vals = vals.transpose(1, 0, 3, 2, 4)
    return vals.reshape(NA_ROWS, N_HEADS, GRID_W, NA_ROWS * GRID_W)


def _rope_tables(S):
    inv_freq = ROPE_THETA ** (-jnp.arange(MLA_HALF, dtype=F32) / MLA_HALF)
    ang = jnp.arange(S).astype(F32)[:, None] * inv_freq[None, :]
    cos, sin = jnp.cos(ang), jnp.sin(ang)
    csn = jnp.concatenate([cos, cos, -sin, sin, jnp.zeros((S, PAIR - 4 * MLA_HALF), F32)], axis=1)
    nt = S // TOK_TILE
    to_t = lambda a: a.reshape(nt, TOK_TILE, MLA_HALF).transpose(0, 2, 1)
    return {'csn': csn, 'cos_t': to_t(cos), 'sin_t': to_t(sin)}


def _layer_weights(w_in, w_uq, w_ukv, w_o, w_gate, w_up, w_down, g_q, g_kv, g_ffn):
    G = GROUP
    col = lambda i: w_in[:, i * G:(i + 1) * G]
    cq, ckv, kr = w_in[:, 9 * G:10 * G], w_in[:, 10 * G:10 * G + 128], w_in[:, 10 * G + 128:]
    kr_swap = jnp.concatenate([kr[:, MLA_HALF:], kr[:, :MLA_HALF]], axis=1)
    kr2 = jnp.concatenate([kr, kr_swap, jnp.zeros((w_in.shape[0], PAIR - 2 * MLA_ROPE), w_in.dtype)], axis=1)
    w_nat = jnp.concatenate([col(0), col(1), col(2), col(4), col(7), cq, ckv, kr2], axis=1)
    w_t = jnp.concatenate([col(3), col(5), col(6), col(8)], axis=1).T
    uq = w_uq.reshape(w_uq.shape[0], N_HEADS, MLA_QK)
    uq = jnp.pad(uq, ((0, 0), (0, 0), (0, PAIR - MLA_QK))).reshape(w_uq.shape[0], N_HEADS * PAIR)
    ukv = w_ukv.reshape(w_ukv.shape[0], N_HEADS, 2 * HEAD_DIM)
    uk = ukv.at[:, :, HEAD_DIM:].set(0.0).reshape(w_ukv.shape[0], N_HEADS * PAIR)
    uv = ukv[:, :, HEAD_DIM:].reshape(w_ukv.shape[0], GROUP)
    place = jnp.zeros((PAIR, N_HEADS * PAIR), F32)
    r = jnp.arange(MLA_ROPE)
    for hd in range(N_HEADS):
        place = place.at[r, hd * PAIR + HEAD_DIM + r].set(1.0)
    b = lambda a: a.astype(BF16)
    row = lambda a: a.astype(F32)[None, :]
    return {
        'w_nat': b(w_nat), 'w_t': b(w_t), 'w_uq_t': b(uq.T), 'w_uk': b(uk), 'w_uv_t': b(uv.T),
        'place': b(place), 'g_q': row(g_q), 'g_kv': row(g_kv), 'g_ffn': row(g_ffn),
        'w_o': b(w_o.reshape(4, GROUP, w_o.shape[1])), 'w_gate': b(w_gate), 'w_up': b(w_up),
        'w_down': b(w_down),
    }


def kernel(x, attn_norm, w_in, na_rpb, diff_lambda, diff_subln, mla_q_norm, w_uq, mla_kv_norm, w_ukv,
           t5_table, w_o, ffn_norm, w_gate, w_up, w_down, final_norm):
    B, S, D = x.shape
    depth = w_in.shape[0]
    rope = _rope_tables(S)
    diff_tiles, diff_far = _diff_bias(t5_table[:, :N_HEADS])
    dil_tiles = _dil_bias(t5_table[:, N_HEADS:])
    g_final = final_norm.astype(F32)[None, :]

    for l in range(depth):
        p = _layer_weights(w_in[l], w_uq[l], w_ukv[l], w_o[l], w_gate[l], w_up[l], w_down[l],
                           mla_q_norm[l], mla_kv_norm[l], ffn_norm[l])
        (qa, ka, va, kb, kc, kd, qb_t, vb_t, qc_t, vc_t, qd_t, vd_t) = _proj_call(
            x, attn_norm[l].astype(F32)[None, :], p, rope)

        o_a = _na_call(qa, ka, va, _na_bias(na_rpb[l]))

        lam_init = 0.8 - 0.6 * math.exp(-0.3 * l)
        lq1, lk1, lq2, lk2 = [diff_lambda[l, j].astype(F32) for j in range(4)]
        lam = jnp.exp(jnp.sum(lq1 * lk1)) - jnp.exp(jnp.sum(lq2 * lk2)) + lam_init
        far = diff_far.reshape(2, 2, 2).transpose(1, 0, 2).reshape(2, 4)
        consts = jnp.concatenate([jnp.broadcast_to(lam, (2, 1)), far, jnp.zeros((2, 3), F32)], axis=1)
        consts = jnp.broadcast_to(consts[:, :, None], (2, 8, 128))
        g_col = jnp.broadcast_to(diff_subln[l].astype(F32)[:, None], (HEAD_DIM, 128))
        o_b = _diff_call(consts, g_col, qb_t, kb, vb_t, diff_tiles, 1.0 - lam_init)

        o_c = _dil_call(qc_t, kc, vc_t, dil_tiles)
        o_d = _mla_call(qd_t, kd, vd_t)

        flat = lambda a: a.reshape(B * S, a.shape[-1])
        x = _post_call(flat(x), flat(o_a), flat(o_b), flat(o_c), flat(o_d), p, g_final,
                       final=(l == depth - 1)).reshape(B, S, D)
    return x
```

```python
import functools
import math

import jax
import jax.numpy as jnp
from jax import lax
from jax.experimental import pallas as pl
from jax.experimental.pallas import tpu as pltpu

F32 = jnp.float32
BF16 = jnp.bfloat16

N_HEADS = 4
HEAD_DIM = 64
GROUP = N_HEADS * HEAD_DIM
PAIR = 2 * HEAD_DIM
GRID_W = 64
NA_ROWS = 8
NA_COLS = 16
DIFF_QK = HEAD_DIM // 2
DIL_PATTERNS = ((128, 1), (512, 4), (2048, 16))
MLA_ROPE = HEAD_DIM // 2
MLA_HALF = MLA_ROPE // 2
MLA_QK = HEAD_DIM + MLA_ROPE
ROPE_THETA = 10000.0
T5_BUCKETS = 32
T5_MAX_DIST = 1024
EPS = 1e-6
LOG2E = 1.4426950408889634
NEG = -1e30
SUM_ROWS = 16
EXP_CHUNK = 32
LANE_SKEW = 128

V7X_VMEM_LIMIT = 56 * 1024 * 1024

TOK_TILE = 512
FULL_T = 512
BAND_T = 256
DIL_BAND = -(-max(w // 2 for w, _ in DIL_PATTERNS) // BAND_T)
DIL_Q_TILES = 2
MLA_Q_BLOCKS = 4
DIFF_Q_BLOCKS = 1
NA_GROUP = 8
NA_ROWS_PER_ITER = 2
FF_SPLIT = 2


def _rms(x, g):
    return x * lax.rsqrt(jnp.mean(x * x, axis=-1, keepdims=True) + EPS) * g


def _nt(a, b):
    return lax.dot_general(a, b, (((1,), (1,)), ((), ())), preferred_element_type=F32)


def _zero_of(x):
    u = lax.bitcast_convert_type(x, jnp.uint32)
    return lax.bitcast_convert_type((u >> 16) >> 16, F32)


def _flash_pipelined(streams, ranges, bufs):
    tk, tq = bufs[0][0].shape
    first = ranges[0][0]
    last = ranges[-1][1] - 1
    ones = jnp.ones((SUM_ROWS, tk), BF16)

    def scores(st, j, bias_fn, s_ref):
        s = jnp.dot(st['load_k'](j), st['qz'], preferred_element_type=F32)
        if bias_fn is not None:
            s = s + bias_fn(j)
        s_ref[...] = s
        return jnp.max(s, axis=0, keepdims=True)

    def accumulate(st, buf, j, par):
        acc_ref, stat_ref = buf[4], buf[5]
        v_aug = jnp.concatenate([st['load_vt'](j), ones], axis=0)
        acc_ref[...] = (stat_ref[_ALPHA + par:_ALPHA + par + 1, :] * acc_ref[...]
                        + jnp.dot(v_aug, buf[2 + par][...], preferred_element_type=F32))

    def softmax_update(buf, par, s_new):
        stat_ref = buf[5]
        m = stat_ref[_MAX:_MAX + 1, :]
        c = stat_ref[_CONST:_CONST + 1, :]
        m_new = jnp.maximum(m, stat_ref[_CMAX:_CMAX + 1, :] + c)
        ref_rows = jnp.broadcast_to(m_new - c, (8, tq))
        for lo_row in range(0, tk, EXP_CHUNK):
            rows = slice(lo_row, lo_row + EXP_CHUNK)
            ref = ref_rows if s_new is None else ref_rows + _zero_of(s_new[lo_row:lo_row + 8, :])
            x = buf[1 - par][rows, :].reshape(EXP_CHUNK // 8, 8, tq) - ref[None]
            buf[3 - par][rows, :] = jnp.exp2(x.reshape(EXP_CHUNK, tq)).astype(BF16)
        stat_ref[_ALPHA + 1 - par:_ALPHA + 2 - par, :] = jnp.exp2(m - m_new)
        stat_ref[_MAX:_MAX + 1, :] = m_new

    def step(par, r, t):
        for st, buf in zip(streams, bufs):
            stat_ref = buf[5]
            s_new = jnp.dot(st['load_k'](t), st['qz'], preferred_element_type=F32)
            if st['bias_fns'][r] is not None:
                s_new = s_new + st['bias_fns'][r](t)
            buf[par][...] = s_new
            cmax_t = jnp.max(s_new, axis=0, keepdims=True)
            softmax_update(buf, par, s_new)
            stat_ref[_CMAX:_CMAX + 1, :] = cmax_t
            stat_ref[_CONST:_CONST + 1, :] = jnp.zeros((1, tq), F32) + st['shifts'][r]
            accumulate(st, buf, jnp.maximum(t - 2, first), par)

    def drain(par):
        for st, buf in zip(streams, bufs):
            softmax_update(buf, par, None)
            accumulate(st, buf, jnp.maximum(last - 1, first), par)
            accumulate(st, buf, last, 1 - par)

    for st, buf in zip(streams, bufs):
        acc_ref, stat_ref = buf[4], buf[5]
        cmax = scores(st, first, st.get('first_bias', st['bias_fns'][0]), buf[0])
        buf[3][...] = jnp.zeros(buf[3].shape, buf[3].dtype)
        acc_ref[...] = jnp.zeros_like(acc_ref)
        stat_ref[_ALPHA:_ALPHA + 2, :] = jnp.ones((2, tq), F32)
        stat_ref[_MAX:_MAX + 1, :] = jnp.full((1, tq), NEG, F32)
        stat_ref[_CMAX:_CMAX + 1, :] = cmax
        stat_ref[_CONST:_CONST + 1, :] = (jnp.zeros((1, tq), F32)
                                          + (0.0 if 'first_bias' in st else st['shifts'][0]))

    def by_parity(j, fn):
        par = (j - first) % 2
        if isinstance(par, int):
            fn(par)
        else:
            pl.when(par == 0)(functools.partial(fn, 0))
            pl.when(par == 1)(functools.partial(fn, 1))

    for r, (lo, hi) in enumerate(ranges):

        def body(t, c, r=r):
            by_parity(t, lambda par: step(par, r, t))
            return c

        lax.fori_loop(jnp.maximum(lo, first + 1), hi, body, 0)

    by_parity(last + 1, drain)
    outs = []
    for buf in bufs:
        acc = buf[4][...]
        outs.append(acc[:HEAD_DIM] / acc[HEAD_DIM:HEAD_DIM + 1])
    return outs


_ALPHA, _MAX, _CMAX, _CONST = 0, 2, 3, 4
_BUFS_PER_STREAM = 6


def _flash_scratch(n_streams, tk, tq):
    wide = tq + LANE_SKEW
    per_stream = [pltpu.VMEM((tk, wide), F32), pltpu.VMEM((tk, wide), F32),
                  pltpu.VMEM((tk, wide), BF16), pltpu.VMEM((tk, wide), BF16),
                  pltpu.VMEM((HEAD_DIM + SUM_ROWS, tq), F32), pltpu.VMEM((8, tq), F32)]
    return per_stream * n_streams


def _flash_bufs(refs):
    n = _BUFS_PER_STREAM
    out = []
    for k in range(len(refs) // n):
        r = refs[n * k:n * k + n]
        tq = r[4].shape[1]
        tiles = [r[b].at[:, (b % 2) * LANE_SKEW:(b % 2) * LANE_SKEW + tq] for b in range(4)]
        out.append(tuple(tiles) + tuple(r[4:]))
    return out


def _store_t_blocks(ref, val, blk):
    for c in range(val.shape[1] // blk):
        ref[c] = val[:, c * blk:(c + 1) * blk].astype(ref.dtype)


def _proj_kernel(x_ref, g_ref, wnat_ref, wt_ref, gq_ref, wuqt_ref, gkv_ref, wuk_ref, wuvt_ref,
                 place_ref, csn_ref, cost_ref, sint_ref,
                 qa_ref, ka_ref, va_ref, kb_ref, kc_ref, kd_ref,
                 qbt_ref, vbt_ref, qct_ref, vct_ref, qdt_ref, vdt_ref):
    h = _rms(x_ref[...], g_ref[...]).astype(BF16)
    nat = jnp.dot(h, wnat_ref[...], preferred_element_type=F32)
    qa_ref[...] = (nat[:, 0:256] * (HEAD_DIM ** -0.5 * LOG2E)).astype(BF16)
    ka_ref[...] = nat[:, 256:512].astype(BF16)
    va_ref[...] = nat[:, 512:768].astype(BF16)
    kb_ref[...] = nat[:, 768:1024].astype(BF16)
    kc_ref[...] = nat[:, 1024:1280].astype(BF16)
    cq = nat[:, 1280:1536]
    ckv = nat[:, 1536:1664]
    kr2 = nat[:, 1664:1792]

    tr = _nt(wt_ref[...], h)
    _store_t_blocks(qbt_ref, tr[0:256] * (DIFF_QK ** -0.5 * LOG2E), FULL_T)
    _store_t_blocks(vbt_ref, tr[256:512], FULL_T)
    _store_t_blocks(qct_ref, tr[512:768] * (HEAD_DIM ** -0.5 * LOG2E), BAND_T)
    _store_t_blocks(vct_ref, tr[768:1024], BAND_T)

    cqn = _rms(cq, gq_ref[...]).astype(BF16)
    qd_t = _nt(wuqt_ref[...], cqn)
    cos_t = cost_ref[...]
    sin_t = sint_ref[...]
    pieces = []
    for hd in range(N_HEADS):
        b0 = hd * PAIR
        x1 = qd_t[b0 + 64:b0 + 80]
        x2 = qd_t[b0 + 80:b0 + 96]
        pieces += [qd_t[b0:b0 + 64], x1 * cos_t - x2 * sin_t, x1 * sin_t + x2 * cos_t,
                   qd_t[b0 + 96:b0 + 128]]
    qd_t = jnp.concatenate(pieces, axis=0) * (MLA_QK ** -0.5 * LOG2E)
    _store_t_blocks(qdt_ref, qd_t, FULL_T)

    ckvn = _rms(ckv, gkv_ref[...]).astype(BF16)
    k_nope = jnp.dot(ckvn, wuk_ref[...], preferred_element_type=F32)
    t = kr2 * csn_ref[...]
    roped = t + pltpu.roll(t, 96, 1)
    kd = k_nope + jnp.dot(roped.astype(BF16), place_ref[...], preferred_element_type=F32)
    kd_ref[...] = kd.astype(BF16)
    _store_t_blocks(vdt_ref, _nt(wuvt_ref[...], ckvn), FULL_T)


def _proj_call(x, g, p, rope):
    B, S, D = x.shape
    ts = TOK_TILE
    nt = S // ts
    nf, nb = ts // FULL_T, ts // BAND_T

    def full(a):
        return pl.BlockSpec(a.shape, lambda b, t: (0,) * a.ndim)

    nat_spec = lambda w: pl.BlockSpec((None, ts, w), lambda b, t: (b, t, 0))
    tb_spec = lambda n, r, blk: pl.BlockSpec((None, n, r, blk), lambda b, t: (b, t, 0, 0))
    nat_shape = lambda w: jax.ShapeDtypeStruct((B, S, w), BF16)
    tb_shape = lambda r, blk: jax.ShapeDtypeStruct((B, S // blk, r, blk), BF16)

    weights = [g, p['w_nat'], p['w_t'], p['g_q'], p['w_uq_t'], p['g_kv'], p['w_uk'], p['w_uv_t'], p['place']]
    in_specs = ([nat_spec(D)] + [full(a) for a in weights] +
                [pl.BlockSpec((ts, PAIR), lambda b, t: (t, 0)),
                 pl.BlockSpec((None, MLA_HALF, ts), lambda b, t: (t, 0, 0)),
                 pl.BlockSpec((None, MLA_HALF, ts), lambda b, t: (t, 0, 0))])
    out_shape = ([nat_shape(GROUP)] * 5 + [nat_shape(2 * GROUP)] +
                 [tb_shape(GROUP, FULL_T)] * 2 + [tb_shape(GROUP, BAND_T)] * 2 +
                 [tb_shape(2 * GROUP, FULL_T), tb_shape(GROUP, FULL_T)])
    out_specs = ([nat_spec(GROUP)] * 5 + [nat_spec(2 * GROUP)] +
                 [tb_spec(nf, GROUP, FULL_T)] * 2 + [tb_spec(nb, GROUP, BAND_T)] * 2 +
                 [tb_spec(nf, 2 * GROUP, FULL_T), tb_spec(nf, GROUP, FULL_T)])
    return pl.pallas_call(
        _proj_kernel,
        grid=(B, nt),
        in_specs=in_specs,
        out_specs=out_specs,
        out_shape=out_shape,
        compiler_params=pltpu.CompilerParams(
            dimension_semantics=("parallel", "parallel"), vmem_limit_bytes=V7X_VMEM_LIMIT),
        name="proj",
    )(x, *weights, rope['csn'], rope['cos_t'], rope['sin_t'])


def _na_kernel(q_ref, kp_ref, kc_ref, kn_ref, vp_ref, vc_ref, vn_ref, bias_ref, o_ref, kk_ref, vv_ref,
               *, n_rows):
    g = pl.program_id(1)
    blk = NA_GROUP * GRID_W
    win = NA_ROWS * GRID_W
    for c, (kr, vr) in enumerate(((kp_ref, vp_ref), (kc_ref, vc_ref), (kn_ref, vn_ref))):
        kk_ref[c * blk:(c + 1) * blk, :] = kr[...]
        vv_ref[c * blk:(c + 1) * blk, :] = vr[...]
    lane = lax.broadcasted_iota(jnp.int32, (GRID_W, PAIR), 1)

    first_head = lane < HEAD_DIM

    def rows_body(jj, carry):
        v_pairs, scores, probs = [], [], []
        for dj in range(NA_ROWS_PER_ITER):
            j = jj * NA_ROWS_PER_ITER + dj
            r = g * NA_GROUP + j
            r0 = jnp.clip(r - NA_ROWS // 2, 0, n_rows - NA_ROWS)
            start = pl.multiple_of((r0 - (g - 1) * NA_GROUP) * GRID_W, GRID_W)
            q_row = q_ref[pl.ds(pl.multiple_of(j * GRID_W, GRID_W), GRID_W), :]
            k_win = kk_ref[pl.ds(start, win), :]
            v_win = vv_ref[pl.ds(start, win), :]
            for pr in range(2):
                q2 = q_row[:, pr * PAIR:(pr + 1) * PAIR]
                zero = jnp.zeros_like(q2)
                qz = jnp.concatenate([jnp.where(first_head, q2, zero), jnp.where(first_head, zero, q2)], axis=0)
                bias = jnp.concatenate([bias_ref[r - r0, 2 * pr], bias_ref[r - r0, 2 * pr + 1]], axis=0)
                scores.append(_nt(qz, k_win[:, pr * PAIR:(pr + 1) * PAIR]) + bias)
                v_pairs.append(v_win[:, pr * PAIR:(pr + 1) * PAIR])
        for s in scores:
            e = jnp.exp2(s - jnp.max(s, axis=-1, keepdims=True))
            probs.append((e.astype(BF16), jnp.sum(e, axis=-1, keepdims=True)))
        outs = []
        for (e, den), v2 in zip(probs, v_pairs):
            o2 = jnp.dot(e, v2, preferred_element_type=F32) / den
            outs.append(jnp.where(first_head, o2[:GRID_W], o2[GRID_W:]))
        for dj in range(NA_ROWS_PER_ITER):
            qs = pl.multiple_of((jj * NA_ROWS_PER_ITER + dj) * GRID_W, GRID_W)
            o_ref[pl.ds(qs, GRID_W), :] = jnp.concatenate(outs[2 * dj:2 * dj + 2], axis=1).astype(o_ref.dtype)
        return carry

    lax.fori_loop(0, NA_GROUP // NA_ROWS_PER_ITER, rows_body, 0)


def _na_call(q, k, v, bias):
    B, S, _ = q.shape
    n_rows = S // GRID_W
    blk = NA_GROUP * GRID_W
    ng = S // blk
    cur = pl.BlockSpec((None, blk, GROUP), lambda b, g: (b, g, 0))
    prv = pl.BlockSpec((None, blk, GROUP), lambda b, g: (b, jnp.maximum(g - 1, 0), 0))
    nxt = pl.BlockSpec((None, blk, GROUP), lambda b, g: (b, jnp.minimum(g + 1, ng - 1), 0))
    return pl.pallas_call(
        functools.partial(_na_kernel, n_rows=n_rows),
        grid=(B, ng),
        in_specs=[cur, prv, cur, nxt, prv, cur, nxt,
                  pl.BlockSpec(bias.shape, lambda b, g: (0, 0, 0, 0))],
        out_specs=cur,
        out_shape=jax.ShapeDtypeStruct((B, S, GROUP), BF16),
        scratch_shapes=[pltpu.VMEM((3 * blk, GROUP), BF16), pltpu.VMEM((3 * blk, GROUP), BF16)],
        compiler_params=pltpu.CompilerParams(
            dimension_semantics=("parallel", "parallel"), vmem_limit_bytes=V7X_VMEM_LIMIT),
        name="na_attn",
    )(q, k, k, k, v, v, v, bias)


def _diff_kernel(c_ref, g_ref, qt_ref, k_ref, vt_ref, bias_ref, o_ref, *scratch, nk, lam_scale):
    i0 = pl.program_id(2) * DIFF_Q_BLOCKS
    tk = vt_ref.shape[2]
    tq = qt_ref.shape[2]
    row = lax.broadcasted_iota(jnp.int32, (PAIR, tq), 0)
    lam = c_ref[0:1, 0:1]
    j_lo = jnp.maximum(i0 - 2, 0)
    j_hi = jnp.minimum(i0 + DIFF_Q_BLOCKS + 2, nk)
    ranges = ((0, j_lo), (j_lo, j_hi), (j_hi, nk))

    def load_k(j):
        return k_ref[pl.ds(pl.multiple_of(j * tk, tk), tk), :]

    streams = []
    for qi in range(DIFF_Q_BLOCKS):
        q_t = qt_ref[qi]
        for hh in range(2):

            def load_vt(j, hh=hh):
                return vt_ref[j, hh * HEAD_DIM:(hh + 1) * HEAD_DIM, :]

            def near_bias(j, hh=hh, qi=qi):
                e0 = 2 * (j - i0 - qi) + 4

                def tile(d):
                    return bias_ref[jnp.clip(e0 + d, 0, 8), hh]
                t0 = tile(0)
                return jnp.concatenate(
                    [jnp.concatenate([t0, tile(-1)], axis=1),
                     jnp.concatenate([tile(1), t0], axis=1)], axis=0)

            for c in range(2):
                r_lo = hh * HEAD_DIM + c * DIFF_QK
                qz = jnp.where((row >= r_lo) & (row < r_lo + DIFF_QK), q_t, jnp.zeros_like(q_t))
                streams.append(dict(qz=qz, load_k=load_k, load_vt=load_vt,
                                    shifts=(c_ref[1 + hh:2 + hh, 0:1], 0.0, c_ref[3 + hh:4 + hh, 0:1]),
                                    bias_fns=(None, near_bias, None), first_bias=near_bias))
    maps = _flash_pipelined(streams, ranges, _flash_bufs(scratch))
    for qi in range(DIFF_Q_BLOCKS):
        heads = []
        for hh in range(2):
            k0 = 4 * qi + 2 * hh
            o = maps[k0] - lam * maps[k0 + 1]
            y = o * lax.rsqrt(jnp.mean(o * o, axis=0, keepdims=True) + EPS) * g_ref[:, 0:1]
            heads.append(y * lam_scale)
        o_ref[qi * tq:(qi + 1) * tq, :] = jnp.concatenate(heads, axis=0).T.astype(o_ref.dtype)


def _resident(block_shape, index_map):
    return pl.BlockSpec(block_shape, index_map, pipeline_mode=pl.Buffered(1))


def _diff_call(consts, g_col, q_t, k, v_t, bias, lam_scale):
    B, nq, _, tq = q_t.shape
    S = k.shape[1]
    nk, _, tk = v_t.shape[1:]
    nqb = DIFF_Q_BLOCKS
    return pl.pallas_call(
        functools.partial(_diff_kernel, nk=nk, lam_scale=lam_scale),
        grid=(B, 2, nq // nqb),
        in_specs=[
            pl.BlockSpec((None, 8, 128), lambda b, p, i: (p, 0, 0)),
            pl.BlockSpec(g_col.shape, lambda b, p, i: (0, 0)),
            pl.BlockSpec((None, nqb, PAIR, tq), lambda b, p, i: (b, i, p, 0)),
            _resident((None, S, PAIR), lambda b, p, i: (b, 0, p)),
            _resident((None, nk, PAIR, tk), lambda b, p, i: (b, 0, p, 0)),
            _resident((9, 2, BAND_T, BAND_T), lambda b, p, i: (0, p, 0, 0)),
        ],
        out_specs=pl.BlockSpec((None, nqb * tq, PAIR), lambda b, p, i: (b, i, p)),
        out_shape=jax.ShapeDtypeStruct((B, S, GROUP), BF16),
        scratch_shapes=_flash_scratch(4 * nqb, tk, tq),
        compiler_params=pltpu.CompilerParams(
            dimension_semantics=("parallel", "parallel", "parallel"), vmem_limit_bytes=V7X_VMEM_LIMIT),
        name="diff_attn",
    )(consts, g_col, q_t, k, v_t, bias)


def _dil_kernel(qt_ref, k_ref, vt_ref, bias_ref, o_ref, *scratch, nk):
    i = pl.program_id(1)
    t = BAND_T
    q_t = jnp.concatenate([qt_ref[a] for a in range(DIL_Q_TILES)], axis=1)
    row = lax.broadcasted_iota(jnp.int32, (PAIR, q_t.shape[1]), 0)
    j0 = i * DIL_Q_TILES
    j_lo = jnp.maximum(j0 - DIL_BAND, 0)
    j_hi = jnp.minimum(j0 + DIL_Q_TILES + DIL_BAND, nk)
    streams = []
    for head in range(N_HEADS):
        pr, hh = divmod(head, 2)

        def load_k(j, pr=pr):
            return k_ref[pl.ds(pl.multiple_of(j * t, t), t), pr * PAIR:(pr + 1) * PAIR]

        def load_vt(j, head=head):
            return vt_ref[j, head * HEAD_DIM:(head + 1) * HEAD_DIM, :]

        def bias(j, head=head):
            e0 = j - j0 + DIL_BAND + DIL_Q_TILES - 1
            return jnp.concatenate([bias_ref[e0 - a, head] for a in range(DIL_Q_TILES)], axis=1)

        q_pair = q_t[pr * PAIR:(pr + 1) * PAIR]
        qz = jnp.where((row // HEAD_DIM) == hh, q_pair, jnp.zeros_like(q_pair))
        streams.append(dict(qz=qz, load_k=load_k, load_vt=load_vt, shifts=(0.0,), bias_fns=(bias,)))
    heads = _flash_pipelined(streams, ((j_lo, j_hi),), _flash_bufs(scratch))
    o_ref[...] = jnp.concatenate(heads, axis=0).T.astype(o_ref.dtype)


def _dil_call(q_t, k, v_t, bias):
    B, nq, _, t = q_t.shape
    S = k.shape[1]
    nk = v_t.shape[1]
    nqt = DIL_Q_TILES
    return pl.pallas_call(
        functools.partial(_dil_kernel, nk=nk),
        grid=(B, nq // nqt),
        in_specs=[
            pl.BlockSpec((None, nqt, GROUP, t), lambda b, i: (b, i, 0, 0)),
            _resident((None, S, GROUP), lambda b, i: (b, 0, 0)),
            _resident((None, nk, GROUP, t), lambda b, i: (b, 0, 0, 0)),
            _resident(bias.shape, lambda b, i: (0, 0, 0, 0)),
        ],
        out_specs=pl.BlockSpec((None, nqt * t, GROUP), lambda b, i: (b, i, 0)),
        out_shape=jax.ShapeDtypeStruct((B, S, GROUP), BF16),
        scratch_shapes=_flash_scratch(N_HEADS, t, nqt * t),
        compiler_params=pltpu.CompilerParams(
            dimension_semantics=("parallel", "parallel"), vmem_limit_bytes=V7X_VMEM_LIMIT),
        name="dil_attn",
    )(q_t, k, v_t, bias)


def _mla_kernel(qt_ref, k_ref, vt_ref, o_ref, *scratch, nk):
    tk = vt_ref.shape[2]
    tq = qt_ref.shape[2]
    streams = []
    for qi in range(MLA_Q_BLOCKS):
        for hh in range(2):

            def load_k(j, hh=hh):
                return k_ref[pl.ds(pl.multiple_of(j * tk, tk), tk), hh * PAIR:(hh + 1) * PAIR]

            def load_vt(j, hh=hh):
                return vt_ref[j, hh * HEAD_DIM:(hh + 1) * HEAD_DIM, :]

            qz = qt_ref[qi, hh * PAIR:(hh + 1) * PAIR, :]
            streams.append(dict(qz=qz, load_k=load_k, load_vt=load_vt, shifts=(0.0,), bias_fns=(None,)))
    outs = _flash_pipelined(streams, ((0, nk),), _flash_bufs(scratch))
    for qi in range(MLA_Q_BLOCKS):
        o_t = jnp.concatenate(outs[2 * qi:2 * qi + 2], axis=0)
        o_ref[qi * tq:(qi + 1) * tq, :] = o_t.T.astype(o_ref.dtype)


def _mla_call(q_t, k, v_t):
    B, nq, _, tq = q_t.shape
    S = k.shape[1]
    nk, _, tk = v_t.shape[1:]
    nqb = MLA_Q_BLOCKS
    return pl.pallas_call(
        functools.partial(_mla_kernel, nk=nk),
        grid=(B, 2, nq // nqb),
        in_specs=[
            pl.BlockSpec((None, nqb, 2 * PAIR, tq), lambda b, p, i: (b, i, p, 0)),
            _resident((None, S, 2 * PAIR), lambda b, p, i: (b, 0, p)),
            _resident((None, nk, PAIR, tk), lambda b, p, i: (b, 0, p, 0)),
        ],
        out_specs=pl.BlockSpec((None, nqb * tq, PAIR), lambda b, p, i: (b, i, p)),
        out_shape=jax.ShapeDtypeStruct((B, S, GROUP), BF16),
        scratch_shapes=_flash_scratch(2 * nqb, tk, tq),
        compiler_params=pltpu.CompilerParams(
            dimension_semantics=("parallel", "parallel", "parallel"), vmem_limit_bytes=V7X_VMEM_LIMIT),
        name="mla_attn",
    )(q_t, k, v_t)


def _post_kernel(x_ref, oa_ref, ob_ref, oc_ref, od_ref, wo_ref, g_ref, wg_ref, wu_ref, wd_ref, gf_ref,
                 out_ref, acc_ref, h_ref, *, final):
    f = pl.program_id(1)

    @pl.when(f == 0)
    def _():
        x1 = x_ref[...]
        for gi, o_ref in enumerate((oa_ref, ob_ref, oc_ref, od_ref)):
            x1 = x1 + jnp.dot(o_ref[...], wo_ref[gi], preferred_element_type=F32)
        acc_ref[...] = x1
        h_ref[...] = _rms(x1, g_ref[...]).astype(BF16)

    h = h_ref[...]
    gate = jnp.dot(h, wg_ref[...], preferred_element_type=F32)
    up = jnp.dot(h, wu_ref[...], preferred_element_type=F32)
    act = (gate / (1.0 + jnp.exp(-gate))) * up
    acc_ref[...] += jnp.dot(act.astype(BF16), wd_ref[...], preferred_element_type=F32)

    @pl.when(f == pl.num_programs(1) - 1)
    def _():
        y = acc_ref[...]
        out_ref[...] = _rms(y, gf_ref[...]) if final else y


def _post_call(x, o_a, o_b, o_c, o_d, p, g_final, final):
    N, D = x.shape
    ts = TOK_TILE
    d_ff = p['w_gate'].shape[1]
    fc = d_ff // FF_SPLIT
    tok = lambda w: pl.BlockSpec((ts, w), lambda t, f: (t, 0))
    return pl.pallas_call(
        functools.partial(_post_kernel, final=final),
        grid=(N // ts, FF_SPLIT),
        in_specs=[tok(D), tok(GROUP), tok(GROUP), tok(GROUP), tok(GROUP),
                  pl.BlockSpec((4, GROUP, D), lambda t, f: (0, 0, 0)),
                  pl.BlockSpec((1, D), lambda t, f: (0, 0)),
                  pl.BlockSpec((D, fc), lambda t, f: (0, f)),
                  pl.BlockSpec((D, fc), lambda t, f: (0, f)),
                  pl.BlockSpec((fc, D), lambda t, f: (f, 0)),
                  pl.BlockSpec((1, D), lambda t, f: (0, 0))],
        out_specs=tok(D),
        out_shape=jax.ShapeDtypeStruct((N, D), F32),
        scratch_shapes=[pltpu.VMEM((ts, D), F32), pltpu.VMEM((ts, D), BF16)],
        compiler_params=pltpu.CompilerParams(
            dimension_semantics=("parallel", "arbitrary"), vmem_limit_bytes=V7X_VMEM_LIMIT),
        name="post_ffn",
    )(x, o_a, o_b, o_c, o_d, p['w_o'], p['g_ffn'], p['w_gate'], p['w_up'], p['w_down'], g_final)


def _t5_bucket(rel):
    nb = T5_BUCKETS // 2
    max_exact = nb // 2
    side = jnp.where(rel > 0, nb, 0)
    n = jnp.abs(rel)
    large = max_exact + (jnp.log(jnp.maximum(n, 1).astype(F32) / max_exact)
                         / math.log(T5_MAX_DIST / max_exact) * (nb - max_exact)).astype(jnp.int32)
    large = jnp.minimum(large, nb - 1)
    return side + jnp.where(n < max_exact, n, large)


def _lookup(table, idx):
    onehot = idx[..., None, None] == jnp.arange(table.shape[0])[:, None]
    return jnp.sum(jnp.where(onehot, table.astype(F32), 0.0), axis=-2)


def _toeplitz(w, t):
    lead = w.shape[:-1]
    w_pad = jnp.concatenate([w, jnp.zeros(lead + (1,), w.dtype)], axis=-1)
    m = jnp.tile(w_pad, t)[..., :t * (2 * t - 1)].reshape(lead + (t, 2 * t - 1))
    return m[..., t - 1:]


def _toeplitz_tiles(vec_fn, t, n_side):
    reach = n_side * t + t - 1
    w = jnp.moveaxis(vec_fn(jnp.arange(-reach, reach + 1)), -1, 0)
    tiles = [_toeplitz(w[:, (e + n_side) * t:(e + n_side + 2) * t - 1], t) for e in range(-n_side, n_side + 1)]
    return jnp.swapaxes(jnp.stack(tiles), -1, -2)


def _diff_bias(t5_diff):
    def vec(rel):
        return _lookup(t5_diff, _t5_bucket(rel)) * LOG2E
    tiles = _toeplitz_tiles(vec, BAND_T, 4)
    far = vec(jnp.array([-2 * T5_MAX_DIST, 2 * T5_MAX_DIST]))
    return tiles, far


def _dil_bias(t5_dil):
    def vec(rel):
        n = jnp.abs(rel)
        mult = jnp.zeros(rel.shape, F32)
        for window, dil in DIL_PATTERNS:
            mult = mult + ((rel % dil == 0) & (n <= window // 2)).astype(F32)
        b = _lookup(t5_dil, _t5_bucket(rel)) + jnp.log(jnp.maximum(mult, 1.0))[..., None]
        return jnp.where((mult > 0)[..., None], b * LOG2E, NEG)
    return _toeplitz_tiles(vec, BAND_T, DIL_BAND + DIL_Q_TILES - 1)


def _na_bias(rpb):
    d = jnp.arange(NA_ROWS)[:, None]
    ridx = jnp.arange(NA_ROWS)[None, :] - d + NA_ROWS - 1
    qc = jnp.arange(GRID_W)[:, None]
    kc = jnp.arange(GRID_W)[None, :]
    c0 = jnp.clip(qc - NA_COLS // 2, 0, GRID_W - NA_COLS)
    valid = (kc >= c0) & (kc < c0 + NA_COLS)
    rows = rpb.astype(F32)[:, ridx]
    side = GRID_W - NA_COLS
    vals = _toeplitz(jnp.pad(rows, ((0, 0),) * 3 + ((side, side),)), GRID_W)
    vals = jnp.where(valid, vals * LOG2E, NEG)
    vals = vals.transpose(1, 0, 3, 2, 4)
    return vals.reshape(NA_ROWS, N_HEADS, GRID_W, NA_ROWS * GRID_W)


def _rope_tables(S):
    inv_freq = ROPE_THETA ** (-jnp.arange(MLA_HALF, dtype=F32) / MLA_HALF)
    ang = jnp.arange(S).astype(F32)[:, None] * inv_freq[None, :]
    cos, sin = jnp.cos(ang), jnp.sin(ang)
    csn = jnp.concatenate([cos, cos, -sin, sin, jnp.zeros((S, PAIR - 4 * MLA_HALF), F32)], axis=1)
    nt = S // TOK_TILE
    to_t = lambda a: a.reshape(nt, TOK_TILE, MLA_HALF).transpose(0, 2, 1)
    return {'csn': csn, 'cos_t': to_t(cos), 'sin_t': to_t(sin)}


def _layer_weights(w_in, w_uq, w_ukv, w_o, w_gate, w_up, w_down, g_q, g_kv, g_ffn):
    G = GROUP
    col = lambda i: w_in[:, i * G:(i + 1) * G]
    cq, ckv, kr = w_in[:, 9 * G:10 * G], w_in[:, 10 * G:10 * G + 128], w_in[:, 10 * G + 128:]
    kr_swap = jnp.concatenate([kr[:, MLA_HALF:], kr[:, :MLA_HALF]], axis=1)
    kr2 = jnp.concatenate([kr, kr_swap, jnp.zeros((w_in.shape[0], PAIR - 2 * MLA_ROPE), w_in.dtype)], axis=1)
    w_nat = jnp.concatenate([col(0), col(1), col(2), col(4), col(7), cq, ckv, kr2], axis=1)
    w_t = jnp.concatenate([col(3), col(5), col(6), col(8)], axis=1).T
    uq = w_uq.reshape(w_uq.shape[0], N_HEADS, MLA_QK)
    uq = jnp.pad(uq, ((0, 0), (0, 0), (0, PAIR - MLA_QK))).reshape(w_uq.shape[0], N_HEADS * PAIR)
    ukv = w_ukv.reshape(w_ukv.shape[0], N_HEADS, 2 * HEAD_DIM)
    uk = ukv.at[:, :, HEAD_DIM:].set(0.0).reshape(w_ukv.shape[0], N_HEADS * PAIR)
    uv = ukv[:, :, HEAD_DIM:].reshape(w_ukv.shape[0], GROUP)
    place = jnp.zeros((PAIR, N_HEADS * PAIR), F32)
    r = jnp.arange(MLA_ROPE)
    for hd in range(N_HEADS):
        place = place.at[r, hd * PAIR + HEAD_DIM + r].set(1.0)
    b = lambda a: a.astype(BF16)
    row = lambda a: a.astype(F32)[None, :]
    return {
        'w_nat': b(w_nat), 'w_t': b(w_t), 'w_uq_t': b(uq.T), 'w_uk': b(uk), 'w_uv_t': b(uv.T),
        'place': b(place), 'g_q': row(g_q), 'g_kv': row(g_kv), 'g_ffn': row(g_ffn),
        'w_o': b(w_o.reshape(4, GROUP, w_o.shape[1])), 'w_gate': b(w_gate), 'w_up': b(w_up),
        'w_down': b(w_down),
    }


def kernel(x, attn_norm, w_in, na_rpb, diff_lambda, diff_subln, mla_q_norm, w_uq, mla_kv_norm, w_ukv,
           t5_table, w_o, ffn_norm, w_gate, w_up, w_down, final_norm):
    B, S, D = x.shape
    depth = w_in.shape[0]
    rope = _rope_tables(S)
    diff_tiles, diff_far = _diff_bias(t5_table[:, :N_HEADS])
    dil_tiles = _dil_bias(t5_table[:, N_HEADS:])
    g_final = final_norm.astype(F32)[None, :]

    for l in range(depth):
        p = _layer_weights(w_in[l], w_uq[l], w_ukv[l], w_o[l], w_gate[l], w_up[l], w_down[l],
                           mla_q_norm[l], mla_kv_norm[l], ffn_norm[l])
        (qa, ka, va, kb, kc, kd, qb_t, vb_t, qc_t, vc_t, qd_t, vd_t) = _proj_call(
            x, attn_norm[l].astype(F32)[None, :], p, rope)

        o_a = _na_call(qa, ka, va, _na_bias(na_rpb[l]))

        lam_init = 0.8 - 0.6 * math.exp(-0.3 * l)
        lq1, lk1, lq2, lk2 = [diff_lambda[l, j].astype(F32) for j in range(4)]
        lam = jnp.exp(jnp.sum(lq1 * lk1)) - jnp.exp(jnp.sum(lq2 * lk2)) + lam_init
        far = diff_far.reshape(2, 2, 2).transpose(1, 0, 2).reshape(2, 4)
        consts = jnp.concatenate([jnp.broadcast_to(lam, (2, 1)), far, jnp.zeros((2, 3), F32)], axis=1)
        consts = jnp.broadcast_to(consts[:, :, None], (2, 8, 128))
        g_col = jnp.broadcast_to(diff_subln[l].astype(F32)[:, None], (HEAD_DIM, 128))
        o_b = _diff_call(consts, g_col, qb_t, kb, vb_t, diff_tiles, 1.0 - lam_init)

        o_c = _dil_call(qc_t, kc, vc_t, dil_tiles)
        o_d = _mla_call(qd_t, kd, vd_t)

        flat = lambda a: a.reshape(B * S, a.shape[-1])
        x = _post_call(flat(x), flat(o_a), flat(o_b), flat(o_c), flat(o_d), p, g_final,
                       final=(l == depth - 1)).reshape(B, S, D)
    return x
```

```python
import functools
import math

import jax
import jax.numpy as jnp
from jax import lax
from jax.experimental import pallas as pl
from jax.experimental.pallas import tpu as pltpu

F32 = jnp.float32
BF16 = jnp.bfloat16

N_HEADS = 4
HEAD_DIM = 64
GROUP = N_HEADS * HEAD_DIM
PAIR = 2 * HEAD_DIM
GRID_W = 64
NA_ROWS = 8
NA_COLS = 16
DIFF_QK = HEAD_DIM // 2
DIL_PATTERNS = ((128, 1), (512, 4), (2048, 16))
MLA_ROPE = HEAD_DIM // 2
MLA_HALF = MLA_ROPE // 2
MLA_QK = HEAD_DIM + MLA_ROPE
ROPE_THETA = 10000.0
T5_BUCKETS = 32
T5_MAX_DIST = 1024
EPS = 1e-6
LOG2E = 1.4426950408889634
NEG = -1e30
SUM_ROWS = 16
EXP_CHUNK = 32

V7X_VMEM_LIMIT = 56 * 1024 * 1024

TOK_TILE = 512
FULL_T = 512
BAND_T = 256
DIL_BAND = -(-max(w // 2 for w, _ in DIL_PATTERNS) // BAND_T)
DIL_Q_TILES = 2
MLA_Q_BLOCKS = 4
DIFF_Q_BLOCKS = 1
NA_GROUP = 8
NA_ROWS_PER_ITER = 2
FF_SPLIT = 2


def _rms(x, g):
    return x * lax.rsqrt(jnp.mean(x * x, axis=-1, keepdims=True) + EPS) * g


def _nt(a, b):
    return lax.dot_general(a, b, (((1,), (1,)), ((), ())), preferred_element_type=F32)


def _zero_of(x):
    u = lax.bitcast_convert_type(x, jnp.uint32)
    return lax.bitcast_convert_type((u >> 16) >> 16, F32)


def _flash_pipelined(streams, ranges, bufs):
    tk, tq = bufs[0][0].shape
    first = ranges[0][0]
    last = ranges[-1][1] - 1
    ones = jnp.ones((SUM_ROWS, tk), BF16)

    def scores(st, j, bias_fn, s_ref):
        s = jnp.dot(st['load_k'](j), st['qz'], preferred_element_type=F32)
        if bias_fn is not None:
            s = s + bias_fn(j)
        s_ref[...] = s
        return jnp.max(s, axis=0, keepdims=True)

    def accumulate(st, buf, j, par):
        acc_ref, stat_ref = buf[4], buf[5]
        v_aug = jnp.concatenate([st['load_vt'](j), ones], axis=0)
        acc_ref[...] = (stat_ref[_ALPHA + par:_ALPHA + par + 1, :] * acc_ref[...]
                        + jnp.dot(v_aug, buf[2 + par][...], preferred_element_type=F32))

    def softmax_update(buf, par, s_new):
        stat_ref = buf[5]
        m = stat_ref[_MAX:_MAX + 1, :]
        c = stat_ref[_CONST:_CONST + 1, :]
        m_new = jnp.maximum(m, stat_ref[_CMAX:_CMAX + 1, :] + c)
        ref_rows = jnp.broadcast_to(m_new - c, (8, tq))
        for lo_row in range(0, tk, EXP_CHUNK):
            rows = slice(lo_row, lo_row + EXP_CHUNK)
            ref = ref_rows if s_new is None else ref_rows + _zero_of(s_new[lo_row:lo_row + 8, :])
            x = buf[1 - par][rows, :].reshape(EXP_CHUNK // 8, 8, tq) - ref[None]
            buf[3 - par][rows, :] = jnp.exp2(x.reshape(EXP_CHUNK, tq)).astype(BF16)
        stat_ref[_ALPHA + 1 - par:_ALPHA + 2 - par, :] = jnp.exp2(m - m_new)
        stat_ref[_MAX:_MAX + 1, :] = m_new

    def step(par, r, t):
        for st, buf in zip(streams, bufs):
            stat_ref = buf[5]
            s_new = jnp.dot(st['load_k'](t), st['qz'], preferred_element_type=F32)
            if st['bias_fns'][r] is not None:
                s_new = s_new + st['bias_fns'][r](t)
            buf[par][...] = s_new
            cmax_t = jnp.max(s_new, axis=0, keepdims=True)
            softmax_update(buf, par, s_new)
            stat_ref[_CMAX:_CMAX + 1, :] = cmax_t
            stat_ref[_CONST:_CONST + 1, :] = jnp.zeros((1, tq), F32) + st['shifts'][r]
            accumulate(st, buf, jnp.maximum(t - 2, first), par)

    def drain(par):
        for st, buf in zip(streams, bufs):
            softmax_update(buf, par, None)
            accumulate(st, buf, jnp.maximum(last - 1, first), par)
            accumulate(st, buf, last, 1 - par)

    for st, buf in zip(streams, bufs):
        acc_ref, stat_ref = buf[4], buf[5]
        cmax = scores(st, first, st.get('first_bias', st['bias_fns'][0]), buf[0])
        buf[3][...] = jnp.zeros(buf[3].shape, buf[3].dtype)
        acc_ref[...] = jnp.zeros_like(acc_ref)
        stat_ref[_ALPHA:_ALPHA + 2, :] = jnp.ones((2, tq), F32)
        stat_ref[_MAX:_MAX + 1, :] = jnp.full((1, tq), NEG, F32)
        stat_ref[_CMAX:_CMAX + 1, :] = cmax
        stat_ref[_CONST:_CONST + 1, :] = (jnp.zeros((1, tq), F32)
                                          + (0.0 if 'first_bias' in st else st['shifts'][0]))

    def by_parity(j, fn):
        par = (j - first) % 2
        if isinstance(par, int):
            fn(par)
        else:
            pl.when(par == 0)(functools.partial(fn, 0))
            pl.when(par == 1)(functools.partial(fn, 1))

    for r, (lo, hi) in enumerate(ranges):

        def body(t, c, r=r):
            by_parity(t, lambda par: step(par, r, t))
            return c

        lax.fori_loop(jnp.maximum(lo, first + 1), hi, body, 0)

    by_parity(last + 1, drain)
    outs = []
    for buf in bufs:
        acc = buf[4][...]
        outs.append(acc[:HEAD_DIM] / acc[HEAD_DIM:HEAD_DIM + 1])
    return outs


_ALPHA, _MAX, _CMAX, _CONST = 0, 2, 3, 4
_BUFS_PER_STREAM = 6


def _flash_scratch(n_streams, tk, tq):
    per_stream = [pltpu.VMEM((tk, tq), F32), pltpu.VMEM((tk, tq), F32),
                  pltpu.VMEM((tk, tq), BF16), pltpu.VMEM((tk, tq), BF16),
                  pltpu.VMEM((HEAD_DIM + SUM_ROWS, tq), F32), pltpu.VMEM((8, tq), F32)]
    return per_stream * n_streams


def _flash_bufs(refs):
    n = _BUFS_PER_STREAM
    return [tuple(refs[n * k:n * k + n]) for k in range(len(refs) // n)]


def _store_t_blocks(ref, val, blk):
    for c in range(val.shape[1] // blk):
        ref[c] = val[:, c * blk:(c + 1) * blk].astype(ref.dtype)


def _proj_kernel(x_ref, g_ref, wnat_ref, wt_ref, gq_ref, wuqt_ref, gkv_ref, wuk_ref, wuvt_ref,
                 place_ref, csn_ref, cost_ref, sint_ref,
                 qa_ref, ka_ref, va_ref, kb_ref, kc_ref, kd_ref,
                 qbt_ref, vbt_ref, qct_ref, vct_ref, qdt_ref, vdt_ref):
    h = _rms(x_ref[...], g_ref[...]).astype(BF16)
    nat = jnp.dot(h, wnat_ref[...], preferred_element_type=F32)
    qa_ref[...] = (nat[:, 0:256] * (HEAD_DIM ** -0.5 * LOG2E)).astype(BF16)
    ka_ref[...] = nat[:, 256:512].astype(BF16)
    va_ref[...] = nat[:, 512:768].astype(BF16)
    kb_ref[...] = nat[:, 768:1024].astype(BF16)
    kc_ref[...] = nat[:, 1024:1280].astype(BF16)
    cq = nat[:, 1280:1536]
    ckv = nat[:, 1536:1664]
    kr2 = nat[:, 1664:1792]

    tr = _nt(wt_ref[...], h)
    _store_t_blocks(qbt_ref, tr[0:256] * (DIFF_QK ** -0.5 * LOG2E), FULL_T)
    _store_t_blocks(vbt_ref, tr[256:512], FULL_T)
    _store_t_blocks(qct_ref, tr[512:768] * (HEAD_DIM ** -0.5 * LOG2E), BAND_T)
    _store_t_blocks(vct_ref, tr[768:1024], BAND_T)

    cqn = _rms(cq, gq_ref[...]).astype(BF16)
    qd_t = _nt(wuqt_ref[...], cqn)
    cos_t = cost_ref[...]
    sin_t = sint_ref[...]
    pieces = []
    for hd in range(N_HEADS):
        b0 = hd * PAIR
        x1 = qd_t[b0 + 64:b0 + 80]
        x2 = qd_t[b0 + 80:b0 + 96]
        pieces += [qd_t[b0:b0 + 64], x1 * cos_t - x2 * sin_t, x1 * sin_t + x2 * cos_t,
                   qd_t[b0 + 96:b0 + 128]]
    qd_t = jnp.concatenate(pieces, axis=0) * (MLA_QK ** -0.5 * LOG2E)
    _store_t_blocks(qdt_ref, qd_t, FULL_T)

    ckvn = _rms(ckv, gkv_ref[...]).astype(BF16)
    k_nope = jnp.dot(ckvn, wuk_ref[...], preferred_element_type=F32)
    t = kr2 * csn_ref[...]
    roped = t + pltpu.roll(t, 96, 1)
    kd = k_nope + jnp.dot(roped.astype(BF16), place_ref[...], preferred_element_type=F32)
    kd_ref[...] = kd.astype(BF16)
    _store_t_blocks(vdt_ref, _nt(wuvt_ref[...], ckvn), FULL_T)


def _proj_call(x, g, p, rope):
    B, S, D = x.shape
    ts = TOK_TILE
    nt = S // ts
    nf, nb = ts // FULL_T, ts // BAND_T

    def full(a):
        return pl.BlockSpec(a.shape, lambda b, t: (0,) * a.ndim)

    nat_spec = lambda w: pl.BlockSpec((None, ts, w), lambda b, t: (b, t, 0))
    tb_spec = lambda n, r, blk: pl.BlockSpec((None, n, r, blk), lambda b, t: (b, t, 0, 0))
    nat_shape = lambda w: jax.ShapeDtypeStruct((B, S, w), BF16)
    tb_shape = lambda r, blk: jax.ShapeDtypeStruct((B, S // blk, r, blk), BF16)

    weights = [g, p['w_nat'], p['w_t'], p['g_q'], p['w_uq_t'], p['g_kv'], p['w_uk'], p['w_uv_t'], p['place']]
    in_specs = ([nat_spec(D)] + [full(a) for a in weights] +
                [pl.BlockSpec((ts, PAIR), lambda b, t: (t, 0)),
                 pl.BlockSpec((None, MLA_HALF, ts), lambda b, t: (t, 0, 0)),
                 pl.BlockSpec((None, MLA_HALF, ts), lambda b, t: (t, 0, 0))])
    out_shape = ([nat_shape(GROUP)] * 5 + [nat_shape(2 * GROUP)] +
                 [tb_shape(GROUP, FULL_T)] * 2 + [tb_shape(GROUP, BAND_T)] * 2 +
                 [tb_shape(2 * GROUP, FULL_T), tb_shape(GROUP, FULL_T)])
    out_specs = ([nat_spec(GROUP)] * 5 + [nat_spec(2 * GROUP)] +
                 [tb_spec(nf, GROUP, FULL_T)] * 2 + [tb_spec(nb, GROUP, BAND_T)] * 2 +
                 [tb_spec(nf, 2 * GROUP, FULL_T), tb_spec(nf, GROUP, FULL_T)])
    return pl.pallas_call(
        _proj_kernel,
        grid=(B, nt),
        in_specs=in_specs,
        out_specs=out_specs,
        out_shape=out_shape,
        compiler_params=pltpu.CompilerParams(
            dimension_semantics=("parallel", "parallel"), vmem_limit_bytes=V7X_VMEM_LIMIT),
        name="proj",
    )(x, *weights, rope['csn'], rope['cos_t'], rope['sin_t'])


def _na_kernel(q_ref, kp_ref, kc_ref, kn_ref, vp_ref, vc_ref, vn_ref, bias_ref, o_ref, kk_ref, vv_ref,
               *, n_rows):
    g = pl.program_id(1)
    blk = NA_GROUP * GRID_W
    win = NA_ROWS * GRID_W
    for c, (kr, vr) in enumerate(((kp_ref, vp_ref), (kc_ref, vc_ref), (kn_ref, vn_ref))):
        kk_ref[c * blk:(c + 1) * blk, :] = kr[...]
        vv_ref[c * blk:(c + 1) * blk, :] = vr[...]
    lane = lax.broadcasted_iota(jnp.int32, (GRID_W, PAIR), 1)

    first_head = lane < HEAD_DIM

    def rows_body(jj, carry):
        v_pairs, scores, probs = [], [], []
        for dj in range(NA_ROWS_PER_ITER):
            j = jj * NA_ROWS_PER_ITER + dj
            r = g * NA_GROUP + j
            r0 = jnp.clip(r - NA_ROWS // 2, 0, n_rows - NA_ROWS)
            start = pl.multiple_of((r0 - (g - 1) * NA_GROUP) * GRID_W, GRID_W)
            q_row = q_ref[pl.ds(pl.multiple_of(j * GRID_W, GRID_W), GRID_W), :]
            k_win = kk_ref[pl.ds(start, win), :]
            v_win = vv_ref[pl.ds(start, win), :]
            for pr in range(2):
                q2 = q_row[:, pr * PAIR:(pr + 1) * PAIR]
                zero = jnp.zeros_like(q2)
                qz = jnp.concatenate([jnp.where(first_head, q2, zero), jnp.where(first_head, zero, q2)], axis=0)
                bias = jnp.concatenate([bias_ref[r - r0, 2 * pr], bias_ref[r - r0, 2 * pr + 1]], axis=0)
                scores.append(_nt(qz, k_win[:, pr * PAIR:(pr + 1) * PAIR]) + bias)
                v_pairs.append(v_win[:, pr * PAIR:(pr + 1) * PAIR])
        for s in scores:
            e = jnp.exp2(s - jnp.max(s, axis=-1, keepdims=True))
            probs.append((e.astype(BF16), jnp.sum(e, axis=-1, keepdims=True)))
        outs = []
        for (e, den), v2 in zip(probs, v_pairs):
            o2 = jnp.dot(e, v2, preferred_element_type=F32) / den
            outs.append(jnp.where(first_head, o2[:GRID_W], o2[GRID_W:]))
        for dj in range(NA_ROWS_PER_ITER):
            qs = pl.multiple_of((jj * NA_ROWS_PER_ITER + dj) * GRID_W, GRID_W)
            o_ref[pl.ds(qs, GRID_W), :] = jnp.concatenate(outs[2 * dj:2 * dj + 2], axis=1).astype(o_ref.dtype)
        return carry

    lax.fori_loop(0, NA_GROUP // NA_ROWS_PER_ITER, rows_body, 0)


def _na_call(q, k, v, bias):
    B, S, _ = q.shape
    n_rows = S // GRID_W
    blk = NA_GROUP * GRID_W
    ng = S // blk
    cur = pl.BlockSpec((None, blk, GROUP), lambda b, g: (b, g, 0))
    prv = pl.BlockSpec((None, blk, GROUP), lambda b, g: (b, jnp.maximum(g - 1, 0), 0))
    nxt = pl.BlockSpec((None, blk, GROUP), lambda b, g: (b, jnp.minimum(g + 1, ng - 1), 0))
    return pl.pallas_call(
        functools.partial(_na_kernel, n_rows=n_rows),
        grid=(B, ng),
        in_specs=[cur, prv, cur, nxt, prv, cur, nxt,
                  pl.BlockSpec(bias.shape, lambda b, g: (0, 0, 0, 0))],
        out_specs=cur,
        out_shape=jax.ShapeDtypeStruct((B, S, GROUP), BF16),
        scratch_shapes=[pltpu.VMEM((3 * blk, GROUP), BF16), pltpu.VMEM((3 * blk, GROUP), BF16)],
        compiler_params=pltpu.CompilerParams(
            dimension_semantics=("parallel", "parallel"), vmem_limit_bytes=V7X_VMEM_LIMIT),
        name="na_attn",
    )(q, k, k, k, v, v, v, bias)


def _diff_kernel(c_ref, g_ref, qt_ref, k_ref, vt_ref, bias_ref, o_ref, *scratch, nk, lam_scale):
    i0 = pl.program_id(2) * DIFF_Q_BLOCKS
    tk = vt_ref.shape[2]
    tq = qt_ref.shape[2]
    row = lax.broadcasted_iota(jnp.int32, (PAIR, tq), 0)
    lam = c_ref[0:1, 0:1]
    j_lo = jnp.maximum(i0 - 2, 0)
    j_hi = jnp.minimum(i0 + DIFF_Q_BLOCKS + 2, nk)
    ranges = ((0, j_lo), (j_lo, j_hi), (j_hi, nk))

    def load_k(j):
        return k_ref[pl.ds(pl.multiple_of(j * tk, tk), tk), :]

    streams = []
    for qi in range(DIFF_Q_BLOCKS):
        q_t = qt_ref[qi]
        for hh in range(2):

            def load_vt(j, hh=hh):
                return vt_ref[j, hh * HEAD_DIM:(hh + 1) * HEAD_DIM, :]

            def near_bias(j, hh=hh, qi=qi):
                e0 = 2 * (j - i0 - qi) + 4

                def tile(d):
                    return bias_ref[jnp.clip(e0 + d, 0, 8), hh]
                t0 = tile(0)
                return jnp.concatenate(
                    [jnp.concatenate([t0, tile(-1)], axis=1),
                     jnp.concatenate([tile(1), t0], axis=1)], axis=0)

            for c in range(2):
                r_lo = hh * HEAD_DIM + c * DIFF_QK
                qz = jnp.where((row >= r_lo) & (row < r_lo + DIFF_QK), q_t, jnp.zeros_like(q_t))
                streams.append(dict(qz=qz, load_k=load_k, load_vt=load_vt,
                                    shifts=(c_ref[1 + hh:2 + hh, 0:1], 0.0, c_ref[3 + hh:4 + hh, 0:1]),
                                    bias_fns=(None, near_bias, None), first_bias=near_bias))
    maps = _flash_pipelined(streams, ranges, _flash_bufs(scratch))
    for qi in range(DIFF_Q_BLOCKS):
        heads = []
        for hh in range(2):
            k0 = 4 * qi + 2 * hh
            o = maps[k0] - lam * maps[k0 + 1]
            y = o * lax.rsqrt(jnp.mean(o * o, axis=0, keepdims=True) + EPS) * g_ref[:, 0:1]
            heads.append(y * lam_scale)
        o_ref[qi * tq:(qi + 1) * tq, :] = jnp.concatenate(heads, axis=0).T.astype(o_ref.dtype)


def _resident(block_shape, index_map):
    return pl.BlockSpec(block_shape, index_map, pipeline_mode=pl.Buffered(1))


def _diff_call(consts, g_col, q_t, k, v_t, bias, lam_scale):
    B, nq, _, tq = q_t.shape
    S = k.shape[1]
    nk, _, tk = v_t.shape[1:]
    nqb = DIFF_Q_BLOCKS
    return pl.pallas_call(
        functools.partial(_diff_kernel, nk=nk, lam_scale=lam_scale),
        grid=(B, 2, nq // nqb),
        in_specs=[
            pl.BlockSpec((None, 8, 128), lambda b, p, i: (p, 0, 0)),
            pl.BlockSpec(g_col.shape, lambda b, p, i: (0, 0)),
            pl.BlockSpec((None, nqb, PAIR, tq), lambda b, p, i: (b, i, p, 0)),
            _resident((None, S, PAIR), lambda b, p, i: (b, 0, p)),
            _resident((None, nk, PAIR, tk), lambda b, p, i: (b, 0, p, 0)),
            _resident((9, 2, BAND_T, BAND_T), lambda b, p, i: (0, p, 0, 0)),
        ],
        out_specs=pl.BlockSpec((None, nqb * tq, PAIR), lambda b, p, i: (b, i, p)),
        out_shape=jax.ShapeDtypeStruct((B, S, GROUP), BF16),
        scratch_shapes=_flash_scratch(4 * nqb, tk, tq),
        compiler_params=pltpu.CompilerParams(
            dimension_semantics=("parallel", "parallel", "parallel"), vmem_limit_bytes=V7X_VMEM_LIMIT),
        name="diff_attn",
    )(consts, g_col, q_t, k, v_t, bias)


def _dil_kernel(qt_ref, k_ref, vt_ref, bias_ref, o_ref, *scratch, nk):
    i = pl.program_id(1)
    t = BAND_T
    q_t = jnp.concatenate([qt_ref[a] for a in range(DIL_Q_TILES)], axis=1)
    row = lax.broadcasted_iota(jnp.int32, (PAIR, q_t.shape[1]), 0)
    j0 = i * DIL_Q_TILES
    j_lo = jnp.maximum(j0 - DIL_BAND, 0)
    j_hi = jnp.minimum(j0 + DIL_Q_TILES + DIL_BAND, nk)
    streams = []
    for head in range(N_HEADS):
        pr, hh = divmod(head, 2)

        def load_k(j, pr=pr):
            return k_ref[pl.ds(pl.multiple_of(j * t, t), t), pr * PAIR:(pr + 1) * PAIR]

        def load_vt(j, head=head):
            return vt_ref[j, head * HEAD_DIM:(head + 1) * HEAD_DIM, :]

        def bias(j, head=head):
            e0 = j - j0 + DIL_BAND + DIL_Q_TILES - 1
            return jnp.concatenate([bias_ref[e0 - a, head] for a in range(DIL_Q_TILES)], axis=1)

        q_pair = q_t[pr * PAIR:(pr + 1) * PAIR]
        qz = jnp.where((row // HEAD_DIM) == hh, q_pair, jnp.zeros_like(q_pair))
        streams.append(dict(qz=qz, load_k=load_k, load_vt=load_vt, shifts=(0.0,), bias_fns=(bias,)))
    heads = _flash_pipelined(streams, ((j_lo, j_hi),), _flash_bufs(scratch))
    o_ref[...] = jnp.concatenate(heads, axis=0).T.astype(o_ref.dtype)


def _dil_call(q_t, k, v_t, bias):
    B, nq, _, t = q_t.shape
    S = k.shape[1]
    nk = v_t.shape[1]
    nqt = DIL_Q_TILES
    return pl.pallas_call(
        functools.partial(_dil_kernel, nk=nk),
        grid=(B, nq // nqt),
        in_specs=[
            pl.BlockSpec((None, nqt, GROUP, t), lambda b, i: (b, i, 0, 0)),
            _resident((None, S, GROUP), lambda b, i: (b, 0, 0)),
            _resident((None, nk, GROUP, t), lambda b, i: (b, 0, 0, 0)),
            _resident(bias.shape, lambda b, i: (0, 0, 0, 0)),
        ],
        out_specs=pl.BlockSpec((None, nqt * t, GROUP), lambda b, i: (b, i, 0)),
        out_shape=jax.ShapeDtypeStruct((B, S, GROUP), BF16),
        scratch_shapes=_flash_scratch(N_HEADS, t, nqt * t),
        compiler_params=pltpu.CompilerParams(
            dimension_semantics=("parallel", "parallel"), vmem_limit_bytes=V7X_VMEM_LIMIT),
        name="dil_attn",
    )(q_t, k, v_t, bias)


def _mla_kernel(qt_ref, k_ref, vt_ref, o_ref, *scratch, nk):
    tk = vt_ref.shape[2]
    tq = qt_ref.shape[2]
    streams = []
    for qi in range(MLA_Q_BLOCKS):
        for hh in range(2):

            def load_k(j, hh=hh):
                return k_ref[pl.ds(pl.multiple_of(j * tk, tk), tk), hh * PAIR:(hh + 1) * PAIR]

            def load_vt(j, hh=hh):
                return vt_ref[j, hh * HEAD_DIM:(hh + 1) * HEAD_DIM, :]

            qz = qt_ref[qi, hh * PAIR:(hh + 1) * PAIR, :]
            streams.append(dict(qz=qz, load_k=load_k, load_vt=load_vt, shifts=(0.0,), bias_fns=(None,)))
    outs = _flash_pipelined(streams, ((0, nk),), _flash_bufs(scratch))
    for qi in range(MLA_Q_BLOCKS):
        o_t = jnp.concatenate(outs[2 * qi:2 * qi + 2], axis=0)
        o_ref[qi * tq:(qi + 1) * tq, :] = o_t.T.astype(o_ref.dtype)


def _mla_call(q_t, k, v_t):
    B, nq, _, tq = q_t.shape
    S = k.shape[1]
    nk, _, tk = v_t.shape[1:]
    nqb = MLA_Q_BLOCKS
    return pl.pallas_call(
        functools.partial(_mla_kernel, nk=nk),
        grid=(B, 2, nq // nqb),
        in_specs=[
            pl.BlockSpec((None, nqb, 2 * PAIR, tq), lambda b, p, i: (b, i, p, 0)),
            _resident((None, S, 2 * PAIR), lambda b, p, i: (b, 0, p)),
            _resident((None, nk, PAIR, tk), lambda b, p, i: (b, 0, p, 0)),
        ],
        out_specs=pl.BlockSpec((None, nqb * tq, PAIR), lambda b, p, i: (b, i, p)),
        out_shape=jax.ShapeDtypeStruct((B, S, GROUP), BF16),
        scratch_shapes=_flash_scratch(2 * nqb, tk, tq),
        compiler_params=pltpu.CompilerParams(
            dimension_semantics=("parallel", "parallel", "parallel"), vmem_limit_bytes=V7X_VMEM_LIMIT),
        name="mla_attn",
    )(q_t, k, v_t)


def _post_kernel(x_ref, oa_ref, ob_ref, oc_ref, od_ref, wo_ref, g_ref, wg_ref, wu_ref, wd_ref, gf_ref,
                 out_ref, acc_ref, h_ref, *, final):
    f = pl.program_id(1)

    @pl.when(f == 0)
    def _():
        x1 = x_ref[...]
        for gi, o_ref in enumerate((oa_ref, ob_ref, oc_ref, od_ref)):
            x1 = x1 + jnp.dot(o_ref[...], wo_ref[gi], preferred_element_type=F32)
        acc_ref[...] = x1
        h_ref[...] = _rms(x1, g_ref[...]).astype(BF16)

    h = h_ref[...]
    gate = jnp.dot(h, wg_ref[...], preferred_element_type=F32)
    up = jnp.dot(h, wu_ref[...], preferred_element_type=F32)
    act = (gate / (1.0 + jnp.exp(-gate))) * up
    acc_ref[...] += jnp.dot(act.astype(BF16), wd_ref[...], preferred_element_type=F32)

    @pl.when(f == pl.num_programs(1) - 1)
    def _():
        y = acc_ref[...]
        out_ref[...] = _rms(y, gf_ref[...]) if final else y


def _post_call(x, o_a, o_b, o_c, o_d, p, g_final, final):
    N, D = x.shape
    ts = TOK_TILE
    d_ff = p['w_gate'].shape[1]
    fc = d_ff // FF_SPLIT
    tok = lambda w: pl.BlockSpec((ts, w), lambda t, f: (t, 0))
    return pl.pallas_call(
        functools.partial(_post_kernel, final=final),
        grid=(N // ts, FF_SPLIT),
        in_specs=[tok(D), tok(GROUP), tok(GROUP), tok(GROUP), tok(GROUP),
                  pl.BlockSpec((4, GROUP, D), lambda t, f: (0, 0, 0)),
                  pl.BlockSpec((1, D), lambda t, f: (0, 0)),
                  pl.BlockSpec((D, fc), lambda t, f: (0, f)),
                  pl.BlockSpec((D, fc), lambda t, f: (0, f)),
                  pl.BlockSpec((fc, D), lambda t, f: (f, 0)),
                  pl.BlockSpec((1, D), lambda t, f: (0, 0))],
        out_specs=tok(D),
        out_shape=jax.ShapeDtypeStruct((N, D), F32),
        scratch_shapes=[pltpu.VMEM((ts, D), F32), pltpu.VMEM((ts, D), BF16)],
        compiler_params=pltpu.CompilerParams(
            dimension_semantics=("parallel", "arbitrary"), vmem_limit_bytes=V7X_VMEM_LIMIT),
        name="post_ffn",
    )(x, o_a, o_b, o_c, o_d, p['w_o'], p['g_ffn'], p['w_gate'], p['w_up'], p['w_down'], g_final)


def _t5_bucket(rel):
    nb = T5_BUCKETS // 2
    max_exact = nb // 2
    side = jnp.where(rel > 0, nb, 0)
    n = jnp.abs(rel)
    large = max_exact + (jnp.log(jnp.maximum(n, 1).astype(F32) / max_exact)
                         / math.log(T5_MAX_DIST / max_exact) * (nb - max_exact)).astype(jnp.int32)
    large = jnp.minimum(large, nb - 1)
    return side + jnp.where(n < max_exact, n, large)


def _lookup(table, idx):
    onehot = idx[..., None, None] == jnp.arange(table.shape[0])[:, None]
    return jnp.sum(jnp.where(onehot, table.astype(F32), 0.0), axis=-2)


def _toeplitz(w, t):
    lead = w.shape[:-1]
    w_pad = jnp.concatenate([w, jnp.zeros(lead + (1,), w.dtype)], axis=-1)
    m = jnp.tile(w_pad, t)[..., :t * (2 * t - 1)].reshape(lead + (t, 2 * t - 1))
    return m[..., t - 1:]


def _toeplitz_tiles(vec_fn, t, n_side):
    reach = n_side * t + t - 1
    w = jnp.moveaxis(vec_fn(jnp.arange(-reach, reach + 1)), -1, 0)
    tiles = [_toeplitz(w[:, (e + n_side) * t:(e + n_side + 2) * t - 1], t) for e in range(-n_side, n_side + 1)]
    return jnp.swapaxes(jnp.stack(tiles), -1, -2)


def _diff_bias(t5_diff):
    def vec(rel):
        return _lookup(t5_diff, _t5_bucket(rel)) * LOG2E
    tiles = _toeplitz_tiles(vec, BAND_T, 4)
    far = vec(jnp.array([-2 * T5_MAX_DIST, 2 * T5_MAX_DIST]))
    return tiles, far


def _dil_bias(t5_dil):
    def vec(rel):
        n = jnp.abs(rel)
        mult = jnp.zeros(rel.shape, F32)
        for window, dil in DIL_PATTERNS:
            mult = mult + ((rel % dil == 0) & (n <= window // 2)).astype(F32)
        b = _lookup(t5_dil, _t5_bucket(rel)) + jnp.log(jnp.maximum(mult, 1.0))[..., None]
        return jnp.where((mult > 0)[..., None], b * LOG2E, NEG)
    return _toeplitz_tiles(vec, BAND_T, DIL_BAND + DIL_Q_TILES - 1)


def _na_bias(rpb):
    d = jnp.arange(NA_ROWS)[:, None]
    ridx = jnp.arange(NA_ROWS)[None, :] - d + NA_ROWS - 1
    qc = jnp.arange(GRID_W)[:, None]
    kc = jnp.arange(GRID_W)[None, :]
    c0 = jnp.clip(qc - NA_COLS // 2, 0, GRID_W - NA_COLS)
    valid = (kc >= c0) & (kc < c0 + NA_COLS)
    rows = rpb.astype(F32)[:, ridx]
    side = GRID_W - NA_COLS
    vals = _toeplitz(jnp.pad(rows, ((0, 0),) * 3 + ((side, side),)), GRID_W)
    vals = jnp.where(valid, vals * LOG2E, NEG)
    vals = vals.transpose(1, 0, 3, 2, 4)
    return vals.reshape(NA_ROWS, N_HEADS, GRID_W, NA_ROWS * GRID_W)


def _rope_tables(S):
    inv_freq = ROPE_THETA ** (-jnp.arange(MLA_HALF, dtype=F32) / MLA_HALF)
    ang = jnp.arange(S).astype(F32)[:, None] * inv_freq[None, :]
    cos, sin = jnp.cos(ang), jnp.sin(ang)
    csn = jnp.concatenate([cos, cos, -sin, sin, jnp.zeros((S, PAIR - 4 * MLA_HALF), F32)], axis=1)
    nt = S // TOK_TILE
    to_t = lambda a: a.reshape(nt, TOK_TILE, MLA_HALF).transpose(0, 2, 1)
    return {'csn': csn, 'cos_t': to_t(cos), 'sin_t': to_t(sin)}


def _layer_weights(w_in, w_uq, w_ukv, w_o, w_gate, w_up, w_down, g_q, g_kv, g_ffn):
    G = GROUP
    col = lambda i: w_in[:, i * G:(i + 1) * G]
    cq, ckv, kr = w_in[:, 9 * G:10 * G], w_in[:, 10 * G:10 * G + 128], w_in[:, 10 * G + 128:]
    kr_swap = jnp.concatenate([kr[:, MLA_HALF:], kr[:, :MLA_HALF]], axis=1)
    kr2 = jnp.concatenate([kr, kr_swap, jnp.zeros((w_in.shape[0], PAIR - 2 * MLA_ROPE), w_in.dtype)], axis=1)
    w_nat = jnp.concatenate([col(0), col(1), col(2), col(4), col(7), cq, ckv, kr2], axis=1)
    w_t = jnp.concatenate([col(3), col(5), col(6), col(8)], axis=1).T
    uq = w_uq.reshape(w_uq.shape[0], N_HEADS, MLA_QK)
    uq = jnp.pad(uq, ((0, 0), (0, 0), (0, PAIR - MLA_QK))).reshape(w_uq.shape[0], N_HEADS * PAIR)
    ukv = w_ukv.reshape(w_ukv.shape[0], N_HEADS, 2 * HEAD_DIM)
    uk = ukv.at[:, :, HEAD_DIM:].set(0.0).reshape(w_ukv.shape[0], N_HEADS * PAIR)
    uv = ukv[:, :, HEAD_DIM:].reshape(w_ukv.shape[0], GROUP)
    place = jnp.zeros((PAIR, N_HEADS * PAIR), F32)
    r = jnp.arange(MLA_ROPE)
    for hd in range(N_HEADS):
        place = place.at[r, hd * PAIR + HEAD_DIM + r].set(1.0)
    b = lambda a: a.astype(BF16)
    row = lambda a: a.astype(F32)[None, :]
    return {
        'w_nat': b(w_nat), 'w_t': b(w_t), 'w_uq_t': b(uq.T), 'w_uk': b(uk), 'w_uv_t': b(uv.T),
        'place': b(place), 'g_q': row(g_q), 'g_kv': row(g_kv), 'g_ffn': row(g_ffn),
        'w_o': b(w_o.reshape(4, GROUP, w_o.shape[1])), 'w_gate': b(w_gate), 'w_up': b(w_up),
        'w_down': b(w_down),
    }


def kernel(x, attn_norm, w_in, na_rpb, diff_lambda, diff_subln, mla_q_norm, w_uq, mla_kv_norm, w_ukv,
           t5_table, w_o, ffn_norm, w_gate, w_up, w_down, final_norm):
    B, S, D = x.shape
    depth = w_in.shape[0]
    rope = _rope_tables(S)
    diff_tiles, diff_far = _diff_bias(t5_table[:, :N_HEADS])
    dil_tiles = _dil_bias(t5_table[:, N_HEADS:])
    g_final = final_norm.astype(F32)[None, :]

    for l in range(depth):
        p = _layer_weights(w_in[l], w_uq[l], w_ukv[l], w_o[l], w_gate[l], w_up[l], w_down[l],
                           mla_q_norm[l], mla_kv_norm[l], ffn_norm[l])
        (qa, ka, va, kb, kc, kd, qb_t, vb_t, qc_t, vc_t, qd_t, vd_t) = _proj_call(
            x, attn_norm[l].astype(F32)[None, :], p, rope)

        o_a = _na_call(qa, ka, va, _na_bias(na_rpb[l]))

        lam_init = 0.8 - 0.6 * math.exp(-0.3 * l)
        lq1, lk1, lq2, lk2 = [diff_lambda[l, j].astype(F32) for j in range(4)]
        lam = jnp.exp(jnp.sum(lq1 * lk1)) - jnp.exp(jnp.sum(lq2 * lk2)) + lam_init
        far = diff_far.reshape(2, 2, 2).transpose(1, 0, 2).reshape(2, 4)
        consts = jnp.concatenate([jnp.broadcast_to(lam, (2, 1)), far, jnp.zeros((2, 3), F32)], axis=1)
        consts = jnp.broadcast_to(consts[:, :, None], (2, 8, 128))
        g_col = jnp.broadcast_to(diff_subln[l].astype(F32)[:, None], (HEAD_DIM, 128))
        o_b = _diff_call(consts, g_col, qb_t, kb, vb_t, diff_tiles, 1.0 - lam_init)

        o_c = _dil_call(qc_t, kc, vc_t, dil_tiles)
        o_d = _mla_call(qd_t, kd, vd_t)

        flat = lambda a: a.reshape(B * S, a.shape[-1])
        x = _post_call(flat(x), flat(o_a), flat(o_b), flat(o_c), flat(o_d), p, g_final,
                       final=(l == depth - 1)).reshape(B, S, D)
    return x
```

```python
import functools
import math

import jax
import jax.numpy as jnp
from jax import lax
from jax.experimental import pallas as pl
from jax.experimental.pallas import tpu as pltpu

F32 = jnp.float32
BF16 = jnp.bfloat16

N_HEADS = 4
HEAD_DIM = 64
GROUP = N_HEADS * HEAD_DIM
PAIR = 2 * HEAD_DIM
GRID_W = 64
NA_ROWS = 8
NA_COLS = 16
DIFF_QK = HEAD_DIM // 2
DIL_PATTERNS = ((128, 1), (512, 4), (2048, 16))
MLA_ROPE = HEAD_DIM // 2
MLA_HALF = MLA_ROPE // 2
MLA_QK = HEAD_DIM + MLA_ROPE
ROPE_THETA = 10000.0
T5_BUCKETS = 32
T5_MAX_DIST = 1024
EPS = 1e-6
LOG2E = 1.4426950408889634
NEG = -1e30
EXP_CHUNK = 32

V7X_VMEM_LIMIT = 56 * 1024 * 1024

TOK_TILE = 512
FULL_T = 512
BAND_T = 256
DIL_BAND = -(-max(w // 2 for w, _ in DIL_PATTERNS) // BAND_T)
DIL_Q_TILES = 2
MLA_Q_BLOCKS = 4
DIFF_Q_BLOCKS = 1
NA_GROUP = 8
NA_ROWS_PER_ITER = 2
FF_SPLIT = 2


def _rms(x, g):
    return x * lax.rsqrt(jnp.mean(x * x, axis=-1, keepdims=True) + EPS) * g


def _nt(a, b):
    return lax.dot_general(a, b, (((1,), (1,)), ((), ())), preferred_element_type=F32)


def _zero_of(x):
    u = lax.bitcast_convert_type(x, jnp.uint32)
    return lax.bitcast_convert_type((u >> 16) >> 16, F32)


def _flash_pipelined(streams, ranges, bufs):
    tk, tq = bufs[0][0].shape
    first = ranges[0][0]
    last = ranges[-1][1] - 1
    def scores(st, j, bias_fn, s_ref):
        s = jnp.dot(st['load_k'](j), st['qz'], preferred_element_type=F32)
        if bias_fn is not None:
            s = s + bias_fn(j)
        s_ref[...] = s
        return jnp.max(s, axis=0, keepdims=True)

    def accumulate(st, buf, j, par):
        acc_ref, stat_ref = buf[4], buf[5]
        acc_ref[...] = (stat_ref[_ALPHA + par:_ALPHA + par + 1, :] * acc_ref[...]
                        + jnp.dot(st['load_vt'](j), buf[2 + par][...], preferred_element_type=F32))

    def softmax_update(buf, par, s_new):
        stat_ref = buf[5]
        m = stat_ref[_MAX:_MAX + 1, :]
        c = stat_ref[_CONST:_CONST + 1, :]
        m_new = jnp.maximum(m, stat_ref[_CMAX:_CMAX + 1, :] + c)
        ref_rows = jnp.broadcast_to(m_new - c, (8, tq))
        p_sum = jnp.zeros((8, tq), F32)
        for lo_row in range(0, tk, EXP_CHUNK):
            rows = slice(lo_row, lo_row + EXP_CHUNK)
            ref = ref_rows if s_new is None else ref_rows + _zero_of(s_new[lo_row:lo_row + 8, :])
            p = jnp.exp2(buf[1 - par][rows, :].reshape(EXP_CHUNK // 8, 8, tq) - ref[None])
            p_sum = p_sum + jnp.sum(p, axis=0)
            buf[3 - par][rows, :] = p.reshape(EXP_CHUNK, tq).astype(BF16)
        alpha = jnp.exp2(m - m_new)
        stat_ref[_ALPHA + 1 - par:_ALPHA + 2 - par, :] = alpha
        stat_ref[_MAX:_MAX + 1, :] = m_new
        stat_ref[_DEN:_DEN + 1, :] = alpha * stat_ref[_DEN:_DEN + 1, :] + jnp.sum(p_sum, axis=0, keepdims=True)

    def step(par, r, t):
        for st, buf in zip(streams, bufs):
            stat_ref = buf[5]
            s_new = jnp.dot(st['load_k'](t), st['qz'], preferred_element_type=F32)
            if st['bias_fns'][r] is not None:
                s_new = s_new + st['bias_fns'][r](t)
            buf[par][...] = s_new
            cmax_t = jnp.max(s_new, axis=0, keepdims=True)
            softmax_update(buf, par, s_new)
            stat_ref[_CMAX:_CMAX + 1, :] = cmax_t
            stat_ref[_CONST:_CONST + 1, :] = jnp.zeros((1, tq), F32) + st['shifts'][r]
            accumulate(st, buf, jnp.maximum(t - 2, first), par)

    def drain(par):
        for st, buf in zip(streams, bufs):
            softmax_update(buf, par, None)
            accumulate(st, buf, jnp.maximum(last - 1, first), par)
            accumulate(st, buf, last, 1 - par)

    for st, buf in zip(streams, bufs):
        acc_ref, stat_ref = buf[4], buf[5]
        cmax = scores(st, first, st.get('first_bias', st['bias_fns'][0]), buf[0])
        buf[3][...] = jnp.zeros(buf[3].shape, buf[3].dtype)
        acc_ref[...] = jnp.zeros_like(acc_ref)
        stat_ref[_ALPHA:_ALPHA + 2, :] = jnp.ones((2, tq), F32)
        stat_ref[_DEN:_DEN + 1, :] = jnp.zeros((1, tq), F32)
        stat_ref[_MAX:_MAX + 1, :] = jnp.full((1, tq), NEG, F32)
        stat_ref[_CMAX:_CMAX + 1, :] = cmax
        stat_ref[_CONST:_CONST + 1, :] = (jnp.zeros((1, tq), F32)
                                          + (0.0 if 'first_bias' in st else st['shifts'][0]))

    def by_parity(j, fn):
        par = (j - first) % 2
        if isinstance(par, int):
            fn(par)
        else:
            pl.when(par == 0)(functools.partial(fn, 0))
            pl.when(par == 1)(functools.partial(fn, 1))

    for r, (lo, hi) in enumerate(ranges):

        def body(t, c, r=r):
            by_parity(t, lambda par: step(par, r, t))
            return c

        lax.fori_loop(jnp.maximum(lo, first + 1), hi, body, 0)

    by_parity(last + 1, drain)
    outs = []
    for buf in bufs:
        outs.append(buf[4][...] / buf[5][_DEN:_DEN + 1, :])
    return outs


_ALPHA, _MAX, _CMAX, _CONST, _DEN = 0, 2, 3, 4, 5
_BUFS_PER_STREAM = 6


def _flash_scratch(n_streams, tk, tq):
    per_stream = [pltpu.VMEM((tk, tq), F32), pltpu.VMEM((tk, tq), F32),
                  pltpu.VMEM((tk, tq), BF16), pltpu.VMEM((tk, tq), BF16),
                  pltpu.VMEM((HEAD_DIM, tq), F32), pltpu.VMEM((8, tq), F32)]
    return per_stream * n_streams


def _flash_bufs(refs):
    n = _BUFS_PER_STREAM
    return [tuple(refs[n * k:n * k + n]) for k in range(len(refs) // n)]


def _store_t_blocks(ref, val, blk):
    for c in range(val.shape[1] // blk):
        ref[c] = val[:, c * blk:(c + 1) * blk].astype(ref.dtype)


def _proj_kernel(x_ref, g_ref, wnat_ref, wt_ref, gq_ref, wuqt_ref, gkv_ref, wuk_ref, wuvt_ref,
                 place_ref, csn_ref, cost_ref, sint_ref,
                 qa_ref, ka_ref, va_ref, kb_ref, kc_ref, kd_ref,
                 qbt_ref, vbt_ref, qct_ref, vct_ref, qdt_ref, vdt_ref):
    h = _rms(x_ref[...], g_ref[...]).astype(BF16)
    nat = jnp.dot(h, wnat_ref[...], preferred_element_type=F32)
    qa_ref[...] = (nat[:, 0:256] * (HEAD_DIM ** -0.5 * LOG2E)).astype(BF16)
    ka_ref[...] = nat[:, 256:512].astype(BF16)
    va_ref[...] = nat[:, 512:768].astype(BF16)
    kb_ref[...] = nat[:, 768:1024].astype(BF16)
    kc_ref[...] = nat[:, 1024:1280].astype(BF16)
    cq = nat[:, 1280:1536]
    ckv = nat[:, 1536:1664]
    kr2 = nat[:, 1664:1792]

    tr = _nt(wt_ref[...], h)
    _store_t_blocks(qbt_ref, tr[0:256] * (DIFF_QK ** -0.5 * LOG2E), FULL_T)
    _store_t_blocks(vbt_ref, tr[256:512], FULL_T)
    _store_t_blocks(qct_ref, tr[512:768] * (HEAD_DIM ** -0.5 * LOG2E), BAND_T)
    _store_t_blocks(vct_ref, tr[768:1024], BAND_T)

    cqn = _rms(cq, gq_ref[...]).astype(BF16)
    qd_t = _nt(wuqt_ref[...], cqn)
    cos_t = cost_ref[...]
    sin_t = sint_ref[...]
    pieces = []
    for hd in range(N_HEADS):
        b0 = hd * PAIR
        x1 = qd_t[b0 + 64:b0 + 80]
        x2 = qd_t[b0 + 80:b0 + 96]
        pieces += [qd_t[b0:b0 + 64], x1 * cos_t - x2 * sin_t, x1 * sin_t + x2 * cos_t,
                   qd_t[b0 + 96:b0 + 128]]
    qd_t = jnp.concatenate(pieces, axis=0) * (MLA_QK ** -0.5 * LOG2E)
    _store_t_blocks(qdt_ref, qd_t, FULL_T)

    ckvn = _rms(ckv, gkv_ref[...]).astype(BF16)
    k_nope = jnp.dot(ckvn, wuk_ref[...], preferred_element_type=F32)
    t = kr2 * csn_ref[...]
    roped = t + pltpu.roll(t, 96, 1)
    kd = k_nope + jnp.dot(roped.astype(BF16), place_ref[...], preferred_element_type=F32)
    kd_ref[...] = kd.astype(BF16)
    _store_t_blocks(vdt_ref, _nt(wuvt_ref[...], ckvn), FULL_T)


def _proj_call(x, g, p, rope):
    B, S, D = x.shape
    ts = TOK_TILE
    nt = S // ts
    nf, nb = ts // FULL_T, ts // BAND_T

    def full(a):
        return pl.BlockSpec(a.shape, lambda b, t: (0,) * a.ndim)

    nat_spec = lambda w: pl.BlockSpec((None, ts, w), lambda b, t: (b, t, 0))
    tb_spec = lambda n, r, blk: pl.BlockSpec((None, n, r, blk), lambda b, t: (b, t, 0, 0))
    nat_shape = lambda w: jax.ShapeDtypeStruct((B, S, w), BF16)
    tb_shape = lambda r, blk: jax.ShapeDtypeStruct((B, S // blk, r, blk), BF16)

    weights = [g, p['w_nat'], p['w_t'], p['g_q'], p['w_uq_t'], p['g_kv'], p['w_uk'], p['w_uv_t'], p['place']]
    in_specs = ([nat_spec(D)] + [full(a) for a in weights] +
                [pl.BlockSpec((ts, PAIR), lambda b, t: (t, 0)),
                 pl.BlockSpec((None, MLA_HALF, ts), lambda b, t: (t, 0, 0)),
                 pl.BlockSpec((None, MLA_HALF, ts), lambda b, t: (t, 0, 0))])
    out_shape = ([nat_shape(GROUP)] * 5 + [nat_shape(2 * GROUP)] +
                 [tb_shape(GROUP, FULL_T)] * 2 + [tb_shape(GROUP, BAND_T)] * 2 +
                 [tb_shape(2 * GROUP, FULL_T), tb_shape(GROUP, FULL_T)])
    out_specs = ([nat_spec(GROUP)] * 5 + [nat_spec(2 * GROUP)] +
                 [tb_spec(nf, GROUP, FULL_T)] * 2 + [tb_spec(nb, GROUP, BAND_T)] * 2 +
                 [tb_spec(nf, 2 * GROUP, FULL_T), tb_spec(nf, GROUP, FULL_T)])
    return pl.pallas_call(
        _proj_kernel,
        grid=(B, nt),
        in_specs=in_specs,
        out_specs=out_specs,
        out_shape=out_shape,
        compiler_params=pltpu.CompilerParams(
            dimension_semantics=("parallel", "parallel"), vmem_limit_bytes=V7X_VMEM_LIMIT),
        name="proj",
    )(x, *weights, rope['csn'], rope['cos_t'], rope['sin_t'])


def _na_kernel(q_ref, kp_ref, kc_ref, kn_ref, vp_ref, vc_ref, vn_ref, bias_ref, o_ref, kk_ref, vv_ref,
               *, n_rows):
    g = pl.program_id(1)
    blk = NA_GROUP * GRID_W
    win = NA_ROWS * GRID_W
    for c, (kr, vr) in enumerate(((kp_ref, vp_ref), (kc_ref, vc_ref), (kn_ref, vn_ref))):
        kk_ref[c * blk:(c + 1) * blk, :] = kr[...]
        vv_ref[c * blk:(c + 1) * blk, :] = vr[...]
    lane = lax.broadcasted_iota(jnp.int32, (GRID_W, PAIR), 1)

    first_head = lane < HEAD_DIM

    def rows_body(jj, carry):
        v_pairs, scores, probs = [], [], []
        for dj in range(NA_ROWS_PER_ITER):
            j = jj * NA_ROWS_PER_ITER + dj
            r = g * NA_GROUP + j
            r0 = jnp.clip(r - NA_ROWS // 2, 0, n_rows - NA_ROWS)
            start = pl.multiple_of((r0 - (g - 1) * NA_GROUP) * GRID_W, GRID_W)
            q_row = q_ref[pl.ds(pl.multiple_of(j * GRID_W, GRID_W), GRID_W), :]
            k_win = kk_ref[pl.ds(start, win), :]
            v_win = vv_ref[pl.ds(start, win), :]
            for pr in range(2):
                q2 = q_row[:, pr * PAIR:(pr + 1) * PAIR]
                zero = jnp.zeros_like(q2)
                qz = jnp.concatenate([jnp.where(first_head, q2, zero), jnp.where(first_head, zero, q2)], axis=0)
                bias = jnp.concatenate([bias_ref[r - r0, 2 * pr], bias_ref[r - r0, 2 * pr + 1]], axis=0)
                scores.append(_nt(qz, k_win[:, pr * PAIR:(pr + 1) * PAIR]) + bias)
                v_pairs.append(v_win[:, pr * PAIR:(pr + 1) * PAIR])
        for s in scores:
            e = jnp.exp2(s - jnp.max(s, axis=-1, keepdims=True))
            probs.append((e.astype(BF16), jnp.sum(e, axis=-1, keepdims=True)))
        outs = []
        for (e, den), v2 in zip(probs, v_pairs):
            o2 = jnp.dot(e, v2, preferred_element_type=F32) / den
            outs.append(jnp.where(first_head, o2[:GRID_W], o2[GRID_W:]))
        for dj in range(NA_ROWS_PER_ITER):
            qs = pl.multiple_of((jj * NA_ROWS_PER_ITER + dj) * GRID_W, GRID_W)
            o_ref[pl.ds(qs, GRID_W), :] = jnp.concatenate(outs[2 * dj:2 * dj + 2], axis=1).astype(o_ref.dtype)
        return carry

    lax.fori_loop(0, NA_GROUP // NA_ROWS_PER_ITER, rows_body, 0)


def _na_call(q, k, v, bias):
    B, S, _ = q.shape
    n_rows = S // GRID_W
    blk = NA_GROUP * GRID_W
    ng = S // blk
    cur = pl.BlockSpec((None, blk, GROUP), lambda b, g: (b, g, 0))
    prv = pl.BlockSpec((None, blk, GROUP), lambda b, g: (b, jnp.maximum(g - 1, 0), 0))
    nxt = pl.BlockSpec((None, blk, GROUP), lambda b, g: (b, jnp.minimum(g + 1, ng - 1), 0))
    return pl.pallas_call(
        functools.partial(_na_kernel, n_rows=n_rows),
        grid=(B, ng),
        in_specs=[cur, prv, cur, nxt, prv, cur, nxt,
                  pl.BlockSpec(bias.shape, lambda b, g: (0, 0, 0, 0))],
        out_specs=cur,
        out_shape=jax.ShapeDtypeStruct((B, S, GROUP), BF16),
        scratch_shapes=[pltpu.VMEM((3 * blk, GROUP), BF16), pltpu.VMEM((3 * blk, GROUP), BF16)],
        compiler_params=pltpu.CompilerParams(
            dimension_semantics=("parallel", "parallel"), vmem_limit_bytes=V7X_VMEM_LIMIT),
        name="na_attn",
    )(q, k, k, k, v, v, v, bias)


def _diff_kernel(c_ref, g_ref, qt_ref, k_ref, vt_ref, bias_ref, o_ref, *scratch, nk, lam_scale):
    i0 = pl.program_id(2) * DIFF_Q_BLOCKS
    tk = vt_ref.shape[2]
    tq = qt_ref.shape[2]
    row = lax.broadcasted_iota(jnp.int32, (PAIR, tq), 0)
    lam = c_ref[0:1, 0:1]
    j_lo = jnp.maximum(i0 - 2, 0)
    j_hi = jnp.minimum(i0 + DIFF_Q_BLOCKS + 2, nk)
    ranges = ((0, j_lo), (j_lo, j_hi), (j_hi, nk))

    def load_k(j):
        return k_ref[pl.ds(pl.multiple_of(j * tk, tk), tk), :]

    streams = []
    for qi in range(DIFF_Q_BLOCKS):
        q_t = qt_ref[qi]
        for hh in range(2):

            def load_vt(j, hh=hh):
                return vt_ref[j, hh * HEAD_DIM:(hh + 1) * HEAD_DIM, :]

            def near_bias(j, hh=hh, qi=qi):
                e0 = 2 * (j - i0 - qi) + 4

                def tile(d):
                    return bias_ref[jnp.clip(e0 + d, 0, 8), hh]
                t0 = tile(0)
                return jnp.concatenate(
                    [jnp.concatenate([t0, tile(-1)], axis=1),
                     jnp.concatenate([tile(1), t0], axis=1)], axis=0)

            for c in range(2):
                r_lo = hh * HEAD_DIM + c * DIFF_QK
                qz = jnp.where((row >= r_lo) & (row < r_lo + DIFF_QK), q_t, jnp.zeros_like(q_t))
                streams.append(dict(qz=qz, load_k=load_k, load_vt=load_vt,
                                    shifts=(c_ref[1 + hh:2 + hh, 0:1], 0.0, c_ref[3 + hh:4 + hh, 0:1]),
                                    bias_fns=(None, near_bias, None), first_bias=near_bias))
    maps = _flash_pipelined(streams, ranges, _flash_bufs(scratch))
    for qi in range(DIFF_Q_BLOCKS):
        heads = []
        for hh in range(2):
            k0 = 4 * qi + 2 * hh
            o = maps[k0] - lam * maps[k0 + 1]
            y = o * lax.rsqrt(jnp.mean(o * o, axis=0, keepdims=True) + EPS) * g_ref[:, 0:1]
            heads.append(y * lam_scale)
        o_ref[qi * tq:(qi + 1) * tq, :] = jnp.concatenate(heads, axis=0).T.astype(o_ref.dtype)


def _resident(block_shape, index_map):
    return pl.BlockSpec(block_shape, index_map, pipeline_mode=pl.Buffered(1))


def _diff_call(consts, g_col, q_t, k, v_t, bias, lam_scale):
    B, nq, _, tq = q_t.shape
    S = k.shape[1]
    nk, _, tk = v_t.shape[1:]
    nqb = DIFF_Q_BLOCKS
    return pl.pallas_call(
        functools.partial(_diff_kernel, nk=nk, lam_scale=lam_scale),
        grid=(B, 2, nq // nqb),
        in_specs=[
            pl.BlockSpec((None, 8, 128), lambda b, p, i: (p, 0, 0)),
            pl.BlockSpec(g_col.shape, lambda b, p, i: (0, 0)),
            pl.BlockSpec((None, nqb, PAIR, tq), lambda b, p, i: (b, i, p, 0)),
            _resident((None, S, PAIR), lambda b, p, i: (b, 0, p)),
            _resident((None, nk, PAIR, tk), lambda b, p, i: (b, 0, p, 0)),
            _resident((9, 2, BAND_T, BAND_T), lambda b, p, i: (0, p, 0, 0)),
        ],
        out_specs=pl.BlockSpec((None, nqb * tq, PAIR), lambda b, p, i: (b, i, p)),
        out_shape=jax.ShapeDtypeStruct((B, S, GROUP), BF16),
        scratch_shapes=_flash_scratch(4 * nqb, tk, tq),
        compiler_params=pltpu.CompilerParams(
            dimension_semantics=("parallel", "parallel", "parallel"), vmem_limit_bytes=V7X_VMEM_LIMIT),
        name="diff_attn",
    )(consts, g_col, q_t, k, v_t, bias)


def _dil_kernel(qt_ref, k_ref, vt_ref, bias_ref, o_ref, *scratch, nk):
    i = pl.program_id(1)
    t = BAND_T
    q_t = jnp.concatenate([qt_ref[a] for a in range(DIL_Q_TILES)], axis=1)
    row = lax.broadcasted_iota(jnp.int32, (PAIR, q_t.shape[1]), 0)
    j0 = i * DIL_Q_TILES
    j_lo = jnp.maximum(j0 - DIL_BAND, 0)
    j_hi = jnp.minimum(j0 + DIL_Q_TILES + DIL_BAND, nk)
    streams = []
    for head in range(N_HEADS):
        pr, hh = divmod(head, 2)

        def load_k(j, pr=pr):
            return k_ref[pl.ds(pl.multiple_of(j * t, t), t), pr * PAIR:(pr + 1) * PAIR]

        def load_vt(j, head=head):
            return vt_ref[j, head * HEAD_DIM:(head + 1) * HEAD_DIM, :]

        def bias(j, head=head):
            e0 = j - j0 + DIL_BAND + DIL_Q_TILES - 1
            return jnp.concatenate([bias_ref[e0 - a, head] for a in range(DIL_Q_TILES)], axis=1)

        q_pair = q_t[pr * PAIR:(pr + 1) * PAIR]
        qz = jnp.where((row // HEAD_DIM) == hh, q_pair, jnp.zeros_like(q_pair))
        streams.append(dict(qz=qz, load_k=load_k, load_vt=load_vt, shifts=(0.0,), bias_fns=(bias,)))
    heads = _flash_pipelined(streams, ((j_lo, j_hi),), _flash_bufs(scratch))
    o_ref[...] = jnp.concatenate(heads, axis=0).T.astype(o_ref.dtype)


def _dil_call(q_t, k, v_t, bias):
    B, nq, _, t = q_t.shape
    S = k.shape[1]
    nk = v_t.shape[1]
    nqt = DIL_Q_TILES
    return pl.pallas_call(
        functools.partial(_dil_kernel, nk=nk),
        grid=(B, nq // nqt),
        in_specs=[
            pl.BlockSpec((None, nqt, GROUP, t), lambda b, i: (b, i, 0, 0)),
            _resident((None, S, GROUP), lambda b, i: (b, 0, 0)),
            _resident((None, nk, GROUP, t), lambda b, i: (b, 0, 0, 0)),
            _resident(bias.shape, lambda b, i: (0, 0, 0, 0)),
        ],
        out_specs=pl.BlockSpec((None, nqt * t, GROUP), lambda b, i: (b, i, 0)),
        out_shape=jax.ShapeDtypeStruct((B, S, GROUP), BF16),
        scratch_shapes=_flash_scratch(N_HEADS, t, nqt * t),
        compiler_params=pltpu.CompilerParams(
            dimension_semantics=("parallel", "parallel"), vmem_limit_bytes=V7X_VMEM_LIMIT),
        name="dil_attn",
    )(q_t, k, v_t, bias)


def _mla_kernel(qt_ref, k_ref, vt_ref, o_ref, *scratch, nk):
    tk = vt_ref.shape[2]
    tq = qt_ref.shape[2]
    streams = []
    for qi in range(MLA_Q_BLOCKS):
        for hh in range(2):

            def load_k(j, hh=hh):
                return k_ref[pl.ds(pl.multiple_of(j * tk, tk), tk), hh * PAIR:(hh + 1) * PAIR]

            def load_vt(j, hh=hh):
                return vt_ref[j, hh * HEAD_DIM:(hh + 1) * HEAD_DIM, :]

            qz = qt_ref[qi, hh * PAIR:(hh + 1) * PAIR, :]
            streams.append(dict(qz=qz, load_k=load_k, load_vt=load_vt, shifts=(0.0,), bias_fns=(None,)))
    outs = _flash_pipelined(streams, ((0, nk),), _flash_bufs(scratch))
    for qi in range(MLA_Q_BLOCKS):
        o_t = jnp.concatenate(outs[2 * qi:2 * qi + 2], axis=0)
        o_ref[qi * tq:(qi + 1) * tq, :] = o_t.T.astype(o_ref.dtype)


def _mla_call(q_t, k, v_t):
    B, nq, _, tq = q_t.shape
    S = k.shape[1]
    nk, _, tk = v_t.shape[1:]
    nqb = MLA_Q_BLOCKS
    return pl.pallas_call(
        functools.partial(_mla_kernel, nk=nk),
        grid=(B, 2, nq // nqb),
        in_specs=[
            pl.BlockSpec((None, nqb, 2 * PAIR, tq), lambda b, p, i: (b, i, p, 0)),
            _resident((None, S, 2 * PAIR), lambda b, p, i: (b, 0, p)),
            _resident((None, nk, PAIR, tk), lambda b, p, i: (b, 0, p, 0)),
        ],
        out_specs=pl.BlockSpec((None, nqb * tq, PAIR), lambda b, p, i: (b, i, p)),
        out_shape=jax.ShapeDtypeStruct((B, S, GROUP), BF16),
        scratch_shapes=_flash_scratch(2 * nqb, tk, tq),
        compiler_params=pltpu.CompilerParams(
            dimension_semantics=("parallel", "parallel", "parallel"), vmem_limit_bytes=V7X_VMEM_LIMIT),
        name="mla_attn",
    )(q_t, k, v_t)


def _post_kernel(x_ref, oa_ref, ob_ref, oc_ref, od_ref, wo_ref, g_ref, wg_ref, wu_ref, wd_ref, gf_ref,
                 out_ref, acc_ref, h_ref, *, final):
    f = pl.program_id(1)

    @pl.when(f == 0)
    def _():
        x1 = x_ref[...]
        for gi, o_ref in enumerate((oa_ref, ob_ref, oc_ref, od_ref)):
            x1 = x1 + jnp.dot(o_ref[...], wo_ref[gi], preferred_element_type=F32)
        acc_ref[...] = x1
        h_ref[...] = _rms(x1, g_ref[...]).astype(BF16)

    h = h_ref[...]
    gate = jnp.dot(h, wg_ref[...], preferred_element_type=F32)
    up = jnp.dot(h, wu_ref[...], preferred_element_type=F32)
    act = (gate / (1.0 + jnp.exp(-gate))) * up
    acc_ref[...] += jnp.dot(act.astype(BF16), wd_ref[...], preferred_element_type=F32)

    @pl.when(f == pl.num_programs(1) - 1)
    def _():
        y = acc_ref[...]
        out_ref[...] = _rms(y, gf_ref[...]) if final else y


def _post_call(x, o_a, o_b, o_c, o_d, p, g_final, final):
    N, D = x.shape
    ts = TOK_TILE
    d_ff = p['w_gate'].shape[1]
    fc = d_ff // FF_SPLIT
    tok = lambda w: pl.BlockSpec((ts, w), lambda t, f: (t, 0))
    return pl.pallas_call(
        functools.partial(_post_kernel, final=final),
        grid=(N // ts, FF_SPLIT),
        in_specs=[tok(D), tok(GROUP), tok(GROUP), tok(GROUP), tok(GROUP),
                  pl.BlockSpec((4, GROUP, D), lambda t, f: (0, 0, 0)),
                  pl.BlockSpec((1, D), lambda t, f: (0, 0)),
                  pl.BlockSpec((D, fc), lambda t, f: (0, f)),
                  pl.BlockSpec((D, fc), lambda t, f: (0, f)),
                  pl.BlockSpec((fc, D), lambda t, f: (f, 0)),
                  pl.BlockSpec((1, D), lambda t, f: (0, 0))],
        out_specs=tok(D),
        out_shape=jax.ShapeDtypeStruct((N, D), F32),
        scratch_shapes=[pltpu.VMEM((ts, D), F32), pltpu.VMEM((ts, D), BF16)],
        compiler_params=pltpu.CompilerParams(
            dimension_semantics=("parallel", "arbitrary"), vmem_limit_bytes=V7X_VMEM_LIMIT),
        name="post_ffn",
    )(x, o_a, o_b, o_c, o_d, p['w_o'], p['g_ffn'], p['w_gate'], p['w_up'], p['w_down'], g_final)


def _t5_bucket(rel):
    nb = T5_BUCKETS // 2
    max_exact = nb // 2
    side = jnp.where(rel > 0, nb, 0)
    n = jnp.abs(rel)
    large = max_exact + (jnp.log(jnp.maximum(n, 1).astype(F32) / max_exact)
                         / math.log(T5_MAX_DIST / max_exact) * (nb - max_exact)).astype(jnp.int32)
    large = jnp.minimum(large, nb - 1)
    return side + jnp.where(n < max_exact, n, large)


def _lookup(table, idx):
    onehot = idx[..., None, None] == jnp.arange(table.shape[0])[:, None]
    return jnp.sum(jnp.where(onehot, table.astype(F32), 0.0), axis=-2)


def _toeplitz(w, t):
    lead = w.shape[:-1]
    w_pad = jnp.concatenate([w, jnp.zeros(lead + (1,), w.dtype)], axis=-1)
    m = jnp.tile(w_pad, t)[..., :t * (2 * t - 1)].reshape(lead + (t, 2 * t - 1))
    return m[..., t - 1:]


def _toeplitz_tiles(vec_fn, t, n_side):
    reach = n_side * t + t - 1
    w = jnp.moveaxis(vec_fn(jnp.arange(-reach, reach + 1)), -1, 0)
    tiles = [_toeplitz(w[:, (e + n_side) * t:(e + n_side + 2) * t - 1], t) for e in range(-n_side, n_side + 1)]
    return jnp.swapaxes(jnp.stack(tiles), -1, -2)


def _diff_bias(t5_diff):
    def vec(rel):
        return _lookup(t5_diff, _t5_bucket(rel)) * LOG2E
    tiles = _toeplitz_tiles(vec, BAND_T, 4)
    far = vec(jnp.array([-2 * T5_MAX_DIST, 2 * T5_MAX_DIST]))
    return tiles, far


def _dil_bias(t5_dil):
    def vec(rel):
        n = jnp.abs(rel)
        mult = jnp.zeros(rel.shape, F32)
        for window, dil in DIL_PATTERNS:
            mult = mult + ((rel % dil == 0) & (n <= window // 2)).astype(F32)
        b = _lookup(t5_dil, _t5_bucket(rel)) + jnp.log(jnp.maximum(mult, 1.0))[..., None]
        return jnp.where((mult > 0)[..., None], b * LOG2E, NEG)
    return _toeplitz_tiles(vec, BAND_T, DIL_BAND + DIL_Q_TILES - 1)


def _na_bias(rpb):
    d = jnp.arange(NA_ROWS)[:, None]
    ridx = jnp.arange(NA_ROWS)[None, :] - d + NA_ROWS - 1
    qc = jnp.arange(GRID_W)[:, None]
    kc = jnp.arange(GRID_W)[None, :]
    c0 = jnp.clip(qc - NA_COLS // 2, 0, GRID_W - NA_COLS)
    valid = (kc >= c0) & (kc < c0 + NA_COLS)
    rows = rpb.astype(F32)[:, ridx]
    side = GRID_W - NA_COLS
    vals = _toeplitz(jnp.pad(rows, ((0, 0),) * 3 + ((side, side),)), GRID_W)
    vals = jnp.where(valid, vals * LOG2E, NEG)
    vals = vals.transpose(1, 0, 3, 2, 4)
    return vals.reshape(NA_ROWS, N_HEADS, GRID_W, NA_ROWS * GRID_W)


def _rope_tables(S):
    inv_freq = ROPE_THETA ** (-jnp.arange(MLA_HALF, dtype=F32) / MLA_HALF)
    ang = jnp.arange(S).astype(F32)[:, None] * inv_freq[None, :]
    cos, sin = jnp.cos(ang), jnp.sin(ang)
    csn = jnp.concatenate([cos, cos, -sin, sin, jnp.zeros((S, PAIR - 4 * MLA_HALF), F32)], axis=1)
    nt = S // TOK_TILE
    to_t = lambda a: a.reshape(nt, TOK_TILE, MLA_HALF).transpose(0, 2, 1)
    return {'csn': csn, 'cos_t': to_t(cos), 'sin_t': to_t(sin)}


def _layer_weights(w_in, w_uq, w_ukv, w_o, w_gate, w_up, w_down, g_q, g_kv, g_ffn):
    G = GROUP
    col = lambda i: w_in[:, i * G:(i + 1) * G]
    cq, ckv, kr = w_in[:, 9 * G:10 * G], w_in[:, 10 * G:10 * G + 128], w_in[:, 10 * G + 128:]
    kr_swap = jnp.concatenate([kr[:, MLA_HALF:], kr[:, :MLA_HALF]], axis=1)
    kr2 = jnp.concatenate([kr, kr_swap, jnp.zeros((w_in.shape[0], PAIR - 2 * MLA_ROPE), w_in.dtype)], axis=1)
    w_nat = jnp.concatenate([col(0), col(1), col(2), col(4), col(7), cq, ckv, kr2], axis=1)
    w_t = jnp.concatenate([col(3), col(5), col(6), col(8)], axis=1).T
    uq = w_uq.reshape(w_uq.shape[0], N_HEADS, MLA_QK)
    uq = jnp.pad(uq, ((0, 0), (0, 0), (0, PAIR - MLA_QK))).reshape(w_uq.shape[0], N_HEADS * PAIR)
    ukv = w_ukv.reshape(w_ukv.shape[0], N_HEADS, 2 * HEAD_DIM)
    uk = ukv.at[:, :, HEAD_DIM:].set(0.0).reshape(w_ukv.shape[0], N_HEADS * PAIR)
    uv = ukv[:, :, HEAD_DIM:].reshape(w_ukv.shape[0], GROUP)
    place = jnp.zeros((PAIR, N_HEADS * PAIR), F32)
    r = jnp.arange(MLA_ROPE)
    for hd in range(N_HEADS):
        place = place.at[r, hd * PAIR + HEAD_DIM + r].set(1.0)
    b = lambda a: a.astype(BF16)
    row = lambda a: a.astype(F32)[None, :]
    return {
        'w_nat': b(w_nat), 'w_t': b(w_t), 'w_uq_t': b(uq.T), 'w_uk': b(uk), 'w_uv_t': b(uv.T),
        'place': b(place), 'g_q': row(g_q), 'g_kv': row(g_kv), 'g_ffn': row(g_ffn),
        'w_o': b(w_o.reshape(4, GROUP, w_o.shape[1])), 'w_gate': b(w_gate), 'w_up': b(w_up),
        'w_down': b(w_down),
    }


def kernel(x, attn_norm, w_in, na_rpb, diff_lambda, diff_subln, mla_q_norm, w_uq, mla_kv_norm, w_ukv,
           t5_table, w_o, ffn_norm, w_gate, w_up, w_down, final_norm):
    B, S, D = x.shape
    depth = w_in.shape[0]
    rope = _rope_tables(S)
    diff_tiles, diff_far = _diff_bias(t5_table[:, :N_HEADS])
    dil_tiles = _dil_bias(t5_table[:, N_HEADS:])
    g_final = final_norm.astype(F32)[None, :]

    for l in range(depth):
        p = _layer_weights(w_in[l], w_uq[l], w_ukv[l], w_o[l], w_gate[l], w_up[l], w_down[l],
                           mla_q_norm[l], mla_kv_norm[l], ffn_norm[l])
        (qa, ka, va, kb, kc, kd, qb_t, vb_t, qc_t, vc_t, qd_t, vd_t) = _proj_call(
            x, attn_norm[l].astype(F32)[None, :], p, rope)

        o_a = _na_call(qa, ka, va, _na_bias(na_rpb[l]))

        lam_init = 0.8 - 0.6 * math.exp(-0.3 * l)
        lq1, lk1, lq2, lk2 = [diff_lambda[l, j].astype(F32) for j in range(4)]
        lam = jnp.exp(jnp.sum(lq1 * lk1)) - jnp.exp(jnp.sum(lq2 * lk2)) + lam_init
        far = diff_far.reshape(2, 2, 2).transpose(1, 0, 2).reshape(2, 4)
        consts = jnp.concatenate([jnp.broadcast_to(lam, (2, 1)), far, jnp.zeros((2, 3), F32)], axis=1)
        consts = jnp.broadcast_to(consts[:, :, None], (2, 8, 128))
        g_col = jnp.broadcast_to(diff_subln[l].astype(F32)[:, None], (HEAD_DIM, 128))
        o_b = _diff_call(consts, g_col, qb_t, kb, vb_t, diff_tiles, 1.0 - lam_init)

        o_c = _dil_call(qc_t, kc, vc_t, dil_tiles)
        o_d = _mla_call(qd_t, kd, vd_t)

        flat = lambda a: a.reshape(B * S, a.shape[-1])
        x = _post_call(flat(x), flat(o_a), flat(o_b), flat(o_c), flat(o_d), p, g_final,
                       final=(l == depth - 1)).reshape(B, S, D)
    return x
```

```python
import functools
import math

import jax
import jax.numpy as jnp
from jax import lax
from jax.experimental import pallas as pl
from jax.experimental.pallas import tpu as pltpu

F32 = jnp.float32
BF16 = jnp.bfloat16

N_HEADS = 4
HEAD_DIM = 64
GROUP = N_HEADS * HEAD_DIM
PAIR = 2 * HEAD_DIM
GRID_W = 64
NA_ROWS = 8
NA_COLS = 16
DIFF_QK = HEAD_DIM // 2
DIL_PATTERNS = ((128, 1), (512, 4), (2048, 16))
MLA_ROPE = HEAD_DIM // 2
MLA_HALF = MLA_ROPE // 2
MLA_QK = HEAD_DIM + MLA_ROPE
ROPE_THETA = 10000.0
T5_BUCKETS = 32
T5_MAX_DIST = 1024
EPS = 1e-6
LOG2E = 1.4426950408889634
NEG = -1e30
SUM_ROWS = 16
EXP_CHUNK = 64

V7X_VMEM_LIMIT = 56 * 1024 * 1024

TOK_TILE = 512
FULL_T = 512
BAND_T = 256
DIL_BAND = -(-max(w // 2 for w, _ in DIL_PATTERNS) // BAND_T)
DIL_Q_TILES = 2
MLA_Q_BLOCKS = 4
DIFF_Q_BLOCKS = 1
NA_GROUP = 8
NA_ROWS_PER_ITER = 2
FF_SPLIT = 2


def _rms(x, g):
    return x * lax.rsqrt(jnp.mean(x * x, axis=-1, keepdims=True) + EPS) * g


def _nt(a, b):
    return lax.dot_general(a, b, (((1,), (1,)), ((), ())), preferred_element_type=F32)


def _zero_of(x):
    u = lax.bitcast_convert_type(x, jnp.uint32)
    return lax.bitcast_convert_type((u >> 16) >> 16, F32)


def _flash_pipelined(streams, ranges, bufs):
    tk, tq = bufs[0][0].shape
    first = ranges[0][0]
    last = ranges[-1][1] - 1
    ones = jnp.ones((SUM_ROWS, tk), BF16)

    def scores(st, j, bias_fn, s_ref):
        s = jnp.dot(st['load_k'](j), st['qz'], preferred_element_type=F32)
        if bias_fn is not None:
            s = s + bias_fn(j)
        s_ref[...] = s
        return jnp.max(s, axis=0, keepdims=True)

    def accumulate(st, buf, j, par):
        acc_ref, stat_ref = buf[4], buf[5]
        v_aug = jnp.concatenate([st['load_vt'](j), ones], axis=0)
        acc_ref[...] = (stat_ref[_ALPHA + par:_ALPHA + par + 1, :] * acc_ref[...]
                        + jnp.dot(v_aug, buf[2 + par][...], preferred_element_type=F32))

    def softmax_update(buf, par, s_new):
        stat_ref = buf[5]
        m = stat_ref[_MAX:_MAX + 1, :]
        c = stat_ref[_CONST:_CONST + 1, :]
        m_new = jnp.maximum(m, stat_ref[_CMAX:_CMAX + 1, :] + c)
        ref_rows = jnp.broadcast_to(m_new - c, (8, tq))
        for lo_row in range(0, tk, EXP_CHUNK):
            rows = slice(lo_row, lo_row + EXP_CHUNK)
            ref = ref_rows if s_new is None else ref_rows + _zero_of(s_new[lo_row:lo_row + 8, :])
            x = buf[1 - par][rows, :].reshape(EXP_CHUNK // 8, 8, tq) - ref[None]
            buf[3 - par][rows, :] = jnp.exp2(x.reshape(EXP_CHUNK, tq)).astype(BF16)
        stat_ref[_ALPHA + 1 - par:_ALPHA + 2 - par, :] = jnp.exp2(m - m_new)
        stat_ref[_MAX:_MAX + 1, :] = m_new

    def step(par, r, t):
        for st, buf in zip(streams, bufs):
            stat_ref = buf[5]
            s_new = jnp.dot(st['load_k'](t), st['qz'], preferred_element_type=F32)
            if st['bias_fns'][r] is not None:
                s_new = s_new + st['bias_fns'][r](t)
            buf[par][...] = s_new
            cmax_t = jnp.max(s_new, axis=0, keepdims=True)
            softmax_update(buf, par, s_new)
            stat_ref[_CMAX:_CMAX + 1, :] = cmax_t
            stat_ref[_CONST:_CONST + 1, :] = jnp.zeros((1, tq), F32) + st['shifts'][r]
            accumulate(st, buf, jnp.maximum(t - 2, first), par)

    def drain(par):
        for st, buf in zip(streams, bufs):
            softmax_update(buf, par, None)
            accumulate(st, buf, jnp.maximum(last - 1, first), par)
            accumulate(st, buf, last, 1 - par)

    for st, buf in zip(streams, bufs):
        acc_ref, stat_ref = buf[4], buf[5]
        cmax = scores(st, first, st.get('first_bias', st['bias_fns'][0]), buf[0])
        buf[3][...] = jnp.zeros(buf[3].shape, buf[3].dtype)
        acc_ref[...] = jnp.zeros_like(acc_ref)
        stat_ref[_ALPHA:_ALPHA + 2, :] = jnp.ones((2, tq), F32)
        stat_ref[_MAX:_MAX + 1, :] = jnp.full((1, tq), NEG, F32)
        stat_ref[_CMAX:_CMAX + 1, :] = cmax
        stat_ref[_CONST:_CONST + 1, :] = (jnp.zeros((1, tq), F32)
                                          + (0.0 if 'first_bias' in st else st['shifts'][0]))

    def by_parity(j, fn):
        par = (j - first) % 2
        if isinstance(par, int):
            fn(par)
        else:
            pl.when(par == 0)(functools.partial(fn, 0))
            pl.when(par == 1)(functools.partial(fn, 1))

    for r, (lo, hi) in enumerate(ranges):

        def body(t, c, r=r):
            by_parity(t, lambda par: step(par, r, t))
            return c

        lax.fori_loop(jnp.maximum(lo, first + 1), hi, body, 0)

    by_parity(last + 1, drain)
    outs = []
    for buf in bufs:
        acc = buf[4][...]
        outs.append(acc[:HEAD_DIM] / acc[HEAD_DIM:HEAD_DIM + 1])
    return outs


_ALPHA, _MAX, _CMAX, _CONST = 0, 2, 3, 4
_BUFS_PER_STREAM = 6


def _flash_scratch(n_streams, tk, tq):
    per_stream = [pltpu.VMEM((tk, tq), F32), pltpu.VMEM((tk, tq), F32),
                  pltpu.VMEM((tk, tq), BF16), pltpu.VMEM((tk, tq), BF16),
                  pltpu.VMEM((HEAD_DIM + SUM_ROWS, tq), F32), pltpu.VMEM((8, tq), F32)]
    return per_stream * n_streams


def _flash_bufs(refs):
    n = _BUFS_PER_STREAM
    return [tuple(refs[n * k:n * k + n]) for k in range(len(refs) // n)]


def _store_t_blocks(ref, val, blk):
    for c in range(val.shape[1] // blk):
        ref[c] = val[:, c * blk:(c + 1) * blk].astype(ref.dtype)


def _proj_kernel(x_ref, g_ref, wnat_ref, wt_ref, gq_ref, wuqt_ref, gkv_ref, wuk_ref, wuvt_ref,
                 place_ref, csn_ref, cost_ref, sint_ref,
                 qa_ref, ka_ref, va_ref, kb_ref, kc_ref, kd_ref,
                 qbt_ref, vbt_ref, qct_ref, vct_ref, qdt_ref, vdt_ref):
    h = _rms(x_ref[...], g_ref[...]).astype(BF16)
    nat = jnp.dot(h, wnat_ref[...], preferred_element_type=F32)
    qa_ref[...] = (nat[:, 0:256] * (HEAD_DIM ** -0.5 * LOG2E)).astype(BF16)
    ka_ref[...] = nat[:, 256:512].astype(BF16)
    va_ref[...] = nat[:, 512:768].astype(BF16)
    kb_ref[...] = nat[:, 768:1024].astype(BF16)
    kc_ref[...] = nat[:, 1024:1280].astype(BF16)
    cq = nat[:, 1280:1536]
    ckv = nat[:, 1536:1664]
    kr2 = nat[:, 1664:1792]

    tr = _nt(wt_ref[...], h)
    _store_t_blocks(qbt_ref, tr[0:256] * (DIFF_QK ** -0.5 * LOG2E), FULL_T)
    _store_t_blocks(vbt_ref, tr[256:512], FULL_T)
    _store_t_blocks(qct_ref, tr[512:768] * (HEAD_DIM ** -0.5 * LOG2E), BAND_T)
    _store_t_blocks(vct_ref, tr[768:1024], BAND_T)

    cqn = _rms(cq, gq_ref[...]).astype(BF16)
    qd_t = _nt(wuqt_ref[...], cqn)
    cos_t = cost_ref[...]
    sin_t = sint_ref[...]
    pieces = []
    for hd in range(N_HEADS):
        b0 = hd * PAIR
        x1 = qd_t[b0 + 64:b0 + 80]
        x2 = qd_t[b0 + 80:b0 + 96]
        pieces += [qd_t[b0:b0 + 64], x1 * cos_t - x2 * sin_t, x1 * sin_t + x2 * cos_t,
                   qd_t[b0 + 96:b0 + 128]]
    qd_t = jnp.concatenate(pieces, axis=0) * (MLA_QK ** -0.5 * LOG2E)
    _store_t_blocks(qdt_ref, qd_t, FULL_T)

    ckvn = _rms(ckv, gkv_ref[...]).astype(BF16)
    k_nope = jnp.dot(ckvn, wuk_ref[...], preferred_element_type=F32)
    t = kr2 * csn_ref[...]
    roped = t + pltpu.roll(t, 96, 1)
    kd = k_nope + jnp.dot(roped.astype(BF16), place_ref[...], preferred_element_type=F32)
    kd_ref[...] = kd.astype(BF16)
    _store_t_blocks(vdt_ref, _nt(wuvt_ref[...], ckvn), FULL_T)


def _proj_call(x, g, p, rope):
    B, S, D = x.shape
    ts = TOK_TILE
    nt = S // ts
    nf, nb = ts // FULL_T, ts // BAND_T

    def full(a):
        return pl.BlockSpec(a.shape, lambda b, t: (0,) * a.ndim)

    nat_spec = lambda w: pl.BlockSpec((None, ts, w), lambda b, t: (b, t, 0))
    tb_spec = lambda n, r, blk: pl.BlockSpec((None, n, r, blk), lambda b, t: (b, t, 0, 0))
    nat_shape = lambda w: jax.ShapeDtypeStruct((B, S, w), BF16)
    tb_shape = lambda r, blk: jax.ShapeDtypeStruct((B, S // blk, r, blk), BF16)

    weights = [g, p['w_nat'], p['w_t'], p['g_q'], p['w_uq_t'], p['g_kv'], p['w_uk'], p['w_uv_t'], p['place']]
    in_specs = ([nat_spec(D)] + [full(a) for a in weights] +
                [pl.BlockSpec((ts, PAIR), lambda b, t: (t, 0)),
                 pl.BlockSpec((None, MLA_HALF, ts), lambda b, t: (t, 0, 0)),
                 pl.BlockSpec((None, MLA_HALF, ts), lambda b, t: (t, 0, 0))])
    out_shape = ([nat_shape(GROUP)] * 5 + [nat_shape(2 * GROUP)] +
                 [tb_shape(GROUP, FULL_T)] * 2 + [tb_shape(GROUP, BAND_T)] * 2 +
                 [tb_shape(2 * GROUP, FULL_T), tb_shape(GROUP, FULL_T)])
    out_specs = ([nat_spec(GROUP)] * 5 + [nat_spec(2 * GROUP)] +
                 [tb_spec(nf, GROUP, FULL_T)] * 2 + [tb_spec(nb, GROUP, BAND_T)] * 2 +
                 [tb_spec(nf, 2 * GROUP, FULL_T), tb_spec(nf, GROUP, FULL_T)])
    return pl.pallas_call(
        _proj_kernel,
        grid=(B, nt),
        in_specs=in_specs,
        out_specs=out_specs,
        out_shape=out_shape,
        compiler_params=pltpu.CompilerParams(
            dimension_semantics=("parallel", "parallel"), vmem_limit_bytes=V7X_VMEM_LIMIT),
        name="proj",
    )(x, *weights, rope['csn'], rope['cos_t'], rope['sin_t'])


def _na_kernel(q_ref, kp_ref, kc_ref, kn_ref, vp_ref, vc_ref, vn_ref, bias_ref, o_ref, kk_ref, vv_ref,
               *, n_rows):
    g = pl.program_id(1)
    blk = NA_GROUP * GRID_W
    win = NA_ROWS * GRID_W
    for c, (kr, vr) in enumerate(((kp_ref, vp_ref), (kc_ref, vc_ref), (kn_ref, vn_ref))):
        kk_ref[c * blk:(c + 1) * blk, :] = kr[...]
        vv_ref[c * blk:(c + 1) * blk, :] = vr[...]
    lane = lax.broadcasted_iota(jnp.int32, (GRID_W, PAIR), 1)

    first_head = lane < HEAD_DIM

    def rows_body(jj, carry):
        v_pairs, scores, probs = [], [], []
        for dj in range(NA_ROWS_PER_ITER):
            j = jj * NA_ROWS_PER_ITER + dj
            r = g * NA_GROUP + j
            r0 = jnp.clip(r - NA_ROWS // 2, 0, n_rows - NA_ROWS)
            start = pl.multiple_of((r0 - (g - 1) * NA_GROUP) * GRID_W, GRID_W)
            q_row = q_ref[pl.ds(pl.multiple_of(j * GRID_W, GRID_W), GRID_W), :]
            k_win = kk_ref[pl.ds(start, win), :]
            v_win = vv_ref[pl.ds(start, win), :]
            for pr in range(2):
                q2 = q_row[:, pr * PAIR:(pr + 1) * PAIR]
                zero = jnp.zeros_like(q2)
                qz = jnp.concatenate([jnp.where(first_head, q2, zero), jnp.where(first_head, zero, q2)], axis=0)
                bias = jnp.concatenate([bias_ref[r - r0, 2 * pr], bias_ref[r - r0, 2 * pr + 1]], axis=0)
                scores.append(_nt(qz, k_win[:, pr * PAIR:(pr + 1) * PAIR]) + bias)
                v_pairs.append(v_win[:, pr * PAIR:(pr + 1) * PAIR])
        for s in scores:
            e = jnp.exp2(s - jnp.max(s, axis=-1, keepdims=True))
            probs.append((e.astype(BF16), jnp.sum(e, axis=-1, keepdims=True)))
        outs = []
        for (e, den), v2 in zip(probs, v_pairs):
            o2 = jnp.dot(e, v2, preferred_element_type=F32) / den
            outs.append(jnp.where(first_head, o2[:GRID_W], o2[GRID_W:]))
        for dj in range(NA_ROWS_PER_ITER):
            qs = pl.multiple_of((jj * NA_ROWS_PER_ITER + dj) * GRID_W, GRID_W)
            o_ref[pl.ds(qs, GRID_W), :] = jnp.concatenate(outs[2 * dj:2 * dj + 2], axis=1).astype(o_ref.dtype)
        return carry

    lax.fori_loop(0, NA_GROUP // NA_ROWS_PER_ITER, rows_body, 0)


def _na_call(q, k, v, bias):
    B, S, _ = q.shape
    n_rows = S // GRID_W
    blk = NA_GROUP * GRID_W
    ng = S // blk
    cur = pl.BlockSpec((None, blk, GROUP), lambda b, g: (b, g, 0))
    prv = pl.BlockSpec((None, blk, GROUP), lambda b, g: (b, jnp.maximum(g - 1, 0), 0))
    nxt = pl.BlockSpec((None, blk, GROUP), lambda b, g: (b, jnp.minimum(g + 1, ng - 1), 0))
    return pl.pallas_call(
        functools.partial(_na_kernel, n_rows=n_rows),
        grid=(B, ng),
        in_specs=[cur, prv, cur, nxt, prv, cur, nxt,
                  pl.BlockSpec(bias.shape, lambda b, g: (0, 0, 0, 0))],
        out_specs=cur,
        out_shape=jax.ShapeDtypeStruct((B, S, GROUP), BF16),
        scratch_shapes=[pltpu.VMEM((3 * blk, GROUP), BF16), pltpu.VMEM((3 * blk, GROUP), BF16)],
        compiler_params=pltpu.CompilerParams(
            dimension_semantics=("parallel", "parallel"), vmem_limit_bytes=V7X_VMEM_LIMIT),
        name="na_attn",
    )(q, k, k, k, v, v, v, bias)


def _diff_kernel(c_ref, g_ref, qt_ref, k_ref, vt_ref, bias_ref, o_ref, *scratch, nk, lam_scale):
    i0 = pl.program_id(2) * DIFF_Q_BLOCKS
    tk = vt_ref.shape[2]
    tq = qt_ref.shape[2]
    row = lax.broadcasted_iota(jnp.int32, (PAIR, tq), 0)
    lam = c_ref[0:1, 0:1]
    j_lo = jnp.maximum(i0 - 2, 0)
    j_hi = jnp.minimum(i0 + DIFF_Q_BLOCKS + 2, nk)
    ranges = ((0, j_lo), (j_lo, j_hi), (j_hi, nk))

    def load_k(j):
        return k_ref[pl.ds(pl.multiple_of(j * tk, tk), tk), :]

    streams = []
    for qi in range(DIFF_Q_BLOCKS):
        q_t = qt_ref[qi]
        for hh in range(2):

            def load_vt(j, hh=hh):
                return vt_ref[j, hh * HEAD_DIM:(hh + 1) * HEAD_DIM, :]

            def near_bias(j, hh=hh, qi=qi):
                e0 = 2 * (j - i0 - qi) + 4

                def tile(d):
                    return bias_ref[jnp.clip(e0 + d, 0, 8), hh]
                t0 = tile(0)
                return jnp.concatenate(
                    [jnp.concatenate([t0, tile(-1)], axis=1),
                     jnp.concatenate([tile(1), t0], axis=1)], axis=0)

            for c in range(2):
                r_lo = hh * HEAD_DIM + c * DIFF_QK
                qz = jnp.where((row >= r_lo) & (row < r_lo + DIFF_QK), q_t, jnp.zeros_like(q_t))
                streams.append(dict(qz=qz, load_k=load_k, load_vt=load_vt,
                                    shifts=(c_ref[1 + hh:2 + hh, 0:1], 0.0, c_ref[3 + hh:4 + hh, 0:1]),
                                    bias_fns=(None, near_bias, None), first_bias=near_bias))
    maps = _flash_pipelined(streams, ranges, _flash_bufs(scratch))
    for qi in range(DIFF_Q_BLOCKS):
        heads = []
        for hh in range(2):
            k0 = 4 * qi + 2 * hh
            o = maps[k0] - lam * maps[k0 + 1]
            y = o * lax.rsqrt(jnp.mean(o * o, axis=0, keepdims=True) + EPS) * g_ref[:, 0:1]
            heads.append(y * lam_scale)
        o_ref[qi * tq:(qi + 1) * tq, :] = jnp.concatenate(heads, axis=0).T.astype(o_ref.dtype)


def _resident(block_shape, index_map):
    return pl.BlockSpec(block_shape, index_map, pipeline_mode=pl.Buffered(1))


def _diff_call(consts, g_col, q_t, k, v_t, bias, lam_scale):
    B, nq, _, tq = q_t.shape
    S = k.shape[1]
    nk, _, tk = v_t.shape[1:]
    nqb = DIFF_Q_BLOCKS
    return pl.pallas_call(
        functools.partial(_diff_kernel, nk=nk, lam_scale=lam_scale),
        grid=(B, 2, nq // nqb),
        in_specs=[
            pl.BlockSpec((None, 8, 128), lambda b, p, i: (p, 0, 0)),
            pl.BlockSpec(g_col.shape, lambda b, p, i: (0, 0)),
            pl.BlockSpec((None, nqb, PAIR, tq), lambda b, p, i: (b, i, p, 0)),
            _resident((None, S, PAIR), lambda b, p, i: (b, 0, p)),
            _resident((None, nk, PAIR, tk), lambda b, p, i: (b, 0, p, 0)),
            _resident((9, 2, BAND_T, BAND_T), lambda b, p, i: (0, p, 0, 0)),
        ],
        out_specs=pl.BlockSpec((None, nqb * tq, PAIR), lambda b, p, i: (b, i, p)),
        out_shape=jax.ShapeDtypeStruct((B, S, GROUP), BF16),
        scratch_shapes=_flash_scratch(4 * nqb, tk, tq),
        compiler_params=pltpu.CompilerParams(
            dimension_semantics=("parallel", "parallel", "parallel"), vmem_limit_bytes=V7X_VMEM_LIMIT),
        name="diff_attn",
    )(consts, g_col, q_t, k, v_t, bias)


def _dil_kernel(qt_ref, k_ref, vt_ref, bias_ref, o_ref, *scratch, nk):
    i = pl.program_id(1)
    t = BAND_T
    q_t = jnp.concatenate([qt_ref[a] for a in range(DIL_Q_TILES)], axis=1)
    row = lax.broadcasted_iota(jnp.int32, (PAIR, q_t.shape[1]), 0)
    j0 = i * DIL_Q_TILES
    j_lo = jnp.maximum(j0 - DIL_BAND, 0)
    j_hi = jnp.minimum(j0 + DIL_Q_TILES + DIL_BAND, nk)
    streams = []
    for head in range(N_HEADS):
        pr, hh = divmod(head, 2)

        def load_k(j, pr=pr):
            return k_ref[pl.ds(pl.multiple_of(j * t, t), t), pr * PAIR:(pr + 1) * PAIR]

        def load_vt(j, head=head):
            return vt_ref[j, head * HEAD_DIM:(head + 1) * HEAD_DIM, :]

        def bias(j, head=head):
            e0 = j - j0 + DIL_BAND + DIL_Q_TILES - 1
            return jnp.concatenate([bias_ref[e0 - a, head] for a in range(DIL_Q_TILES)], axis=1)

        q_pair = q_t[pr * PAIR:(pr + 1) * PAIR]
        qz = jnp.where((row // HEAD_DIM) == hh, q_pair, jnp.zeros_like(q_pair))
        streams.append(dict(qz=qz, load_k=load_k, load_vt=load_vt, shifts=(0.0,), bias_fns=(bias,)))
    heads = _flash_pipelined(streams, ((j_lo, j_hi),), _flash_bufs(scratch))
    o_ref[...] = jnp.concatenate(heads, axis=0).T.astype(o_ref.dtype)


def _dil_call(q_t, k, v_t, bias):
    B, nq, _, t = q_t.shape
    S = k.shape[1]
    nk = v_t.shape[1]
    nqt = DIL_Q_TILES
    return pl.pallas_call(
        functools.partial(_dil_kernel, nk=nk),
        grid=(B, nq // nqt),
        in_specs=[
            pl.BlockSpec((None, nqt, GROUP, t), lambda b, i: (b, i, 0, 0)),
            _resident((None, S, GROUP), lambda b, i: (b, 0, 0)),
            _resident((None, nk, GROUP, t), lambda b, i: (b, 0, 0, 0)),
            _resident(bias.shape, lambda b, i: (0, 0, 0, 0)),
        ],
        out_specs=pl.BlockSpec((None, nqt * t, GROUP), lambda b, i: (b, i, 0)),
        out_shape=jax.ShapeDtypeStruct((B, S, GROUP), BF16),
        scratch_shapes=_flash_scratch(N_HEADS, t, nqt * t),
        compiler_params=pltpu.CompilerParams(
            dimension_semantics=("parallel", "parallel"), vmem_limit_bytes=V7X_VMEM_LIMIT),
        name="dil_attn",
    )(q_t, k, v_t, bias)


def _mla_kernel(qt_ref, k_ref, vt_ref, o_ref, *scratch, nk):
    tk = vt_ref.shape[2]
    tq = qt_ref.shape[2]
    streams = []
    for qi in range(MLA_Q_BLOCKS):
        for hh in range(2):

            def load_k(j, hh=hh):
                return k_ref[pl.ds(pl.multiple_of(j * tk, tk), tk), hh * PAIR:(hh + 1) * PAIR]

            def load_vt(j, hh=hh):
                return vt_ref[j, hh * HEAD_DIM:(hh + 1) * HEAD_DIM, :]

            qz = qt_ref[qi, hh * PAIR:(hh + 1) * PAIR, :]
            streams.append(dict(qz=qz, load_k=load_k, load_vt=load_vt, shifts=(0.0,), bias_fns=(None,)))
    outs = _flash_pipelined(streams, ((0, nk),), _flash_bufs(scratch))
    for qi in range(MLA_Q_BLOCKS):
        o_t = jnp.concatenate(outs[2 * qi:2 * qi + 2], axis=0)
        o_ref[qi * tq:(qi + 1) * tq, :] = o_t.T.astype(o_ref.dtype)


def _mla_call(q_t, k, v_t):
    B, nq, _, tq = q_t.shape
    S = k.shape[1]
    nk, _, tk = v_t.shape[1:]
    nqb = MLA_Q_BLOCKS
    return pl.pallas_call(
        functools.partial(_mla_kernel, nk=nk),
        grid=(B, 2, nq // nqb),
        in_specs=[
            pl.BlockSpec((None, nqb, 2 * PAIR, tq), lambda b, p, i: (b, i, p, 0)),
            _resident((None, S, 2 * PAIR), lambda b, p, i: (b, 0, p)),
            _resident((None, nk, PAIR, tk), lambda b, p, i: (b, 0, p, 0)),
        ],
        out_specs=pl.BlockSpec((None, nqb * tq, PAIR), lambda b, p, i: (b, i, p)),
        out_shape=jax.ShapeDtypeStruct((B, S, GROUP), BF16),
        scratch_shapes=_flash_scratch(2 * nqb, tk, tq),
        compiler_params=pltpu.CompilerParams(
            dimension_semantics=("parallel", "parallel", "parallel"), vmem_limit_bytes=V7X_VMEM_LIMIT),
        name="mla_attn",
    )(q_t, k, v_t)


def _post_kernel(x_ref, oa_ref, ob_ref, oc_ref, od_ref, wo_ref, g_ref, wg_ref, wu_ref, wd_ref, gf_ref,
                 out_ref, acc_ref, h_ref, *, final):
    f = pl.program_id(1)

    @pl.when(f == 0)
    def _():
        x1 = x_ref[...]
        for gi, o_ref in enumerate((oa_ref, ob_ref, oc_ref, od_ref)):
            x1 = x1 + jnp.dot(o_ref[...], wo_ref[gi], preferred_element_type=F32)
        acc_ref[...] = x1
        h_ref[...] = _rms(x1, g_ref[...]).astype(BF16)

    h = h_ref[...]
    gate = jnp.dot(h, wg_ref[...], preferred_element_type=F32)
    up = jnp.dot(h, wu_ref[...], preferred_element_type=F32)
    act = (gate / (1.0 + jnp.exp(-gate))) * up
    acc_ref[...] += jnp.dot(act.astype(BF16), wd_ref[...], preferred_element_type=F32)

    @pl.when(f == pl.num_programs(1) - 1)
    def _():
        y = acc_ref[...]
        out_ref[...] = _rms(y, gf_ref[...]) if final else y


def _post_call(x, o_a, o_b, o_c, o_d, p, g_final, final):
    N, D = x.shape
    ts = TOK_TILE
    d_ff = p['w_gate'].shape[1]
    fc = d_ff // FF_SPLIT
    tok = lambda w: pl.BlockSpec((ts, w), lambda t, f: (t, 0))
    return pl.pallas_call(
        functools.partial(_post_kernel, final=final),
        grid=(N // ts, FF_SPLIT),
        in_specs=[tok(D), tok(GROUP), tok(GROUP), tok(GROUP), tok(GROUP),
                  pl.BlockSpec((4, GROUP, D), lambda t, f: (0, 0, 0)),
                  pl.BlockSpec((1, D), lambda t, f: (0, 0)),
                  pl.BlockSpec((D, fc), lambda t, f: (0, f)),
                  pl.BlockSpec((D, fc), lambda t, f: (0, f)),
                  pl.BlockSpec((fc, D), lambda t, f: (f, 0)),
                  pl.BlockSpec((1, D), lambda t, f: (0, 0))],
        out_specs=tok(D),
        out_shape=jax.ShapeDtypeStruct((N, D), F32),
        scratch_shapes=[pltpu.VMEM((ts, D), F32), pltpu.VMEM((ts, D), BF16)],
        compiler_params=pltpu.CompilerParams(
            dimension_semantics=("parallel", "arbitrary"), vmem_limit_bytes=V7X_VMEM_LIMIT),
        name="post_ffn",
    )(x, o_a, o_b, o_c, o_d, p['w_o'], p['g_ffn'], p['w_gate'], p['w_up'], p['w_down'], g_final)


def _t5_bucket(rel):
    nb = T5_BUCKETS // 2
    max_exact = nb // 2
    side = jnp.where(rel > 0, nb, 0)
    n = jnp.abs(rel)
    large = max_exact + (jnp.log(jnp.maximum(n, 1).astype(F32) / max_exact)
                         / math.log(T5_MAX_DIST / max_exact) * (nb - max_exact)).astype(jnp.int32)
    large = jnp.minimum(large, nb - 1)
    return side + jnp.where(n < max_exact, n, large)


def _lookup(table, idx):
    onehot = idx[..., None, None] == jnp.arange(table.shape[0])[:, None]
    return jnp.sum(jnp.where(onehot, table.astype(F32), 0.0), axis=-2)


def _toeplitz(w, t):
    lead = w.shape[:-1]
    w_pad = jnp.concatenate([w, jnp.zeros(lead + (1,), w.dtype)], axis=-1)
    m = jnp.tile(w_pad, t)[..., :t * (2 * t - 1)].reshape(lead + (t, 2 * t - 1))
    return m[..., t - 1:]


def _toeplitz_tiles(vec_fn, t, n_side):
    reach = n_side * t + t - 1
    w = jnp.moveaxis(vec_fn(jnp.arange(-reach, reach + 1)), -1, 0)
    tiles = [_toeplitz(w[:, (e + n_side) * t:(e + n_side + 2) * t - 1], t) for e in range(-n_side, n_side + 1)]
    return jnp.swapaxes(jnp.stack(tiles), -1, -2)


def _diff_bias(t5_diff):
    def vec(rel):
        return _lookup(t5_diff, _t5_bucket(rel)) * LOG2E
    tiles = _toeplitz_tiles(vec, BAND_T, 4)
    far = vec(jnp.array([-2 * T5_MAX_DIST, 2 * T5_MAX_DIST]))
    return tiles, far


def _dil_bias(t5_dil):
    def vec(rel):
        n = jnp.abs(rel)
        mult = jnp.zeros(rel.shape, F32)
        for window, dil in DIL_PATTERNS:
            mult = mult + ((rel % dil == 0) & (n <= window // 2)).astype(F32)
        b = _lookup(t5_dil, _t5_bucket(rel)) + jnp.log(jnp.maximum(mult, 1.0))[..., None]
        return jnp.where((mult > 0)[..., None], b * LOG2E, NEG)
    return _toeplitz_tiles(vec, BAND_T, DIL_BAND + DIL_Q_TILES - 1)


def _na_bias(rpb):
    d = jnp.arange(NA_ROWS)[:, None]
    ridx = jnp.arange(NA_ROWS)[None, :] - d + NA_ROWS - 1
    qc = jnp.arange(GRID_W)[:, None]
    kc = jnp.arange(GRID_W)[None, :]
    c0 = jnp.clip(qc - NA_COLS // 2, 0, GRID_W - NA_COLS)
    valid = (kc >= c0) & (kc < c0 + NA_COLS)
    rows = rpb.astype(F32)[:, ridx]
    side = GRID_W - NA_COLS
    vals = _toeplitz(jnp.pad(rows, ((0, 0),) * 3 + ((side, side),)), GRID_W)
    vals = jnp.where(valid, vals * LOG2E, NEG)
    vals = vals.transpose(1, 0, 3, 2, 4)
    return vals.reshape(NA_ROWS, N_HEADS, GRID_W, NA_ROWS * GRID_W)


def _rope_tables(S):
    inv_freq = ROPE_THETA ** (-jnp.arange(MLA_HALF, dtype=F32) / MLA_HALF)
    ang = jnp.arange(S).astype(F32)[:, None] * inv_freq[None, :]
    cos, sin = jnp.cos(ang), jnp.sin(ang)
    csn = jnp.concatenate([cos, cos, -sin, sin, jnp.zeros((S, PAIR - 4 * MLA_HALF), F32)], axis=1)
    nt = S // TOK_TILE
    to_t = lambda a: a.reshape(nt, TOK_TILE, MLA_HALF).transpose(0, 2, 1)
    return {'csn': csn, 'cos_t': to_t(cos), 'sin_t': to_t(sin)}


def _layer_weights(w_in, w_uq, w_ukv, w_o, w_gate, w_up, w_down, g_q, g_kv, g_ffn):
    G = GROUP
    col = lambda i: w_in[:, i * G:(i + 1) * G]
    cq, ckv, kr = w_in[:, 9 * G:10 * G], w_in[:, 10 * G:10 * G + 128], w_in[:, 10 * G + 128:]
    kr_swap = jnp.concatenate([kr[:, MLA_HALF:], kr[:, :MLA_HALF]], axis=1)
    kr2 = jnp.concatenate([kr, kr_swap, jnp.zeros((w_in.shape[0], PAIR - 2 * MLA_ROPE), w_in.dtype)], axis=1)
    w_nat = jnp.concatenate([col(0), col(1), col(2), col(4), col(7), cq, ckv, kr2], axis=1)
    w_t = jnp.concatenate([col(3), col(5), col(6), col(8)], axis=1).T
    uq = w_uq.reshape(w_uq.shape[0], N_HEADS, MLA_QK)
    uq = jnp.pad(uq, ((0, 0), (0, 0), (0, PAIR - MLA_QK))).reshape(w_uq.shape[0], N_HEADS * PAIR)
    ukv = w_ukv.reshape(w_ukv.shape[0], N_HEADS, 2 * HEAD_DIM)
    uk = ukv.at[:, :, HEAD_DIM:].set(0.0).reshape(w_ukv.shape[0], N_HEADS * PAIR)
    uv = ukv[:, :, HEAD_DIM:].reshape(w_ukv.shape[0], GROUP)
    place = jnp.zeros((PAIR, N_HEADS * PAIR), F32)
    r = jnp.arange(MLA_ROPE)
    for hd in range(N_HEADS):
        place = place.at[r, hd * PAIR + HEAD_DIM + r].set(1.0)
    b = lambda a: a.astype(BF16)
    row = lambda a: a.astype(F32)[None, :]
    return {
        'w_nat': b(w_nat), 'w_t': b(w_t), 'w_uq_t': b(uq.T), 'w_uk': b(uk), 'w_uv_t': b(uv.T),
        'place': b(place), 'g_q': row(g_q), 'g_kv': row(g_kv), 'g_ffn': row(g_ffn),
        'w_o': b(w_o.reshape(4, GROUP, w_o.shape[1])), 'w_gate': b(w_gate), 'w_up': b(w_up),
        'w_down': b(w_down),
    }


def kernel(x, attn_norm, w_in, na_rpb, diff_lambda, diff_subln, mla_q_norm, w_uq, mla_kv_norm, w_ukv,
           t5_table, w_o, ffn_norm, w_gate, w_up, w_down, final_norm):
    B, S, D = x.shape
    depth = w_in.shape[0]
    rope = _rope_tables(S)
    diff_tiles, diff_far = _diff_bias(t5_table[:, :N_HEADS])
    dil_tiles = _dil_bias(t5_table[:, N_HEADS:])
    g_final = final_norm.astype(F32)[None, :]

    for l in range(depth):
        p = _layer_weights(w_in[l], w_uq[l], w_ukv[l], w_o[l], w_gate[l], w_up[l], w_down[l],
                           mla_q_norm[l], mla_kv_norm[l], ffn_norm[l])
        (qa, ka, va, kb, kc, kd, qb_t, vb_t, qc_t, vc_t, qd_t, vd_t) = _proj_call(
            x, attn_norm[l].astype(F32)[None, :], p, rope)

        o_a = _na_call(qa, ka, va, _na_bias(na_rpb[l]))

        lam_init = 0.8 - 0.6 * math.exp(-0.3 * l)
        lq1, lk1, lq2, lk2 = [diff_lambda[l, j].astype(F32) for j in range(4)]
        lam = jnp.exp(jnp.sum(lq1 * lk1)) - jnp.exp(jnp.sum(lq2 * lk2)) + lam_init
        far = diff_far.reshape(2, 2, 2).transpose(1, 0, 2).reshape(2, 4)
        consts = jnp.concatenate([jnp.broadcast_to(lam, (2, 1)), far, jnp.zeros((2, 3), F32)], axis=1)
        consts = jnp.broadcast_to(consts[:, :, None], (2, 8, 128))
        g_col = jnp.broadcast_to(diff_subln[l].astype(F32)[:, None], (HEAD_DIM, 128))
        o_b = _diff_call(consts, g_col, qb_t, kb, vb_t, diff_tiles, 1.0 - lam_init)

        o_c = _dil_call(qc_t, kc, vc_t, dil_tiles)
        o_d = _mla_call(qd_t, kd, vd_t)

        flat = lambda a: a.reshape(B * S, a.shape[-1])
        x = _post_call(flat(x), flat(o_a), flat(o_b), flat(o_c), flat(o_d), p, g_final,
                       final=(l == depth - 1)).reshape(B, S, D)
    return x
```

```python
import functools
import math

import jax
import jax.numpy as jnp
from jax import lax
from jax.experimental import pallas as pl
from jax.experimental.pallas import tpu as pltpu

F32 = jnp.float32
BF16 = jnp.bfloat16

N_HEADS = 4
HEAD_DIM = 64
GROUP = N_HEADS * HEAD_DIM
PAIR = 2 * HEAD_DIM
GRID_W = 64
NA_ROWS = 8
NA_COLS = 16
DIFF_QK = HEAD_DIM // 2
DIL_PATTERNS = ((128, 1), (512, 4), (2048, 16))
MLA_ROPE = HEAD_DIM // 2
MLA_HALF = MLA_ROPE // 2
MLA_QK = HEAD_DIM + MLA_ROPE
ROPE_THETA = 10000.0
T5_BUCKETS = 32
T5_MAX_DIST = 1024
EPS = 1e-6
LOG2E = 1.4426950408889634
NEG = -1e30
SUM_ROWS = 16
EXP_CHUNK = 128

V7X_VMEM_LIMIT = 56 * 1024 * 1024

TOK_TILE = 512
FULL_T = 512
BAND_T = 256
DIL_BAND = -(-max(w // 2 for w, _ in DIL_PATTERNS) // BAND_T)
DIL_Q_TILES = 2
MLA_Q_BLOCKS = 4
DIFF_Q_BLOCKS = 1
NA_GROUP = 8
NA_ROWS_PER_ITER = 2
FF_SPLIT = 2


def _rms(x, g):
    return x * lax.rsqrt(jnp.mean(x * x, axis=-1, keepdims=True) + EPS) * g


def _nt(a, b):
    return lax.dot_general(a, b, (((1,), (1,)), ((), ())), preferred_element_type=F32)


def _zero_of(x):
    u = lax.bitcast_convert_type(x, jnp.uint32)
    return lax.bitcast_convert_type((u >> 16) >> 16, F32)


def _flash_pipelined(streams, ranges, bufs):
    tk, tq = bufs[0][0].shape
    first = ranges[0][0]
    last = ranges[-1][1] - 1
    ones = jnp.ones((SUM_ROWS, tk), BF16)

    def scores(st, j, bias_fn, s_ref):
        s = jnp.dot(st['load_k'](j), st['qz'], preferred_element_type=F32)
        if bias_fn is not None:
            s = s + bias_fn(j)
        s_ref[...] = s
        return jnp.max(s, axis=0, keepdims=True)

    def accumulate(st, buf, j, par):
        acc_ref, stat_ref = buf[4], buf[5]
        v_aug = jnp.concatenate([st['load_vt'](j), ones], axis=0)
        acc_ref[...] = (stat_ref[_ALPHA + par:_ALPHA + par + 1, :] * acc_ref[...]
                        + jnp.dot(v_aug, buf[2 + par][...], preferred_element_type=F32))

    def softmax_update(buf, par, s_new):
        stat_ref = buf[5]
        m = stat_ref[_MAX:_MAX + 1, :]
        c = stat_ref[_CONST:_CONST + 1, :]
        m_new = jnp.maximum(m, stat_ref[_CMAX:_CMAX + 1, :] + c)
        ref_rows = jnp.broadcast_to(m_new - c, (8, tq))
        for lo_row in range(0, tk, EXP_CHUNK):
            rows = slice(lo_row, lo_row + EXP_CHUNK)
            ref = ref_rows if s_new is None else ref_rows + _zero_of(s_new[lo_row:lo_row + 8, :])
            x = buf[1 - par][rows, :].reshape(EXP_CHUNK // 8, 8, tq) - ref[None]
            buf[3 - par][rows, :] = jnp.exp2(x.reshape(EXP_CHUNK, tq)).astype(BF16)
        stat_ref[_ALPHA + 1 - par:_ALPHA + 2 - par, :] = jnp.exp2(m - m_new)
        stat_ref[_MAX:_MAX + 1, :] = m_new

    def step(par, r, t):
        for st, buf in zip(streams, bufs):
            stat_ref = buf[5]
            s_new = jnp.dot(st['load_k'](t), st['qz'], preferred_element_type=F32)
            if st['bias_fns'][r] is not None:
                s_new = s_new + st['bias_fns'][r](t)
            buf[par][...] = s_new
            cmax_t = jnp.max(s_new, axis=0, keepdims=True)
            softmax_update(buf, par, s_new)
            stat_ref[_CMAX:_CMAX + 1, :] = cmax_t
            stat_ref[_CONST:_CONST + 1, :] = jnp.zeros((1, tq), F32) + st['shifts'][r]
            accumulate(st, buf, jnp.maximum(t - 2, first), par)

    def drain(par):
        for st, buf in zip(streams, bufs):
            softmax_update(buf, par, None)
            accumulate(st, buf, jnp.maximum(last - 1, first), par)
            accumulate(st, buf, last, 1 - par)

    for st, buf in zip(streams, bufs):
        acc_ref, stat_ref = buf[4], buf[5]
        cmax = scores(st, first, st.get('first_bias', st['bias_fns'][0]), buf[0])
        buf[3][...] = jnp.zeros(buf[3].shape, buf[3].dtype)
        acc_ref[...] = jnp.zeros_like(acc_ref)
        stat_ref[_ALPHA:_ALPHA + 2, :] = jnp.ones((2, tq), F32)
        stat_ref[_MAX:_MAX + 1, :] = jnp.full((1, tq), NEG, F32)
        stat_ref[_CMAX:_CMAX + 1, :] = cmax
        stat_ref[_CONST:_CONST + 1, :] = (jnp.zeros((1, tq), F32)
                                          + (0.0 if 'first_bias' in st else st['shifts'][0]))

    def by_parity(j, fn):
        par = (j - first) % 2
        if isinstance(par, int):
            fn(par)
        else:
            pl.when(par == 0)(functools.partial(fn, 0))
            pl.when(par == 1)(functools.partial(fn, 1))

    for r, (lo, hi) in enumerate(ranges):

        def body(t, c, r=r):
            by_parity(t, lambda par: step(par, r, t))
            return c

        lax.fori_loop(jnp.maximum(lo, first + 1), hi, body, 0)

    by_parity(last + 1, drain)
    outs = []
    for buf in bufs:
        acc = buf[4][...]
        outs.append(acc[:HEAD_DIM] / acc[HEAD_DIM:HEAD_DIM + 1])
    return outs


_ALPHA, _MAX, _CMAX, _CONST = 0, 2, 3, 4
_BUFS_PER_STREAM = 6


def _flash_scratch(n_streams, tk, tq):
    per_stream = [pltpu.VMEM((tk, tq), F32), pltpu.VMEM((tk, tq), F32),
                  pltpu.VMEM((tk, tq), BF16), pltpu.VMEM((tk, tq), BF16),
                  pltpu.VMEM((HEAD_DIM + SUM_ROWS, tq), F32), pltpu.VMEM((8, tq), F32)]
    return per_stream * n_streams


def _flash_bufs(refs):
    n = _BUFS_PER_STREAM
    return [tuple(refs[n * k:n * k + n]) for k in range(len(refs) // n)]


def _store_t_blocks(ref, val, blk):
    for c in range(val.shape[1] // blk):
        ref[c] = val[:, c * blk:(c + 1) * blk].astype(ref.dtype)


def _proj_kernel(x_ref, g_ref, wnat_ref, wt_ref, gq_ref, wuqt_ref, gkv_ref, wuk_ref, wuvt_ref,
                 place_ref, csn_ref, cost_ref, sint_ref,
                 qa_ref, ka_ref, va_ref, kb_ref, kc_ref, kd_ref,
                 qbt_ref, vbt_ref, qct_ref, vct_ref, qdt_ref, vdt_ref):
    h = _rms(x_ref[...], g_ref[...]).astype(BF16)
    nat = jnp.dot(h, wnat_ref[...], preferred_element_type=F32)
    qa_ref[...] = (nat[:, 0:256] * (HEAD_DIM ** -0.5 * LOG2E)).astype(BF16)
    ka_ref[...] = nat[:, 256:512].astype(BF16)
    va_ref[...] = nat[:, 512:768].astype(BF16)
    kb_ref[...] = nat[:, 768:1024].astype(BF16)
    kc_ref[...] = nat[:, 1024:1280].astype(BF16)
    cq = nat[:, 1280:1536]
    ckv = nat[:, 1536:1664]
    kr2 = nat[:, 1664:1792]

    tr = _nt(wt_ref[...], h)
    _store_t_blocks(qbt_ref, tr[0:256] * (DIFF_QK ** -0.5 * LOG2E), FULL_T)
    _store_t_blocks(vbt_ref, tr[256:512], FULL_T)
    _store_t_blocks(qct_ref, tr[512:768] * (HEAD_DIM ** -0.5 * LOG2E), BAND_T)
    _store_t_blocks(vct_ref, tr[768:1024], BAND_T)

    cqn = _rms(cq, gq_ref[...]).astype(BF16)
    qd_t = _nt(wuqt_ref[...], cqn)
    cos_t = cost_ref[...]
    sin_t = sint_ref[...]
    pieces = []
    for hd in range(N_HEADS):
        b0 = hd * PAIR
        x1 = qd_t[b0 + 64:b0 + 80]
        x2 = qd_t[b0 + 80:b0 + 96]
        pieces += [qd_t[b0:b0 + 64], x1 * cos_t - x2 * sin_t, x1 * sin_t + x2 * cos_t,
                   qd_t[b0 + 96:b0 + 128]]
    qd_t = jnp.concatenate(pieces, axis=0) * (MLA_QK ** -0.5 * LOG2E)
    _store_t_blocks(qdt_ref, qd_t, FULL_T)

    ckvn = _rms(ckv, gkv_ref[...]).astype(BF16)
    k_nope = jnp.dot(ckvn, wuk_ref[...], preferred_element_type=F32)
    t = kr2 * csn_ref[...]
    roped = t + pltpu.roll(t, 96, 1)
    kd = k_nope + jnp.dot(roped.astype(BF16), place_ref[...], preferred_element_type=F32)
    kd_ref[...] = kd.astype(BF16)
    _store_t_blocks(vdt_ref, _nt(wuvt_ref[...], ckvn), FULL_T)


def _proj_call(x, g, p, rope):
    B, S, D = x.shape
    ts = TOK_TILE
    nt = S // ts
    nf, nb = ts // FULL_T, ts // BAND_T

    def full(a):
        return pl.BlockSpec(a.shape, lambda b, t: (0,) * a.ndim)

    nat_spec = lambda w: pl.BlockSpec((None, ts, w), lambda b, t: (b, t, 0))
    tb_spec = lambda n, r, blk: pl.BlockSpec((None, n, r, blk), lambda b, t: (b, t, 0, 0))
    nat_shape = lambda w: jax.ShapeDtypeStruct((B, S, w), BF16)
    tb_shape = lambda r, blk: jax.ShapeDtypeStruct((B, S // blk, r, blk), BF16)

    weights = [g, p['w_nat'], p['w_t'], p['g_q'], p['w_uq_t'], p['g_kv'], p['w_uk'], p['w_uv_t'], p['place']]
    in_specs = ([nat_spec(D)] + [full(a) for a in weights] +
                [pl.BlockSpec((ts, PAIR), lambda b, t: (t, 0)),
                 pl.BlockSpec((None, MLA_HALF, ts), lambda b, t: (t, 0, 0)),
                 pl.BlockSpec((None, MLA_HALF, ts), lambda b, t: (t, 0, 0))])
    out_shape = ([nat_shape(GROUP)] * 5 + [nat_shape(2 * GROUP)] +
                 [tb_shape(GROUP, FULL_T)] * 2 + [tb_shape(GROUP, BAND_T)] * 2 +
                 [tb_shape(2 * GROUP, FULL_T), tb_shape(GROUP, FULL_T)])
    out_specs = ([nat_spec(GROUP)] * 5 + [nat_spec(2 * GROUP)] +
                 [tb_spec(nf, GROUP, FULL_T)] * 2 + [tb_spec(nb, GROUP, BAND_T)] * 2 +
                 [tb_spec(nf, 2 * GROUP, FULL_T), tb_spec(nf, GROUP, FULL_T)])
    return pl.pallas_call(
        _proj_kernel,
        grid=(B, nt),
        in_specs=in_specs,
        out_specs=out_specs,
        out_shape=out_shape,
        compiler_params=pltpu.CompilerParams(
            dimension_semantics=("parallel", "parallel"), vmem_limit_bytes=V7X_VMEM_LIMIT),
        name="proj",
    )(x, *weights, rope['csn'], rope['cos_t'], rope['sin_t'])


def _na_kernel(q_ref, kp_ref, kc_ref, kn_ref, vp_ref, vc_ref, vn_ref, bias_ref, o_ref, kk_ref, vv_ref,
               *, n_rows):
    g = pl.program_id(1)
    blk = NA_GROUP * GRID_W
    win = NA_ROWS * GRID_W
    for c, (kr, vr) in enumerate(((kp_ref, vp_ref), (kc_ref, vc_ref), (kn_ref, vn_ref))):
        kk_ref[c * blk:(c + 1) * blk, :] = kr[...]
        vv_ref[c * blk:(c + 1) * blk, :] = vr[...]
    lane = lax.broadcasted_iota(jnp.int32, (GRID_W, PAIR), 1)

    first_head = lane < HEAD_DIM

    def rows_body(jj, carry):
        v_pairs, scores, probs = [], [], []
        for dj in range(NA_ROWS_PER_ITER):
            j = jj * NA_ROWS_PER_ITER + dj
            r = g * NA_GROUP + j
            r0 = jnp.clip(r - NA_ROWS // 2, 0, n_rows - NA_ROWS)
            start = pl.multiple_of((r0 - (g - 1) * NA_GROUP) * GRID_W, GRID_W)
            q_row = q_ref[pl.ds(pl.multiple_of(j * GRID_W, GRID_W), GRID_W), :]
            k_win = kk_ref[pl.ds(start, win), :]
            v_win = vv_ref[pl.ds(start, win), :]
            for pr in range(2):
                q2 = q_row[:, pr * PAIR:(pr + 1) * PAIR]
                zero = jnp.zeros_like(q2)
                qz = jnp.concatenate([jnp.where(first_head, q2, zero), jnp.where(first_head, zero, q2)], axis=0)
                bias = jnp.concatenate([bias_ref[r - r0, 2 * pr], bias_ref[r - r0, 2 * pr + 1]], axis=0)
                scores.append(_nt(qz, k_win[:, pr * PAIR:(pr + 1) * PAIR]) + bias)
                v_pairs.append(v_win[:, pr * PAIR:(pr + 1) * PAIR])
        for s in scores:
            e = jnp.exp2(s - jnp.max(s, axis=-1, keepdims=True))
            probs.append((e.astype(BF16), jnp.sum(e, axis=-1, keepdims=True)))
        outs = []
        for (e, den), v2 in zip(probs, v_pairs):
            o2 = jnp.dot(e, v2, preferred_element_type=F32) / den
            outs.append(jnp.where(first_head, o2[:GRID_W], o2[GRID_W:]))
        for dj in range(NA_ROWS_PER_ITER):
            qs = pl.multiple_of((jj * NA_ROWS_PER_ITER + dj) * GRID_W, GRID_W)
            o_ref[pl.ds(qs, GRID_W), :] = jnp.concatenate(outs[2 * dj:2 * dj + 2], axis=1).astype(o_ref.dtype)
        return carry

    lax.fori_loop(0, NA_GROUP // NA_ROWS_PER_ITER, rows_body, 0)


def _na_call(q, k, v, bias):
    B, S, _ = q.shape
    n_rows = S // GRID_W
    blk = NA_GROUP * GRID_W
    ng = S // blk
    cur = pl.BlockSpec((None, blk, GROUP), lambda b, g: (b, g, 0))
    prv = pl.BlockSpec((None, blk, GROUP), lambda b, g: (b, jnp.maximum(g - 1, 0), 0))
    nxt = pl.BlockSpec((None, blk, GROUP), lambda b, g: (b, jnp.minimum(g + 1, ng - 1), 0))
    return pl.pallas_call(
        functools.partial(_na_kernel, n_rows=n_rows),
        grid=(B, ng),
        in_specs=[cur, prv, cur, nxt, prv, cur, nxt,
                  pl.BlockSpec(bias.shape, lambda b, g: (0, 0, 0, 0))],
        out_specs=cur,
        out_shape=jax.ShapeDtypeStruct((B, S, GROUP), BF16),
        scratch_shapes=[pltpu.VMEM((3 * blk, GROUP), BF16), pltpu.VMEM((3 * blk, GROUP), BF16)],
        compiler_params=pltpu.CompilerParams(
            dimension_semantics=("parallel", "parallel"), vmem_limit_bytes=V7X_VMEM_LIMIT),
        name="na_attn",
    )(q, k, k, k, v, v, v, bias)


def _diff_kernel(c_ref, g_ref, qt_ref, k_ref, vt_ref, bias_ref, o_ref, *scratch, nk, lam_scale):
    i0 = pl.program_id(2) * DIFF_Q_BLOCKS
    tk = vt_ref.shape[2]
    tq = qt_ref.shape[2]
    row = lax.broadcasted_iota(jnp.int32, (PAIR, tq), 0)
    lam = c_ref[0:1, 0:1]
    j_lo = jnp.maximum(i0 - 2, 0)
    j_hi = jnp.minimum(i0 + DIFF_Q_BLOCKS + 2, nk)
    ranges = ((0, j_lo), (j_lo, j_hi), (j_hi, nk))

    def load_k(j):
        return k_ref[pl.ds(pl.multiple_of(j * tk, tk), tk), :]

    streams = []
    for qi in range(DIFF_Q_BLOCKS):
        q_t = qt_ref[qi]
        for hh in range(2):

            def load_vt(j, hh=hh):
                return vt_ref[j, hh * HEAD_DIM:(hh + 1) * HEAD_DIM, :]

            def near_bias(j, hh=hh, qi=qi):
                e0 = 2 * (j - i0 - qi) + 4

                def tile(d):
                    return bias_ref[jnp.clip(e0 + d, 0, 8), hh]
                t0 = tile(0)
                return jnp.concatenate(
                    [jnp.concatenate([t0, tile(-1)], axis=1),
                     jnp.concatenate([tile(1), t0], axis=1)], axis=0)

            for c in range(2):
                r_lo = hh * HEAD_DIM + c * DIFF_QK
                qz = jnp.where((row >= r_lo) & (row < r_lo + DIFF_QK), q_t, jnp.zeros_like(q_t))
                streams.append(dict(qz=qz, load_k=load_k, load_vt=load_vt,
                                    shifts=(c_ref[1 + hh:2 + hh, 0:1], 0.0, c_ref[3 + hh:4 + hh, 0:1]),
                                    bias_fns=(None, near_bias, None), first_bias=near_bias))
    maps = _flash_pipelined(streams, ranges, _flash_bufs(scratch))
    for qi in range(DIFF_Q_BLOCKS):
        heads = []
        for hh in range(2):
            k0 = 4 * qi + 2 * hh
            o = maps[k0] - lam * maps[k0 + 1]
            y = o * lax.rsqrt(jnp.mean(o * o, axis=0, keepdims=True) + EPS) * g_ref[:, 0:1]
            heads.append(y * lam_scale)
        o_ref[qi * tq:(qi + 1) * tq, :] = jnp.concatenate(heads, axis=0).T.astype(o_ref.dtype)


def _resident(block_shape, index_map):
    return pl.BlockSpec(block_shape, index_map, pipeline_mode=pl.Buffered(1))


def _diff_call(consts, g_col, q_t, k, v_t, bias, lam_scale):
    B, nq, _, tq = q_t.shape
    S = k.shape[1]
    nk, _, tk = v_t.shape[1:]
    nqb = DIFF_Q_BLOCKS
    return pl.pallas_call(
        functools.partial(_diff_kernel, nk=nk, lam_scale=lam_scale),
        grid=(B, 2, nq // nqb),
        in_specs=[
            pl.BlockSpec((None, 8, 128), lambda b, p, i: (p, 0, 0)),
            pl.BlockSpec(g_col.shape, lambda b, p, i: (0, 0)),
            pl.BlockSpec((None, nqb, PAIR, tq), lambda b, p, i: (b, i, p, 0)),
            _resident((None, S, PAIR), lambda b, p, i: (b, 0, p)),
            _resident((None, nk, PAIR, tk), lambda b, p, i: (b, 0, p, 0)),
            _resident((9, 2, BAND_T, BAND_T), lambda b, p, i: (0, p, 0, 0)),
        ],
        out_specs=pl.BlockSpec((None, nqb * tq, PAIR), lambda b, p, i: (b, i, p)),
        out_shape=jax.ShapeDtypeStruct((B, S, GROUP), BF16),
        scratch_shapes=_flash_scratch(4 * nqb, tk, tq),
        compiler_params=pltpu.CompilerParams(
            dimension_semantics=("parallel", "parallel", "parallel"), vmem_limit_bytes=V7X_VMEM_LIMIT),
        name="diff_attn",
    )(consts, g_col, q_t, k, v_t, bias)


def _dil_kernel(qt_ref, k_ref, vt_ref, bias_ref, o_ref, *scratch, nk):
    i = pl.program_id(1)
    t = BAND_T
    q_t = jnp.concatenate([qt_ref[a] for a in range(DIL_Q_TILES)], axis=1)
    row = lax.broadcasted_iota(jnp.int32, (PAIR, q_t.shape[1]), 0)
    j0 = i * DIL_Q_TILES
    j_lo = jnp.maximum(j0 - DIL_BAND, 0)
    j_hi = jnp.minimum(j0 + DIL_Q_TILES + DIL_BAND, nk)
    streams = []
    for head in range(N_HEADS):
        pr, hh = divmod(head, 2)

        def load_k(j, pr=pr):
            return k_ref[pl.ds(pl.multiple_of(j * t, t), t), pr * PAIR:(pr + 1) * PAIR]

        def load_vt(j, head=head):
            return vt_ref[j, head * HEAD_DIM:(head + 1) * HEAD_DIM, :]

        def bias(j, head=head):
            e0 = j - j0 + DIL_BAND + DIL_Q_TILES - 1
            return jnp.concatenate([bias_ref[e0 - a, head] for a in range(DIL_Q_TILES)], axis=1)

        q_pair = q_t[pr * PAIR:(pr + 1) * PAIR]
        qz = jnp.where((row // HEAD_DIM) == hh, q_pair, jnp.zeros_like(q_pair))
        streams.append(dict(qz=qz, load_k=load_k, load_vt=load_vt, shifts=(0.0,), bias_fns=(bias,)))
    heads = _flash_pipelined(streams, ((j_lo, j_hi),), _flash_bufs(scratch))
    o_ref[...] = jnp.concatenate(heads, axis=0).T.astype(o_ref.dtype)


def _dil_call(q_t, k, v_t, bias):
    B, nq, _, t = q_t.shape
    S = k.shape[1]
    nk = v_t.shape[1]
    nqt = DIL_Q_TILES
    return pl.pallas_call(
        functools.partial(_dil_kernel, nk=nk),
        grid=(B, nq // nqt),
        in_specs=[
            pl.BlockSpec((None, nqt, GROUP, t), lambda b, i: (b, i, 0, 0)),
            _resident((None, S, GROUP), lambda b, i: (b, 0, 0)),
            _resident((None, nk, GROUP, t), lambda b, i: (b, 0, 0, 0)),
            _resident(bias.shape, lambda b, i: (0, 0, 0, 0)),
        ],
        out_specs=pl.BlockSpec((None, nqt * t, GROUP), lambda b, i: (b, i, 0)),
        out_shape=jax.ShapeDtypeStruct((B, S, GROUP), BF16),
        scratch_shapes=_flash_scratch(N_HEADS, t, nqt * t),
        compiler_params=pltpu.CompilerParams(
            dimension_semantics=("parallel", "parallel"), vmem_limit_bytes=V7X_VMEM_LIMIT),
        name="dil_attn",
    )(q_t, k, v_t, bias)


def _mla_kernel(qt_ref, k_ref, vt_ref, o_ref, *scratch, nk):
    tk = vt_ref.shape[2]
    tq = qt_ref.shape[2]
    streams = []
    for qi in range(MLA_Q_BLOCKS):
        for hh in range(2):

            def load_k(j, hh=hh):
                return k_ref[pl.ds(pl.multiple_of(j * tk, tk), tk), hh * PAIR:(hh + 1) * PAIR]

            def load_vt(j, hh=hh):
                return vt_ref[j, hh * HEAD_DIM:(hh + 1) * HEAD_DIM, :]

            qz = qt_ref[qi, hh * PAIR:(hh + 1) * PAIR, :]
            streams.append(dict(qz=qz, load_k=load_k, load_vt=load_vt, shifts=(0.0,), bias_fns=(None,)))
    outs = _flash_pipelined(streams, ((0, nk),), _flash_bufs(scratch))
    for qi in range(MLA_Q_BLOCKS):
        o_t = jnp.concatenate(outs[2 * qi:2 * qi + 2], axis=0)
        o_ref[qi * tq:(qi + 1) * tq, :] = o_t.T.astype(o_ref.dtype)


def _mla_call(q_t, k, v_t):
    B, nq, _, tq = q_t.shape
    S = k.shape[1]
    nk, _, tk = v_t.shape[1:]
    nqb = MLA_Q_BLOCKS
    return pl.pallas_call(
        functools.partial(_mla_kernel, nk=nk),
        grid=(B, 2, nq // nqb),
        in_specs=[
            pl.BlockSpec((None, nqb, 2 * PAIR, tq), lambda b, p, i: (b, i, p, 0)),
            _resident((None, S, 2 * PAIR), lambda b, p, i: (b, 0, p)),
            _resident((None, nk, PAIR, tk), lambda b, p, i: (b, 0, p, 0)),
        ],
        out_specs=pl.BlockSpec((None, nqb * tq, PAIR), lambda b, p, i: (b, i, p)),
        out_shape=jax.ShapeDtypeStruct((B, S, GROUP), BF16),
        scratch_shapes=_flash_scratch(2 * nqb, tk, tq),
        compiler_params=pltpu.CompilerParams(
            dimension_semantics=("parallel", "parallel", "parallel"), vmem_limit_bytes=V7X_VMEM_LIMIT),
        name="mla_attn",
    )(q_t, k, v_t)


def _post_kernel(x_ref, oa_ref, ob_ref, oc_ref, od_ref, wo_ref, g_ref, wg_ref, wu_ref, wd_ref, gf_ref,
                 out_ref, acc_ref, h_ref, *, final):
    f = pl.program_id(1)

    @pl.when(f == 0)
    def _():
        x1 = x_ref[...]
        for gi, o_ref in enumerate((oa_ref, ob_ref, oc_ref, od_ref)):
            x1 = x1 + jnp.dot(o_ref[...], wo_ref[gi], preferred_element_type=F32)
        acc_ref[...] = x1
        h_ref[...] = _rms(x1, g_ref[...]).astype(BF16)

    h = h_ref[...]
    gate = jnp.dot(h, wg_ref[...], preferred_element_type=F32)
    up = jnp.dot(h, wu_ref[...], preferred_element_type=F32)
    act = (gate / (1.0 + jnp.exp(-gate))) * up
    acc_ref[...] += jnp.dot(act.astype(BF16), wd_ref[...], preferred_element_type=F32)

    @pl.when(f == pl.num_programs(1) - 1)
    def _():
        y = acc_ref[...]
        out_ref[...] = _rms(y, gf_ref[...]) if final else y


def _post_call(x, o_a, o_b, o_c, o_d, p, g_final, final):
    N, D = x.shape
    ts = TOK_TILE
    d_ff = p['w_gate'].shape[1]
    fc = d_ff // FF_SPLIT
    tok = lambda w: pl.BlockSpec((ts, w), lambda t, f: (t, 0))
    return pl.pallas_call(
        functools.partial(_post_kernel, final=final),
        grid=(N // ts, FF_SPLIT),
        in_specs=[tok(D), tok(GROUP), tok(GROUP), tok(GROUP), tok(GROUP),
                  pl.BlockSpec((4, GROUP, D), lambda t, f: (0, 0, 0)),
                  pl.BlockSpec((1, D), lambda t, f: (0, 0)),
                  pl.BlockSpec((D, fc), lambda t, f: (0, f)),
                  pl.BlockSpec((D, fc), lambda t, f: (0, f)),
                  pl.BlockSpec((fc, D), lambda t, f: (f, 0)),
                  pl.BlockSpec((1, D), lambda t, f: (0, 0))],
        out_specs=tok(D),
        out_shape=jax.ShapeDtypeStruct((N, D), F32),
        scratch_shapes=[pltpu.VMEM((ts, D), F32), pltpu.VMEM((ts, D), BF16)],
        compiler_params=pltpu.CompilerParams(
            dimension_semantics=("parallel", "arbitrary"), vmem_limit_bytes=V7X_VMEM_LIMIT),
        name="post_ffn",
    )(x, o_a, o_b, o_c, o_d, p['w_o'], p['g_ffn'], p['w_gate'], p['w_up'], p['w_down'], g_final)


def _t5_bucket(rel):
    nb = T5_BUCKETS // 2
    max_exact = nb // 2
    side = jnp.where(rel > 0, nb, 0)
    n = jnp.abs(rel)
    large = max_exact + (jnp.log(jnp.maximum(n, 1).astype(F32) / max_exact)
                         / math.log(T5_MAX_DIST / max_exact) * (nb - max_exact)).astype(jnp.int32)
    large = jnp.minimum(large, nb - 1)
    return side + jnp.where(n < max_exact, n, large)


def _lookup(table, idx):
    onehot = idx[..., None, None] == jnp.arange(table.shape[0])[:, None]
    return jnp.sum(jnp.where(onehot, table.astype(F32), 0.0), axis=-2)


def _toeplitz(w, t):
    lead = w.shape[:-1]
    w_pad = jnp.concatenate([w, jnp.zeros(lead + (1,), w.dtype)], axis=-1)
    m = jnp.tile(w_pad, t)[..., :t * (2 * t - 1)].reshape(lead + (t, 2 * t - 1))
    return m[..., t - 1:]


def _toeplitz_tiles(vec_fn, t, n_side):
    reach = n_side * t + t - 1
    w = jnp.moveaxis(vec_fn(jnp.arange(-reach, reach + 1)), -1, 0)
    tiles = [_toeplitz(w[:, (e + n_side) * t:(e + n_side + 2) * t - 1], t) for e in range(-n_side, n_side + 1)]
    return jnp.swapaxes(jnp.stack(tiles), -1, -2)


def _diff_bias(t5_diff):
    def vec(rel):
        return _lookup(t5_diff, _t5_bucket(rel)) * LOG2E
    tiles = _toeplitz_tiles(vec, BAND_T, 4)
    far = vec(jnp.array([-2 * T5_MAX_DIST, 2 * T5_MAX_DIST]))
    return tiles, far


def _dil_bias(t5_dil):
    def vec(rel):
        n = jnp.abs(rel)
        mult = jnp.zeros(rel.shape, F32)
        for window, dil in DIL_PATTERNS:
            mult = mult + ((rel % dil == 0) & (n <= window // 2)).astype(F32)
        b = _lookup(t5_dil, _t5_bucket(rel)) + jnp.log(jnp.maximum(mult, 1.0))[..., None]
        return jnp.where((mult > 0)[..., None], b * LOG2E, NEG)
    return _toeplitz_tiles(vec, BAND_T, DIL_BAND + DIL_Q_TILES - 1)


def _na_bias(rpb):
    d = jnp.arange(NA_ROWS)[:, None]
    ridx = jnp.arange(NA_ROWS)[None, :] - d + NA_ROWS - 1
    qc = jnp.arange(GRID_W)[:, None]
    kc = jnp.arange(GRID_W)[None, :]
    c0 = jnp.clip(qc - NA_COLS // 2, 0, GRID_W - NA_COLS)
    valid = (kc >= c0) & (kc < c0 + NA_COLS)
    rows = rpb.astype(F32)[:, ridx]
    side = GRID_W - NA_COLS
    vals = _toeplitz(jnp.pad(rows, ((0, 0),) * 3 + ((side, side),)), GRID_W)
    vals = jnp.where(valid, vals * LOG2E, NEG)
    vals = vals.transpose(1, 0, 3, 2, 4)
    return vals.reshape(NA_ROWS, N_HEADS, GRID_W, NA_ROWS * GRID_W)


def _rope_tables(S):
    inv_freq = ROPE_THETA ** (-jnp.arange(MLA_HALF, dtype=F32) / MLA_HALF)
    ang = jnp.arange(S).astype(F32)[:, None] * inv_freq[None, :]
    cos, sin = jnp.cos(ang), jnp.sin(ang)
    csn = jnp.concatenate([cos, cos, -sin, sin, jnp.zeros((S, PAIR - 4 * MLA_HALF), F32)], axis=1)
    nt = S // TOK_TILE
    to_t = lambda a: a.reshape(nt, TOK_TILE, MLA_HALF).transpose(0, 2, 1)
    return {'csn': csn, 'cos_t': to_t(cos), 'sin_t': to_t(sin)}


def _layer_weights(w_in, w_uq, w_ukv, w_o, w_gate, w_up, w_down, g_q, g_kv, g_ffn):
    G = GROUP
    col = lambda i: w_in[:, i * G:(i + 1) * G]
    cq, ckv, kr = w_in[:, 9 * G:10 * G], w_in[:, 10 * G:10 * G + 128], w_in[:, 10 * G + 128:]
    kr_swap = jnp.concatenate([kr[:, MLA_HALF:], kr[:, :MLA_HALF]], axis=1)
    kr2 = jnp.concatenate([kr, kr_swap, jnp.zeros((w_in.shape[0], PAIR - 2 * MLA_ROPE), w_in.dtype)], axis=1)
    w_nat = jnp.concatenate([col(0), col(1), col(2), col(4), col(7), cq, ckv, kr2], axis=1)
    w_t = jnp.concatenate([col(3), col(5), col(6), col(8)], axis=1).T
    uq = w_uq.reshape(w_uq.shape[0], N_HEADS, MLA_QK)
    uq = jnp.pad(uq, ((0, 0), (0, 0), (0, PAIR - MLA_QK))).reshape(w_uq.shape[0], N_HEADS * PAIR)
    ukv = w_ukv.reshape(w_ukv.shape[0], N_HEADS, 2 * HEAD_DIM)
    uk = ukv.at[:, :, HEAD_DIM:].set(0.0).reshape(w_ukv.shape[0], N_HEADS * PAIR)
    uv = ukv[:, :, HEAD_DIM:].reshape(w_ukv.shape[0], GROUP)
    place = jnp.zeros((PAIR, N_HEADS * PAIR), F32)
    r = jnp.arange(MLA_ROPE)
    for hd in range(N_HEADS):
        place = place.at[r, hd * PAIR + HEAD_DIM + r].set(1.0)
    b = lambda a: a.astype(BF16)
    row = lambda a: a.astype(F32)[None, :]
    return {
        'w_nat': b(w_nat), 'w_t': b(w_t), 'w_uq_t': b(uq.T), 'w_uk': b(uk), 'w_uv_t': b(uv.T),
        'place': b(place), 'g_q': row(g_q), 'g_kv': row(g_kv), 'g_ffn': row(g_ffn),
        'w_o': b(w_o.reshape(4, GROUP, w_o.shape[1])), 'w_gate': b(w_gate), 'w_up': b(w_up),
        'w_down': b(w_down),
    }


def kernel(x, attn_norm, w_in, na_rpb, diff_lambda, diff_subln, mla_q_norm, w_uq, mla_kv_norm, w_ukv,
           t5_table, w_o, ffn_norm, w_gate, w_up, w_down, final_norm):
    B, S, D = x.shape
    depth = w_in.shape[0]
    rope = _rope_tables(S)
    diff_tiles, diff_far = _diff_bias(t5_table[:, :N_HEADS])
    dil_tiles = _dil_bias(t5_table[:, N_HEADS:])
    g_final = final_norm.astype(F32)[None, :]

    for l in range(depth):
        p = _layer_weights(w_in[l], w_uq[l], w_ukv[l], w_o[l], w_gate[l], w_up[l], w_down[l],
                           mla_q_norm[l], mla_kv_norm[l], ffn_norm[l])
        (qa, ka, va, kb, kc, kd, qb_t, vb_t, qc_t, vc_t, qd_t, vd_t) = _proj_call(
            x, attn_norm[l].astype(F32)[None, :], p, rope)

        o_a = _na_call(qa, ka, va, _na_bias(na_rpb[l]))

        lam_init = 0.8 - 0.6 * math.exp(-0.3 * l)
        lq1, lk1, lq2, lk2 = [diff_lambda[l, j].astype(F32) for j in range(4)]
        lam = jnp.exp(jnp.sum(lq1 * lk1)) - jnp.exp(jnp.sum(lq2 * lk2)) + lam_init
        far = diff_far.reshape(2, 2, 2).transpose(1, 0, 2).reshape(2, 4)
        consts = jnp.concatenate([jnp.broadcast_to(lam, (2, 1)), far, jnp.zeros((2, 3), F32)], axis=1)
        consts = jnp.broadcast_to(consts[:, :, None], (2, 8, 128))
        g_col = jnp.broadcast_to(diff_subln[l].astype(F32)[:, None], (HEAD_DIM, 128))
        o_b = _diff_call(consts, g_col, qb_t, kb, vb_t, diff_tiles, 1.0 - lam_init)

        o_c = _dil_call(qc_t, kc, vc_t, dil_tiles)
        o_d = _mla_call(qd_t, kd, vd_t)

        flat = lambda a: a.reshape(B * S, a.shape[-1])
        x = _post_call(flat(x), flat(o_a), flat(o_b), flat(o_c), flat(o_d), p, g_final,
                       final=(l == depth - 1)).reshape(B, S, D)
    return x
```

```python
import functools
import math

import jax
import jax.numpy as jnp
from jax import lax
from jax.experimental import pallas as pl
from jax.experimental.pallas import tpu as pltpu

F32 = jnp.float32
BF16 = jnp.bfloat16

N_HEADS = 4
HEAD_DIM = 64
GROUP = N_HEADS * HEAD_DIM
PAIR = 2 * HEAD_DIM
GRID_W = 64
NA_ROWS = 8
NA_COLS = 16
DIFF_QK = HEAD_DIM // 2
DIL_PATTERNS = ((128, 1), (512, 4), (2048, 16))
MLA_ROPE = HEAD_DIM // 2
MLA_HALF = MLA_ROPE // 2
MLA_QK = HEAD_DIM + MLA_ROPE
ROPE_THETA = 10000.0
T5_BUCKETS = 32
T5_MAX_DIST = 1024
EPS = 1e-6
LOG2E = 1.4426950408889634
NEG = -1e30
SUM_ROWS = 16
EXP_CHUNK = 64

V7X_VMEM_LIMIT = 56 * 1024 * 1024

TOK_TILE = 512
FULL_T = 512
BAND_T = 256
DIL_BAND = -(-max(w // 2 for w, _ in DIL_PATTERNS) // BAND_T)
DIL_Q_TILES = 2
MLA_Q_BLOCKS = 2
DIFF_Q_BLOCKS = 1
NA_GROUP = 8
NA_ROWS_PER_ITER = 2
FF_SPLIT = 2


def _rms(x, g):
    return x * lax.rsqrt(jnp.mean(x * x, axis=-1, keepdims=True) + EPS) * g


def _nt(a, b):
    return lax.dot_general(a, b, (((1,), (1,)), ((), ())), preferred_element_type=F32)


def _zero_of(x):
    u = lax.bitcast_convert_type(x, jnp.uint32)
    return lax.bitcast_convert_type((u >> 16) >> 16, F32)


def _flash_pipelined(streams, ranges, bufs):
    tk, tq = bufs[0][0].shape
    first = ranges[0][0]
    last = ranges[-1][1] - 1
    ones = jnp.ones((SUM_ROWS, tk), BF16)

    def scores(st, j, bias_fn, s_ref):
        s = jnp.dot(st['load_k'](j), st['qz'], preferred_element_type=F32)
        if bias_fn is not None:
            s = s + bias_fn(j)
        s_ref[...] = s
        return jnp.max(s, axis=0, keepdims=True)

    def accumulate(st, buf, j, par):
        acc_ref, stat_ref = buf[4], buf[5]
        v_aug = jnp.concatenate([st['load_vt'](j), ones], axis=0)
        acc_ref[...] = (stat_ref[_ALPHA + par:_ALPHA + par + 1, :] * acc_ref[...]
                        + jnp.dot(v_aug, buf[2 + par][...], preferred_element_type=F32))

    def softmax_update(buf, par, s_new):
        stat_ref = buf[5]
        m = stat_ref[_MAX:_MAX + 1, :]
        c = stat_ref[_CONST:_CONST + 1, :]
        m_new = jnp.maximum(m, stat_ref[_CMAX:_CMAX + 1, :] + c)
        ref_rows = jnp.broadcast_to(m_new - c, (8, tq))
        for lo_row in range(0, tk, EXP_CHUNK):
            rows = slice(lo_row, lo_row + EXP_CHUNK)
            ref = ref_rows if s_new is None else ref_rows + _zero_of(s_new[lo_row:lo_row + 8, :])
            x = buf[1 - par][rows, :].reshape(EXP_CHUNK // 8, 8, tq) - ref[None]
            buf[3 - par][rows, :] = jnp.exp2(x.reshape(EXP_CHUNK, tq)).astype(BF16)
        stat_ref[_ALPHA + 1 - par:_ALPHA + 2 - par, :] = jnp.exp2(m - m_new)
        stat_ref[_MAX:_MAX + 1, :] = m_new

    def step(par, r, t):
        for st, buf in zip(streams, bufs):
            stat_ref = buf[5]
            s_new = jnp.dot(st['load_k'](t), st['qz'], preferred_element_type=F32)
            if st['bias_fns'][r] is not None:
                s_new = s_new + st['bias_fns'][r](t)
            buf[par][...] = s_new
            cmax_t = jnp.max(s_new, axis=0, keepdims=True)
            softmax_update(buf, par, s_new)
            stat_ref[_CMAX:_CMAX + 1, :] = cmax_t
            stat_ref[_CONST:_CONST + 1, :] = jnp.zeros((1, tq), F32) + st['shifts'][r]
            accumulate(st, buf, jnp.maximum(t - 2, first), par)

    def drain(par):
        for st, buf in zip(streams, bufs):
            softmax_update(buf, par, None)
            accumulate(st, buf, jnp.maximum(last - 1, first), par)
            accumulate(st, buf, last, 1 - par)

    for st, buf in zip(streams, bufs):
        acc_ref, stat_ref = buf[4], buf[5]
        cmax = scores(st, first, st.get('first_bias', st['bias_fns'][0]), buf[0])
        buf[3][...] = jnp.zeros(buf[3].shape, buf[3].dtype)
        acc_ref[...] = jnp.zeros_like(acc_ref)
        stat_ref[_ALPHA:_ALPHA + 2, :] = jnp.ones((2, tq), F32)
        stat_ref[_MAX:_MAX + 1, :] = jnp.full((1, tq), NEG, F32)
        stat_ref[_CMAX:_CMAX + 1, :] = cmax
        stat_ref[_CONST:_CONST + 1, :] = (jnp.zeros((1, tq), F32)
                                          + (0.0 if 'first_bias' in st else st['shifts'][0]))

    def by_parity(j, fn):
        par = (j - first) % 2
        if isinstance(par, int):
            fn(par)
        else:
            pl.when(par == 0)(functools.partial(fn, 0))
            pl.when(par == 1)(functools.partial(fn, 1))

    for r, (lo, hi) in enumerate(ranges):

        def body(t, c, r=r):
            by_parity(t, lambda par: step(par, r, t))
            return c

        lax.fori_loop(jnp.maximum(lo, first + 1), hi, body, 0)

    by_parity(last + 1, drain)
    outs = []
    for buf in bufs:
        acc = buf[4][...]
        outs.append(acc[:HEAD_DIM] / acc[HEAD_DIM:HEAD_DIM + 1])
    return outs


_ALPHA, _MAX, _CMAX, _CONST = 0, 2, 3, 4
_BUFS_PER_STREAM = 6


def _flash_scratch(n_streams, tk, tq):
    per_stream = [pltpu.VMEM((tk, tq), F32), pltpu.VMEM((tk, tq), F32),
                  pltpu.VMEM((tk, tq), BF16), pltpu.VMEM((tk, tq), BF16),
                  pltpu.VMEM((HEAD_DIM + SUM_ROWS, tq), F32), pltpu.VMEM((8, tq), F32)]
    return per_stream * n_streams


def _flash_bufs(refs):
    n = _BUFS_PER_STREAM
    return [tuple(refs[n * k:n * k + n]) for k in range(len(refs) // n)]


def _store_t_blocks(ref, val, blk):
    for c in range(val.shape[1] // blk):
        ref[c] = val[:, c * blk:(c + 1) * blk].astype(ref.dtype)


def _proj_kernel(x_ref, g_ref, wnat_ref, wt_ref, gq_ref, wuqt_ref, gkv_ref, wuk_ref, wuvt_ref,
                 place_ref, csn_ref, cost_ref, sint_ref,
                 qa_ref, ka_ref, va_ref, kb_ref, kc_ref, kd_ref,
                 qbt_ref, vbt_ref, qct_ref, vct_ref, qdt_ref, vdt_ref):
    h = _rms(x_ref[...], g_ref[...]).astype(BF16)
    nat = jnp.dot(h, wnat_ref[...], preferred_element_type=F32)
    qa_ref[...] = (nat[:, 0:256] * (HEAD_DIM ** -0.5 * LOG2E)).astype(BF16)
    ka_ref[...] = nat[:, 256:512].astype(BF16)
    va_ref[...] = nat[:, 512:768].astype(BF16)
    kb_ref[...] = nat[:, 768:1024].astype(BF16)
    kc_ref[...] = nat[:, 1024:1280].astype(BF16)
    cq = nat[:, 1280:1536]
    ckv = nat[:, 1536:1664]
    kr2 = nat[:, 1664:1792]

    tr = _nt(wt_ref[...], h)
    _store_t_blocks(qbt_ref, tr[0:256] * (DIFF_QK ** -0.5 * LOG2E), FULL_T)
    _store_t_blocks(vbt_ref, tr[256:512], FULL_T)
    _store_t_blocks(qct_ref, tr[512:768] * (HEAD_DIM ** -0.5 * LOG2E), BAND_T)
    _store_t_blocks(vct_ref, tr[768:1024], BAND_T)

    cqn = _rms(cq, gq_ref[...]).astype(BF16)
    qd_t = _nt(wuqt_ref[...], cqn)
    cos_t = cost_ref[...]
    sin_t = sint_ref[...]
    pieces = []
    for hd in range(N_HEADS):
        b0 = hd * PAIR
        x1 = qd_t[b0 + 64:b0 + 80]
        x2 = qd_t[b0 + 80:b0 + 96]
        pieces += [qd_t[b0:b0 + 64], x1 * cos_t - x2 * sin_t, x1 * sin_t + x2 * cos_t,
                   qd_t[b0 + 96:b0 + 128]]
    qd_t = jnp.concatenate(pieces, axis=0) * (MLA_QK ** -0.5 * LOG2E)
    _store_t_blocks(qdt_ref, qd_t, FULL_T)

    ckvn = _rms(ckv, gkv_ref[...]).astype(BF16)
    k_nope = jnp.dot(ckvn, wuk_ref[...], preferred_element_type=F32)
    t = kr2 * csn_ref[...]
    roped = t + pltpu.roll(t, 96, 1)
    kd = k_nope + jnp.dot(roped.astype(BF16), place_ref[...], preferred_element_type=F32)
    kd_ref[...] = kd.astype(BF16)
    _store_t_blocks(vdt_ref, _nt(wuvt_ref[...], ckvn), FULL_T)


def _proj_call(x, g, p, rope):
    B, S, D = x.shape
    ts = TOK_TILE
    nt = S // ts
    nf, nb = ts // FULL_T, ts // BAND_T

    def full(a):
        return pl.BlockSpec(a.shape, lambda b, t: (0,) * a.ndim)

    nat_spec = lambda w: pl.BlockSpec((None, ts, w), lambda b, t: (b, t, 0))
    tb_spec = lambda n, r, blk: pl.BlockSpec((None, n, r, blk), lambda b, t: (b, t, 0, 0))
    nat_shape = lambda w: jax.ShapeDtypeStruct((B, S, w), BF16)
    tb_shape = lambda r, blk: jax.ShapeDtypeStruct((B, S // blk, r, blk), BF16)

    weights = [g, p['w_nat'], p['w_t'], p['g_q'], p['w_uq_t'], p['g_kv'], p['w_uk'], p['w_uv_t'], p['place']]
    in_specs = ([nat_spec(D)] + [full(a) for a in weights] +
                [pl.BlockSpec((ts, PAIR), lambda b, t: (t, 0)),
                 pl.BlockSpec((None, MLA_HALF, ts), lambda b, t: (t, 0, 0)),
                 pl.BlockSpec((None, MLA_HALF, ts), lambda b, t: (t, 0, 0))])
    out_shape = ([nat_shape(GROUP)] * 5 + [nat_shape(2 * GROUP)] +
                 [tb_shape(GROUP, FULL_T)] * 2 + [tb_shape(GROUP, BAND_T)] * 2 +
                 [tb_shape(2 * GROUP, FULL_T), tb_shape(GROUP, FULL_T)])
    out_specs = ([nat_spec(GROUP)] * 5 + [nat_spec(2 * GROUP)] +
                 [tb_spec(nf, GROUP, FULL_T)] * 2 + [tb_spec(nb, GROUP, BAND_T)] * 2 +
                 [tb_spec(nf, 2 * GROUP, FULL_T), tb_spec(nf, GROUP, FULL_T)])
    return pl.pallas_call(
        _proj_kernel,
        grid=(B, nt),
        in_specs=in_specs,
        out_specs=out_specs,
        out_shape=out_shape,
        compiler_params=pltpu.CompilerParams(
            dimension_semantics=("parallel", "parallel"), vmem_limit_bytes=V7X_VMEM_LIMIT),
        name="proj",
    )(x, *weights, rope['csn'], rope['cos_t'], rope['sin_t'])


def _na_kernel(q_ref, kp_ref, kc_ref, kn_ref, vp_ref, vc_ref, vn_ref, bias_ref, o_ref, kk_ref, vv_ref,
               *, n_rows):
    g = pl.program_id(1)
    blk = NA_GROUP * GRID_W
    win = NA_ROWS * GRID_W
    for c, (kr, vr) in enumerate(((kp_ref, vp_ref), (kc_ref, vc_ref), (kn_ref, vn_ref))):
        kk_ref[c * blk:(c + 1) * blk, :] = kr[...]
        vv_ref[c * blk:(c + 1) * blk, :] = vr[...]
    lane = lax.broadcasted_iota(jnp.int32, (GRID_W, PAIR), 1)

    first_head = lane < HEAD_DIM

    def rows_body(jj, carry):
        v_pairs, scores, probs = [], [], []
        for dj in range(NA_ROWS_PER_ITER):
            j = jj * NA_ROWS_PER_ITER + dj
            r = g * NA_GROUP + j
            r0 = jnp.clip(r - NA_ROWS // 2, 0, n_rows - NA_ROWS)
            start = pl.multiple_of((r0 - (g - 1) * NA_GROUP) * GRID_W, GRID_W)
            q_row = q_ref[pl.ds(pl.multiple_of(j * GRID_W, GRID_W), GRID_W), :]
            k_win = kk_ref[pl.ds(start, win), :]
            v_win = vv_ref[pl.ds(start, win), :]
            for pr in range(2):
                q2 = q_row[:, pr * PAIR:(pr + 1) * PAIR]
                zero = jnp.zeros_like(q2)
                qz = jnp.concatenate([jnp.where(first_head, q2, zero), jnp.where(first_head, zero, q2)], axis=0)
                bias = jnp.concatenate([bias_ref[r - r0, 2 * pr], bias_ref[r - r0, 2 * pr + 1]], axis=0)
                scores.append(_nt(qz, k_win[:, pr * PAIR:(pr + 1) * PAIR]) + bias)
                v_pairs.append(v_win[:, pr * PAIR:(pr + 1) * PAIR])
        for s in scores:
            e = jnp.exp2(s - jnp.max(s, axis=-1, keepdims=True))
            probs.append((e.astype(BF16), jnp.sum(e, axis=-1, keepdims=True)))
        outs = []
        for (e, den), v2 in zip(probs, v_pairs):
            o2 = jnp.dot(e, v2, preferred_element_type=F32) / den
            outs.append(jnp.where(first_head, o2[:GRID_W], o2[GRID_W:]))
        for dj in range(NA_ROWS_PER_ITER):
            qs = pl.multiple_of((jj * NA_ROWS_PER_ITER + dj) * GRID_W, GRID_W)
            o_ref[pl.ds(qs, GRID_W), :] = jnp.concatenate(outs[2 * dj:2 * dj + 2], axis=1).astype(o_ref.dtype)
        return carry

    lax.fori_loop(0, NA_GROUP // NA_ROWS_PER_ITER, rows_body, 0)


def _na_call(q, k, v, bias):
    B, S, _ = q.shape
    n_rows = S // GRID_W
    blk = NA_GROUP * GRID_W
    ng = S // blk
    cur = pl.BlockSpec((None, blk, GROUP), lambda b, g: (b, g, 0))
    prv = pl.BlockSpec((None, blk, GROUP), lambda b, g: (b, jnp.maximum(g - 1, 0), 0))
    nxt = pl.BlockSpec((None, blk, GROUP), lambda b, g: (b, jnp.minimum(g + 1, ng - 1), 0))
    return pl.pallas_call(
        functools.partial(_na_kernel, n_rows=n_rows),
        grid=(B, ng),
        in_specs=[cur, prv, cur, nxt, prv, cur, nxt,
                  pl.BlockSpec(bias.shape, lambda b, g: (0, 0, 0, 0))],
        out_specs=cur,
        out_shape=jax.ShapeDtypeStruct((B, S, GROUP), BF16),
        scratch_shapes=[pltpu.VMEM((3 * blk, GROUP), BF16), pltpu.VMEM((3 * blk, GROUP), BF16)],
        compiler_params=pltpu.CompilerParams(
            dimension_semantics=("parallel", "parallel"), vmem_limit_bytes=V7X_VMEM_LIMIT),
        name="na_attn",
    )(q, k, k, k, v, v, v, bias)


def _diff_kernel(c_ref, g_ref, qt_ref, k_ref, vt_ref, bias_ref, o_ref, *scratch, nk, lam_scale):
    i0 = pl.program_id(2) * DIFF_Q_BLOCKS
    tk = vt_ref.shape[2]
    tq = qt_ref.shape[2]
    row = lax.broadcasted_iota(jnp.int32, (PAIR, tq), 0)
    lam = c_ref[0:1, 0:1]
    j_lo = jnp.maximum(i0 - 2, 0)
    j_hi = jnp.minimum(i0 + DIFF_Q_BLOCKS + 2, nk)
    ranges = ((0, j_lo), (j_lo, j_hi), (j_hi, nk))

    def load_k(j):
        return k_ref[pl.ds(pl.multiple_of(j * tk, tk), tk), :]

    streams = []
    for qi in range(DIFF_Q_BLOCKS):
        q_t = qt_ref[qi]
        for hh in range(2):

            def load_vt(j, hh=hh):
                return vt_ref[j, hh * HEAD_DIM:(hh + 1) * HEAD_DIM, :]

            def near_bias(j, hh=hh, qi=qi):
                e0 = 2 * (j - i0 - qi) + 4

                def tile(d):
                    return bias_ref[jnp.clip(e0 + d, 0, 8), hh]
                t0 = tile(0)
                return jnp.concatenate(
                    [jnp.concatenate([t0, tile(-1)], axis=1),
                     jnp.concatenate([tile(1), t0], axis=1)], axis=0)

            for c in range(2):
                r_lo = hh * HEAD_DIM + c * DIFF_QK
                qz = jnp.where((row >= r_lo) & (row < r_lo + DIFF_QK), q_t, jnp.zeros_like(q_t))
                streams.append(dict(qz=qz, load_k=load_k, load_vt=load_vt,
                                    shifts=(c_ref[1 + hh:2 + hh, 0:1], 0.0, c_ref[3 + hh:4 + hh, 0:1]),
                                    bias_fns=(None, near_bias, None), first_bias=near_bias))
    maps = _flash_pipelined(streams, ranges, _flash_bufs(scratch))
    for qi in range(DIFF_Q_BLOCKS):
        heads = []
        for hh in range(2):
            k0 = 4 * qi + 2 * hh
            o = maps[k0] - lam * maps[k0 + 1]
            y = o * lax.rsqrt(jnp.mean(o * o, axis=0, keepdims=True) + EPS) * g_ref[:, 0:1]
            heads.append(y * lam_scale)
        o_ref[qi * tq:(qi + 1) * tq, :] = jnp.concatenate(heads, axis=0).T.astype(o_ref.dtype)


def _resident(block_shape, index_map):
    return pl.BlockSpec(block_shape, index_map, pipeline_mode=pl.Buffered(1))


def _diff_call(consts, g_col, q_t, k, v_t, bias, lam_scale):
    B, nq, _, tq = q_t.shape
    S = k.shape[1]
    nk, _, tk = v_t.shape[1:]
    nqb = DIFF_Q_BLOCKS
    return pl.pallas_call(
        functools.partial(_diff_kernel, nk=nk, lam_scale=lam_scale),
        grid=(B, 2, nq // nqb),
        in_specs=[
            pl.BlockSpec((None, 8, 128), lambda b, p, i: (p, 0, 0)),
            pl.BlockSpec(g_col.shape, lambda b, p, i: (0, 0)),
            pl.BlockSpec((None, nqb, PAIR, tq), lambda b, p, i: (b, i, p, 0)),
            _resident((None, S, PAIR), lambda b, p, i: (b, 0, p)),
            _resident((None, nk, PAIR, tk), lambda b, p, i: (b, 0, p, 0)),
            _resident((9, 2, BAND_T, BAND_T), lambda b, p, i: (0, p, 0, 0)),
        ],
        out_specs=pl.BlockSpec((None, nqb * tq, PAIR), lambda b, p, i: (b, i, p)),
        out_shape=jax.ShapeDtypeStruct((B, S, GROUP), BF16),
        scratch_shapes=_flash_scratch(4 * nqb, tk, tq),
        compiler_params=pltpu.CompilerParams(
            dimension_semantics=("parallel", "parallel", "parallel"), vmem_limit_bytes=V7X_VMEM_LIMIT),
        name="diff_attn",
    )(consts, g_col, q_t, k, v_t, bias)


def _dil_kernel(qt_ref, k_ref, vt_ref, bias_ref, o_ref, *scratch, nk):
    i = pl.program_id(1)
    t = BAND_T
    q_t = jnp.concatenate([qt_ref[a] for a in range(DIL_Q_TILES)], axis=1)
    row = lax.broadcasted_iota(jnp.int32, (PAIR, q_t.shape[1]), 0)
    j0 = i * DIL_Q_TILES
    j_lo = jnp.maximum(j0 - DIL_BAND, 0)
    j_hi = jnp.minimum(j0 + DIL_Q_TILES + DIL_BAND, nk)
    streams = []
    for head in range(N_HEADS):
        pr, hh = divmod(head, 2)

        def load_k(j, pr=pr):
            return k_ref[pl.ds(pl.multiple_of(j * t, t), t), pr * PAIR:(pr + 1) * PAIR]

        def load_vt(j, head=head):
            return vt_ref[j, head * HEAD_DIM:(head + 1) * HEAD_DIM, :]

        def bias(j, head=head):
            e0 = j - j0 + DIL_BAND + DIL_Q_TILES - 1
            return jnp.concatenate([bias_ref[e0 - a, head] for a in range(DIL_Q_TILES)], axis=1)

        q_pair = q_t[pr * PAIR:(pr + 1) * PAIR]
        qz = jnp.where((row // HEAD_DIM) == hh, q_pair, jnp.zeros_like(q_pair))
        streams.append(dict(qz=qz, load_k=load_k, load_vt=load_vt, shifts=(0.0,), bias_fns=(bias,)))
    heads = _flash_pipelined(streams, ((j_lo, j_hi),), _flash_bufs(scratch))
    o_ref[...] = jnp.concatenate(heads, axis=0).T.astype(o_ref.dtype)


def _dil_call(q_t, k, v_t, bias):
    B, nq, _, t = q_t.shape
    S = k.shape[1]
    nk = v_t.shape[1]
    nqt = DIL_Q_TILES
    return pl.pallas_call(
        functools.partial(_dil_kernel, nk=nk),
        grid=(B, nq // nqt),
        in_specs=[
            pl.BlockSpec((None, nqt, GROUP, t), lambda b, i: (b, i, 0, 0)),
            _resident((None, S, GROUP), lambda b, i: (b, 0, 0)),
            _resident((None, nk, GROUP, t), lambda b, i: (b, 0, 0, 0)),
            _resident(bias.shape, lambda b, i: (0, 0, 0, 0)),
        ],
        out_specs=pl.BlockSpec((None, nqt * t, GROUP), lambda b, i: (b, i, 0)),
        out_shape=jax.ShapeDtypeStruct((B, S, GROUP), BF16),
        scratch_shapes=_flash_scratch(N_HEADS, t, nqt * t),
        compiler_params=pltpu.CompilerParams(
            dimension_semantics=("parallel", "parallel"), vmem_limit_bytes=V7X_VMEM_LIMIT),
        name="dil_attn",
    )(q_t, k, v_t, bias)


def _mla_kernel(qt_ref, k_ref, vt_ref, o_ref, *scratch, nk):
    tk = vt_ref.shape[2]
    tq = qt_ref.shape[2]
    streams = []
    for qi in range(MLA_Q_BLOCKS):
        for hh in range(2):

            def load_k(j, hh=hh):
                return k_ref[pl.ds(pl.multiple_of(j * tk, tk), tk), hh * PAIR:(hh + 1) * PAIR]

            def load_vt(j, hh=hh):
                return vt_ref[j, hh * HEAD_DIM:(hh + 1) * HEAD_DIM, :]

            qz = qt_ref[qi, hh * PAIR:(hh + 1) * PAIR, :]
            streams.append(dict(qz=qz, load_k=load_k, load_vt=load_vt, shifts=(0.0,), bias_fns=(None,)))
    outs = _flash_pipelined(streams, ((0, nk),), _flash_bufs(scratch))
    for qi in range(MLA_Q_BLOCKS):
        o_t = jnp.concatenate(outs[2 * qi:2 * qi + 2], axis=0)
        o_ref[qi * tq:(qi + 1) * tq, :] = o_t.T.astype(o_ref.dtype)


def _mla_call(q_t, k, v_t):
    B, nq, _, tq = q_t.shape
    S = k.shape[1]
    nk, _, tk = v_t.shape[1:]
    nqb = MLA_Q_BLOCKS
    return pl.pallas_call(
        functools.partial(_mla_kernel, nk=nk),
        grid=(B, 2, nq // nqb),
        in_specs=[
            pl.BlockSpec((None, nqb, 2 * PAIR, tq), lambda b, p, i: (b, i, p, 0)),
            _resident((None, S, 2 * PAIR), lambda b, p, i: (b, 0, p)),
            _resident((None, nk, PAIR, tk), lambda b, p, i: (b, 0, p, 0)),
        ],
        out_specs=pl.BlockSpec((None, nqb * tq, PAIR), lambda b, p, i: (b, i, p)),
        out_shape=jax.ShapeDtypeStruct((B, S, GROUP), BF16),
        scratch_shapes=_flash_scratch(2 * nqb, tk, tq),
        compiler_params=pltpu.CompilerParams(
            dimension_semantics=("parallel", "parallel", "parallel"), vmem_limit_bytes=V7X_VMEM_LIMIT),
        name="mla_attn",
    )(q_t, k, v_t)


def _post_kernel(x_ref, oa_ref, ob_ref, oc_ref, od_ref, wo_ref, g_ref, wg_ref, wu_ref, wd_ref, gf_ref,
                 out_ref, acc_ref, h_ref, *, final):
    f = pl.program_id(1)

    @pl.when(f == 0)
    def _():
        x1 = x_ref[...]
        for gi, o_ref in enumerate((oa_ref, ob_ref, oc_ref, od_ref)):
            x1 = x1 + jnp.dot(o_ref[...], wo_ref[gi], preferred_element_type=F32)
        acc_ref[...] = x1
        h_ref[...] = _rms(x1, g_ref[...]).astype(BF16)

    h = h_ref[...]
    gate = jnp.dot(h, wg_ref[...], preferred_element_type=F32)
    up = jnp.dot(h, wu_ref[...], preferred_element_type=F32)
    act = (gate / (1.0 + jnp.exp(-gate))) * up
    acc_ref[...] += jnp.dot(act.astype(BF16), wd_ref[...], preferred_element_type=F32)

    @pl.when(f == pl.num_programs(1) - 1)
    def _():
        y = acc_ref[...]
        out_ref[...] = _rms(y, gf_ref[...]) if final else y


def _post_call(x, o_a, o_b, o_c, o_d, p, g_final, final):
    N, D = x.shape
    ts = TOK_TILE
    d_ff = p['w_gate'].shape[1]
    fc = d_ff // FF_SPLIT
    tok = lambda w: pl.BlockSpec((ts, w), lambda t, f: (t, 0))
    return pl.pallas_call(
        functools.partial(_post_kernel, final=final),
        grid=(N // ts, FF_SPLIT),
        in_specs=[tok(D), tok(GROUP), tok(GROUP), tok(GROUP), tok(GROUP),
                  pl.BlockSpec((4, GROUP, D), lambda t, f: (0, 0, 0)),
                  pl.BlockSpec((1, D), lambda t, f: (0, 0)),
                  pl.BlockSpec((D, fc), lambda t, f: (0, f)),
                  pl.BlockSpec((D, fc), lambda t, f: (0, f)),
                  pl.BlockSpec((fc, D), lambda t, f: (f, 0)),
                  pl.BlockSpec((1, D), lambda t, f: (0, 0))],
        out_specs=tok(D),
        out_shape=jax.ShapeDtypeStruct((N, D), F32),
        scratch_shapes=[pltpu.VMEM((ts, D), F32), pltpu.VMEM((ts, D), BF16)],
        compiler_params=pltpu.CompilerParams(
            dimension_semantics=("parallel", "arbitrary"), vmem_limit_bytes=V7X_VMEM_LIMIT),
        name="post_ffn",
    )(x, o_a, o_b, o_c, o_d, p['w_o'], p['g_ffn'], p['w_gate'], p['w_up'], p['w_down'], g_final)


def _t5_bucket(rel):
    nb = T5_BUCKETS // 2
    max_exact = nb // 2
    side = jnp.where(rel > 0, nb, 0)
    n = jnp.abs(rel)
    large = max_exact + (jnp.log(jnp.maximum(n, 1).astype(F32) / max_exact)
                         / math.log(T5_MAX_DIST / max_exact) * (nb - max_exact)).astype(jnp.int32)
    large = jnp.minimum(large, nb - 1)
    return side + jnp.where(n < max_exact, n, large)


def _lookup(table, idx):
    onehot = idx[..., None, None] == jnp.arange(table.shape[0])[:, None]
    return jnp.sum(jnp.where(onehot, table.astype(F32), 0.0), axis=-2)


def _toeplitz(w, t):
    lead = w.shape[:-1]
    w_pad = jnp.concatenate([w, jnp.zeros(lead + (1,), w.dtype)], axis=-1)
    m = jnp.tile(w_pad, t)[..., :t * (2 * t - 1)].reshape(lead + (t, 2 * t - 1))
    return m[..., t - 1:]


def _toeplitz_tiles(vec_fn, t, n_side):
    reach = n_side * t + t - 1
    w = jnp.moveaxis(vec_fn(jnp.arange(-reach, reach + 1)), -1, 0)
    tiles = [_toeplitz(w[:, (e + n_side) * t:(e + n_side + 2) * t - 1], t) for e in range(-n_side, n_side + 1)]
    return jnp.swapaxes(jnp.stack(tiles), -1, -2)


def _diff_bias(t5_diff):
    def vec(rel):
        return _lookup(t5_diff, _t5_bucket(rel)) * LOG2E
    tiles = _toeplitz_tiles(vec, BAND_T, 4)
    far = vec(jnp.array([-2 * T5_MAX_DIST, 2 * T5_MAX_DIST]))
    return tiles, far


def _dil_bias(t5_dil):
    def vec(rel):
        n = jnp.abs(rel)
        mult = jnp.zeros(rel.shape, F32)
        for window, dil in DIL_PATTERNS:
            mult = mult + ((rel % dil == 0) & (n <= window // 2)).astype(F32)
        b = _lookup(t5_dil, _t5_bucket(rel)) + jnp.log(jnp.maximum(mult, 1.0))[..., None]
        return jnp.where((mult > 0)[..., None], b * LOG2E, NEG)
    return _toeplitz_tiles(vec, BAND_T, DIL_BAND + DIL_Q_TILES - 1)


def _na_bias(rpb):
    d = jnp.arange(NA_ROWS)[:, None]
    ridx = jnp.arange(NA_ROWS)[None, :] - d + NA_ROWS - 1
    qc = jnp.arange(GRID_W)[:, None]
    kc = jnp.arange(GRID_W)[None, :]
    c0 = jnp.clip(qc - NA_COLS // 2, 0, GRID_W - NA_COLS)
    valid = (kc >= c0) & (kc < c0 + NA_COLS)
    rows = rpb.astype(F32)[:, ridx]
    side = GRID_W - NA_COLS
    vals = _toeplitz(jnp.pad(rows, ((0, 0),) * 3 + ((side, side),)), GRID_W)
    vals = jnp.where(valid, vals * LOG2E, NEG)
    vals = vals.transpose(1, 0, 3, 2, 4)
    return vals.reshape(NA_ROWS, N_HEADS, GRID_W, NA_ROWS * GRID_W)


def _rope_tables(S):
    inv_freq = ROPE_THETA ** (-jnp.arange(MLA_HALF, dtype=F32) / MLA_HALF)
    ang = jnp.arange(S).astype(F32)[:, None] * inv_freq[None, :]
    cos, sin = jnp.cos(ang), jnp.sin(ang)
    csn = jnp.concatenate([cos, cos, -sin, sin, jnp.zeros((S, PAIR - 4 * MLA_HALF), F32)], axis=1)
    nt = S // TOK_TILE
    to_t = lambda a: a.reshape(nt, TOK_TILE, MLA_HALF).transpose(0, 2, 1)
    return {'csn': csn, 'cos_t': to_t(cos), 'sin_t': to_t(sin)}


def _layer_weights(w_in, w_uq, w_ukv, w_o, w_gate, w_up, w_down, g_q, g_kv, g_ffn):
    G = GROUP
    col = lambda i: w_in[:, i * G:(i + 1) * G]
    cq, ckv, kr = w_in[:, 9 * G:10 * G], w_in[:, 10 * G:10 * G + 128], w_in[:, 10 * G + 128:]
    kr_swap = jnp.concatenate([kr[:, MLA_HALF:], kr[:, :MLA_HALF]], axis=1)
    kr2 = jnp.concatenate([kr, kr_swap, jnp.zeros((w_in.shape[0], PAIR - 2 * MLA_ROPE), w_in.dtype)], axis=1)
    w_nat = jnp.concatenate([col(0), col(1), col(2), col(4), col(7), cq, ckv, kr2], axis=1)
    w_t = jnp.concatenate([col(3), col(5), col(6), col(8)], axis=1).T
    uq = w_uq.reshape(w_uq.shape[0], N_HEADS, MLA_QK)
    uq = jnp.pad(uq, ((0, 0), (0, 0), (0, PAIR - MLA_QK))).reshape(w_uq.shape[0], N_HEADS * PAIR)
    ukv = w_ukv.reshape(w_ukv.shape[0], N_HEADS, 2 * HEAD_DIM)
    uk = ukv.at[:, :, HEAD_DIM:].set(0.0).reshape(w_ukv.shape[0], N_HEADS * PAIR)
    uv = ukv[:, :, HEAD_DIM:].reshape(w_ukv.shape[0], GROUP)
    place = jnp.zeros((PAIR, N_HEADS * PAIR), F32)
    r = jnp.arange(MLA_ROPE)
    for hd in range(N_HEADS):
        place = place.at[r, hd * PAIR + HEAD_DIM + r].set(1.0)
    b = lambda a: a.astype(BF16)
    row = lambda a: a.astype(F32)[None, :]
    return {
        'w_nat': b(w_nat), 'w_t': b(w_t), 'w_uq_t': b(uq.T), 'w_uk': b(uk), 'w_uv_t': b(uv.T),
        'place': b(place), 'g_q': row(g_q), 'g_kv': row(g_kv), 'g_ffn': row(g_ffn),
        'w_o': b(w_o.reshape(4, GROUP, w_o.shape[1])), 'w_gate': b(w_gate), 'w_up': b(w_up),
        'w_down': b(w_down),
    }


def kernel(x, attn_norm, w_in, na_rpb, diff_lambda, diff_subln, mla_q_norm, w_uq, mla_kv_norm, w_ukv,
           t5_table, w_o, ffn_norm, w_gate, w_up, w_down, final_norm):
    B, S, D = x.shape
    depth = w_in.shape[0]
    rope = _rope_tables(S)
    diff_tiles, diff_far = _diff_bias(t5_table[:, :N_HEADS])
    dil_tiles = _dil_bias(t5_table[:, N_HEADS:])
    g_final = final_norm.astype(F32)[None, :]

    for l in range(depth):
        p = _layer_weights(w_in[l], w_uq[l], w_ukv[l], w_o[l], w_gate[l], w_up[l], w_down[l],
                           mla_q_norm[l], mla_kv_norm[l], ffn_norm[l])
        (qa, ka, va, kb, kc, kd, qb_t, vb_t, qc_t, vc_t, qd_t, vd_t) = _proj_call(
            x, attn_norm[l].astype(F32)[None, :], p, rope)

        o_a = _na_call(qa, ka, va, _na_bias(na_rpb[l]))

        lam_init = 0.8 - 0.6 * math.exp(-0.3 * l)
        lq1, lk1, lq2, lk2 = [diff_lambda[l, j].astype(F32) for j in range(4)]
        lam = jnp.exp(jnp.sum(lq1 * lk1)) - jnp.exp(jnp.sum(lq2 * lk2)) + lam_init
        far = diff_far.reshape(2, 2, 2).transpose(1, 0, 2).reshape(2, 4)
        consts = jnp.concatenate([jnp.broadcast_to(lam, (2, 1)), far, jnp.zeros((2, 3), F32)], axis=1)
        consts = jnp.broadcast_to(consts[:, :, None], (2, 8, 128))
        g_col = jnp.broadcast_to(diff_subln[l].astype(F32)[:, None], (HEAD_DIM, 128))
        o_b = _diff_call(consts, g_col, qb_t, kb, vb_t, diff_tiles, 1.0 - lam_init)

        o_c = _dil_call(qc_t, kc, vc_t, dil_tiles)
        o_d = _mla_call(qd_t, kd, vd_t)

        flat = lambda a: a.reshape(B * S, a.shape[-1])
        x = _post_call(flat(x), flat(o_a), flat(o_b), flat(o_c), flat(o_d), p, g_final,
                       final=(l == depth - 1)).reshape(B, S, D)
    return x
```

```python
import functools
import math

import jax
import jax.numpy as jnp
from jax import lax
from jax.experimental import pallas as pl
from jax.experimental.pallas import tpu as pltpu

F32 = jnp.float32
BF16 = jnp.bfloat16

N_HEADS = 4
HEAD_DIM = 64
GROUP = N_HEADS * HEAD_DIM
PAIR = 2 * HEAD_DIM
GRID_W = 64
NA_ROWS = 8
NA_COLS = 16
DIFF_QK = HEAD_DIM // 2
DIL_PATTERNS = ((128, 1), (512, 4), (2048, 16))
MLA_ROPE = HEAD_DIM // 2
MLA_HALF = MLA_ROPE // 2
MLA_QK = HEAD_DIM + MLA_ROPE
ROPE_THETA = 10000.0
T5_BUCKETS = 32
T5_MAX_DIST = 1024
EPS = 1e-6
LOG2E = 1.4426950408889634
NEG = -1e30
SUM_ROWS = 16
EXP_CHUNK = 64

V7X_VMEM_LIMIT = 56 * 1024 * 1024

TOK_TILE = 512
FULL_T = 512
BAND_T = 256
DIL_BAND = -(-max(w // 2 for w, _ in DIL_PATTERNS) // BAND_T)
DIL_Q_TILES = 2
MLA_Q_BLOCKS = 4
DIFF_Q_BLOCKS = 2
NA_GROUP = 8
NA_ROWS_PER_ITER = 2
FF_SPLIT = 2


def _rms(x, g):
    return x * lax.rsqrt(jnp.mean(x * x, axis=-1, keepdims=True) + EPS) * g


def _nt(a, b):
    return lax.dot_general(a, b, (((1,), (1,)), ((), ())), preferred_element_type=F32)


def _zero_of(x):
    u = lax.bitcast_convert_type(x, jnp.uint32)
    return lax.bitcast_convert_type((u >> 16) >> 16, F32)


def _flash_pipelined(streams, ranges, bufs):
    tk, tq = bufs[0][0].shape
    first = ranges[0][0]
    last = ranges[-1][1] - 1
    ones = jnp.ones((SUM_ROWS, tk), BF16)

    def scores(st, j, bias_fn, s_ref):
        s = jnp.dot(st['load_k'](j), st['qz'], preferred_element_type=F32)
        if bias_fn is not None:
            s = s + bias_fn(j)
        s_ref[...] = s
        return jnp.max(s, axis=0, keepdims=True)

    def accumulate(st, buf, j, par):
        acc_ref, stat_ref = buf[4], buf[5]
        v_aug = jnp.concatenate([st['load_vt'](j), ones], axis=0)
        acc_ref[...] = (stat_ref[_ALPHA + par:_ALPHA + par + 1, :] * acc_ref[...]
                        + jnp.dot(v_aug, buf[2 + par][...], preferred_element_type=F32))

    def softmax_update(buf, par, s_new):
        stat_ref = buf[5]
        m = stat_ref[_MAX:_MAX + 1, :]
        c = stat_ref[_CONST:_CONST + 1, :]
        m_new = jnp.maximum(m, stat_ref[_CMAX:_CMAX + 1, :] + c)
        ref_rows = jnp.broadcast_to(m_new - c, (8, tq))
        for lo_row in range(0, tk, EXP_CHUNK):
            rows = slice(lo_row, lo_row + EXP_CHUNK)
            ref = ref_rows if s_new is None else ref_rows + _zero_of(s_new[lo_row:lo_row + 8, :])
            x = buf[1 - par][rows, :].reshape(EXP_CHUNK // 8, 8, tq) - ref[None]
            buf[3 - par][rows, :] = jnp.exp2(x.reshape(EXP_CHUNK, tq)).astype(BF16)
        stat_ref[_ALPHA + 1 - par:_ALPHA + 2 - par, :] = jnp.exp2(m - m_new)
        stat_ref[_MAX:_MAX + 1, :] = m_new

    def step(par, r, t):
        for st, buf in zip(streams, bufs):
            stat_ref = buf[5]
            s_new = jnp.dot(st['load_k'](t), st['qz'], preferred_element_type=F32)
            if st['bias_fns'][r] is not None:
                s_new = s_new + st['bias_fns'][r](t)
            buf[par][...] = s_new
            cmax_t = jnp.max(s_new, axis=0, keepdims=True)
            softmax_update(buf, par, s_new)
            stat_ref[_CMAX:_CMAX + 1, :] = cmax_t
            stat_ref[_CONST:_CONST + 1, :] = jnp.zeros((1, tq), F32) + st['shifts'][r]
            accumulate(st, buf, jnp.maximum(t - 2, first), par)

    def drain(par):
        for st, buf in zip(streams, bufs):
            softmax_update(buf, par, None)
            accumulate(st, buf, jnp.maximum(last - 1, first), par)
            accumulate(st, buf, last, 1 - par)

    for st, buf in zip(streams, bufs):
        acc_ref, stat_ref = buf[4], buf[5]
        cmax = scores(st, first, st.get('first_bias', st['bias_fns'][0]), buf[0])
        buf[3][...] = jnp.zeros(buf[3].shape, buf[3].dtype)
        acc_ref[...] = jnp.zeros_like(acc_ref)
        stat_ref[_ALPHA:_ALPHA + 2, :] = jnp.ones((2, tq), F32)
        stat_ref[_MAX:_MAX + 1, :] = jnp.full((1, tq), NEG, F32)
        stat_ref[_CMAX:_CMAX + 1, :] = cmax
        stat_ref[_CONST:_CONST + 1, :] = (jnp.zeros((1, tq), F32)
                                          + (0.0 if 'first_bias' in st else st['shifts'][0]))

    def by_parity(j, fn):
        par = (j - first) % 2
        if isinstance(par, int):
            fn(par)
        else:
            pl.when(par == 0)(functools.partial(fn, 0))
            pl.when(par == 1)(functools.partial(fn, 1))

    for r, (lo, hi) in enumerate(ranges):

        def body(t, c, r=r):
            by_parity(t, lambda par: step(par, r, t))
            return c

        lax.fori_loop(jnp.maximum(lo, first + 1), hi, body, 0)

    by_parity(last + 1, drain)
    outs = []
    for buf in bufs:
        acc = buf[4][...]
        outs.append(acc[:HEAD_DIM] / acc[HEAD_DIM:HEAD_DIM + 1])
    return outs


_ALPHA, _MAX, _CMAX, _CONST = 0, 2, 3, 4
_BUFS_PER_STREAM = 6


def _flash_scratch(n_streams, tk, tq):
    per_stream = [pltpu.VMEM((tk, tq), F32), pltpu.VMEM((tk, tq), F32),
                  pltpu.VMEM((tk, tq), BF16), pltpu.VMEM((tk, tq), BF16),
                  pltpu.VMEM((HEAD_DIM + SUM_ROWS, tq), F32), pltpu.VMEM((8, tq), F32)]
    return per_stream * n_streams


def _flash_bufs(refs):
    n = _BUFS_PER_STREAM
    return [tuple(refs[n * k:n * k + n]) for k in range(len(refs) // n)]


def _store_t_blocks(ref, val, blk):
    for c in range(val.shape[1] // blk):
        ref[c] = val[:, c * blk:(c + 1) * blk].astype(ref.dtype)


def _proj_kernel(x_ref, g_ref, wnat_ref, wt_ref, gq_ref, wuqt_ref, gkv_ref, wuk_ref, wuvt_ref,
                 place_ref, csn_ref, cost_ref, sint_ref,
                 qa_ref, ka_ref, va_ref, kb_ref, kc_ref, kd_ref,
                 qbt_ref, vbt_ref, qct_ref, vct_ref, qdt_ref, vdt_ref):
    h = _rms(x_ref[...], g_ref[...]).astype(BF16)
    nat = jnp.dot(h, wnat_ref[...], preferred_element_type=F32)
    qa_ref[...] = (nat[:, 0:256] * (HEAD_DIM ** -0.5 * LOG2E)).astype(BF16)
    ka_ref[...] = nat[:, 256:512].astype(BF16)
    va_ref[...] = nat[:, 512:768].astype(BF16)
    kb_ref[...] = nat[:, 768:1024].astype(BF16)
    kc_ref[...] = nat[:, 1024:1280].astype(BF16)
    cq = nat[:, 1280:1536]
    ckv = nat[:, 1536:1664]
    kr2 = nat[:, 1664:1792]

    tr = _nt(wt_ref[...], h)
    _store_t_blocks(qbt_ref, tr[0:256] * (DIFF_QK ** -0.5 * LOG2E), FULL_T)
    _store_t_blocks(vbt_ref, tr[256:512], FULL_T)
    _store_t_blocks(qct_ref, tr[512:768] * (HEAD_DIM ** -0.5 * LOG2E), BAND_T)
    _store_t_blocks(vct_ref, tr[768:1024], BAND_T)

    cqn = _rms(cq, gq_ref[...]).astype(BF16)
    qd_t = _nt(wuqt_ref[...], cqn)
    cos_t = cost_ref[...]
    sin_t = sint_ref[...]
    pieces = []
    for hd in range(N_HEADS):
        b0 = hd * PAIR
        x1 = qd_t[b0 + 64:b0 + 80]
        x2 = qd_t[b0 + 80:b0 + 96]
        pieces += [qd_t[b0:b0 + 64], x1 * cos_t - x2 * sin_t, x1 * sin_t + x2 * cos_t,
                   qd_t[b0 + 96:b0 + 128]]
    qd_t = jnp.concatenate(pieces, axis=0) * (MLA_QK ** -0.5 * LOG2E)
    _store_t_blocks(qdt_ref, qd_t, FULL_T)

    ckvn = _rms(ckv, gkv_ref[...]).astype(BF16)
    k_nope = jnp.dot(ckvn, wuk_ref[...], preferred_element_type=F32)
    t = kr2 * csn_ref[...]
    roped = t + pltpu.roll(t, 96, 1)
    kd = k_nope + jnp.dot(roped.astype(BF16), place_ref[...], preferred_element_type=F32)
    kd_ref[...] = kd.astype(BF16)
    _store_t_blocks(vdt_ref, _nt(wuvt_ref[...], ckvn), FULL_T)


def _proj_call(x, g, p, rope):
    B, S, D = x.shape
    ts = TOK_TILE
    nt = S // ts
    nf, nb = ts // FULL_T, ts // BAND_T

    def full(a):
        return pl.BlockSpec(a.shape, lambda b, t: (0,) * a.ndim)

    nat_spec = lambda w: pl.BlockSpec((None, ts, w), lambda b, t: (b, t, 0))
    tb_spec = lambda n, r, blk: pl.BlockSpec((None, n, r, blk), lambda b, t: (b, t, 0, 0))
    nat_shape = lambda w: jax.ShapeDtypeStruct((B, S, w), BF16)
    tb_shape = lambda r, blk: jax.ShapeDtypeStruct((B, S // blk, r, blk), BF16)

    weights = [g, p['w_nat'], p['w_t'], p['g_q'], p['w_uq_t'], p['g_kv'], p['w_uk'], p['w_uv_t'], p['place']]
    in_specs = ([nat_spec(D)] + [full(a) for a in weights] +
                [pl.BlockSpec((ts, PAIR), lambda b, t: (t, 0)),
                 pl.BlockSpec((None, MLA_HALF, ts), lambda b, t: (t, 0, 0)),
                 pl.BlockSpec((None, MLA_HALF, ts), lambda b, t: (t, 0, 0))])
    out_shape = ([nat_shape(GROUP)] * 5 + [nat_shape(2 * GROUP)] +
                 [tb_shape(GROUP, FULL_T)] * 2 + [tb_shape(GROUP, BAND_T)] * 2 +
                 [tb_shape(2 * GROUP, FULL_T), tb_shape(GROUP, FULL_T)])
    out_specs = ([nat_spec(GROUP)] * 5 + [nat_spec(2 * GROUP)] +
                 [tb_spec(nf, GROUP, FULL_T)] * 2 + [tb_spec(nb, GROUP, BAND_T)] * 2 +
                 [tb_spec(nf, 2 * GROUP, FULL_T), tb_spec(nf, GROUP, FULL_T)])
    return pl.pallas_call(
        _proj_kernel,
        grid=(B, nt),
        in_specs=in_specs,
        out_specs=out_specs,
        out_shape=out_shape,
        compiler_params=pltpu.CompilerParams(
            dimension_semantics=("parallel", "parallel"), vmem_limit_bytes=V7X_VMEM_LIMIT),
        name="proj",
    )(x, *weights, rope['csn'], rope['cos_t'], rope['sin_t'])


def _na_kernel(q_ref, kp_ref, kc_ref, kn_ref, vp_ref, vc_ref, vn_ref, bias_ref, o_ref, kk_ref, vv_ref,
               *, n_rows):
    g = pl.program_id(1)
    blk = NA_GROUP * GRID_W
    win = NA_ROWS * GRID_W
    for c, (kr, vr) in enumerate(((kp_ref, vp_ref), (kc_ref, vc_ref), (kn_ref, vn_ref))):
        kk_ref[c * blk:(c + 1) * blk, :] = kr[...]
        vv_ref[c * blk:(c + 1) * blk, :] = vr[...]
    lane = lax.broadcasted_iota(jnp.int32, (GRID_W, PAIR), 1)

    first_head = lane < HEAD_DIM

    def rows_body(jj, carry):
        v_pairs, scores, probs = [], [], []
        for dj in range(NA_ROWS_PER_ITER):
            j = jj * NA_ROWS_PER_ITER + dj
            r = g * NA_GROUP + j
            r0 = jnp.clip(r - NA_ROWS // 2, 0, n_rows - NA_ROWS)
            start = pl.multiple_of((r0 - (g - 1) * NA_GROUP) * GRID_W, GRID_W)
            q_row = q_ref[pl.ds(pl.multiple_of(j * GRID_W, GRID_W), GRID_W), :]
            k_win = kk_ref[pl.ds(start, win), :]
            v_win = vv_ref[pl.ds(start, win), :]
            for pr in range(2):
                q2 = q_row[:, pr * PAIR:(pr + 1) * PAIR]
                zero = jnp.zeros_like(q2)
                qz = jnp.concatenate([jnp.where(first_head, q2, zero), jnp.where(first_head, zero, q2)], axis=0)
                bias = jnp.concatenate([bias_ref[r - r0, 2 * pr], bias_ref[r - r0, 2 * pr + 1]], axis=0)
                scores.append(_nt(qz, k_win[:, pr * PAIR:(pr + 1) * PAIR]) + bias)
                v_pairs.append(v_win[:, pr * PAIR:(pr + 1) * PAIR])
        for s in scores:
            e = jnp.exp2(s - jnp.max(s, axis=-1, keepdims=True))
            probs.append((e.astype(BF16), jnp.sum(e, axis=-1, keepdims=True)))
        outs = []
        for (e, den), v2 in zip(probs, v_pairs):
            o2 = jnp.dot(e, v2, preferred_element_type=F32) / den
            outs.append(jnp.where(first_head, o2[:GRID_W], o2[GRID_W:]))
        for dj in range(NA_ROWS_PER_ITER):
            qs = pl.multiple_of((jj * NA_ROWS_PER_ITER + dj) * GRID_W, GRID_W)
            o_ref[pl.ds(qs, GRID_W), :] = jnp.concatenate(outs[2 * dj:2 * dj + 2], axis=1).astype(o_ref.dtype)
        return carry

    lax.fori_loop(0, NA_GROUP // NA_ROWS_PER_ITER, rows_body, 0)


def _na_call(q, k, v, bias):
    B, S, _ = q.shape
    n_rows = S // GRID_W
    blk = NA_GROUP * GRID_W
    ng = S // blk
    cur = pl.BlockSpec((None, blk, GROUP), lambda b, g: (b, g, 0))
    prv = pl.BlockSpec((None, blk, GROUP), lambda b, g: (b, jnp.maximum(g - 1, 0), 0))
    nxt = pl.BlockSpec((None, blk, GROUP), lambda b, g: (b, jnp.minimum(g + 1, ng - 1), 0))
    return pl.pallas_call(
        functools.partial(_na_kernel, n_rows=n_rows),
        grid=(B, ng),
        in_specs=[cur, prv, cur, nxt, prv, cur, nxt,
                  pl.BlockSpec(bias.shape, lambda b, g: (0, 0, 0, 0))],
        out_specs=cur,
        out_shape=jax.ShapeDtypeStruct((B, S, GROUP), BF16),
        scratch_shapes=[pltpu.VMEM((3 * blk, GROUP), BF16), pltpu.VMEM((3 * blk, GROUP), BF16)],
        compiler_params=pltpu.CompilerParams(
            dimension_semantics=("parallel", "parallel"), vmem_limit_bytes=V7X_VMEM_LIMIT),
        name="na_attn",
    )(q, k, k, k, v, v, v, bias)


def _diff_kernel(c_ref, g_ref, qt_ref, k_ref, vt_ref, bias_ref, o_ref, *scratch, nk, lam_scale):
    i0 = pl.program_id(2) * DIFF_Q_BLOCKS
    tk = vt_ref.shape[2]
    tq = qt_ref.shape[2]
    row = lax.broadcasted_iota(jnp.int32, (PAIR, tq), 0)
    lam = c_ref[0:1, 0:1]
    j_lo = jnp.maximum(i0 - 2, 0)
    j_hi = jnp.minimum(i0 + DIFF_Q_BLOCKS + 2, nk)
    ranges = ((0, j_lo), (j_lo, j_hi), (j_hi, nk))

    def load_k(j):
        return k_ref[pl.ds(pl.multiple_of(j * tk, tk), tk), :]

    streams = []
    for qi in range(DIFF_Q_BLOCKS):
        q_t = qt_ref[qi]
        for hh in range(2):

            def load_vt(j, hh=hh):
                return vt_ref[j, hh * HEAD_DIM:(hh + 1) * HEAD_DIM, :]

            def near_bias(j, hh=hh, qi=qi):
                e0 = 2 * (j - i0 - qi) + 4

                def tile(d):
                    return bias_ref[jnp.clip(e0 + d, 0, 8), hh]
                t0 = tile(0)
                return jnp.concatenate(
                    [jnp.concatenate([t0, tile(-1)], axis=1),
                     jnp.concatenate([tile(1), t0], axis=1)], axis=0)

            for c in range(2):
                r_lo = hh * HEAD_DIM + c * DIFF_QK
                qz = jnp.where((row >= r_lo) & (row < r_lo + DIFF_QK), q_t, jnp.zeros_like(q_t))
                streams.append(dict(qz=qz, load_k=load_k, load_vt=load_vt,
                                    shifts=(c_ref[1 + hh:2 + hh, 0:1], 0.0, c_ref[3 + hh:4 + hh, 0:1]),
                                    bias_fns=(None, near_bias, None), first_bias=near_bias))
    maps = _flash_pipelined(streams, ranges, _flash_bufs(scratch))
    for qi in range(DIFF_Q_BLOCKS):
        heads = []
        for hh in range(2):
            k0 = 4 * qi + 2 * hh
            o = maps[k0] - lam * maps[k0 + 1]
            y = o * lax.rsqrt(jnp.mean(o * o, axis=0, keepdims=True) + EPS) * g_ref[:, 0:1]
            heads.append(y * lam_scale)
        o_ref[qi * tq:(qi + 1) * tq, :] = jnp.concatenate(heads, axis=0).T.astype(o_ref.dtype)


def _resident(block_shape, index_map):
    return pl.BlockSpec(block_shape, index_map, pipeline_mode=pl.Buffered(1))


def _diff_call(consts, g_col, q_t, k, v_t, bias, lam_scale):
    B, nq, _, tq = q_t.shape
    S = k.shape[1]
    nk, _, tk = v_t.shape[1:]
    nqb = DIFF_Q_BLOCKS
    return pl.pallas_call(
        functools.partial(_diff_kernel, nk=nk, lam_scale=lam_scale),
        grid=(B, 2, nq // nqb),
        in_specs=[
            pl.BlockSpec((None, 8, 128), lambda b, p, i: (p, 0, 0)),
            pl.BlockSpec(g_col.shape, lambda b, p, i: (0, 0)),
            pl.BlockSpec((None, nqb, PAIR, tq), lambda b, p, i: (b, i, p, 0)),
            _resident((None, S, PAIR), lambda b, p, i: (b, 0, p)),
            _resident((None, nk, PAIR, tk), lambda b, p, i: (b, 0, p, 0)),
            _resident((9, 2, BAND_T, BAND_T), lambda b, p, i: (0, p, 0, 0)),
        ],
        out_specs=pl.BlockSpec((None, nqb * tq, PAIR), lambda b, p, i: (b, i, p)),
        out_shape=jax.ShapeDtypeStruct((B, S, GROUP), BF16),
        scratch_shapes=_flash_scratch(4 * nqb, tk, tq),
        compiler_params=pltpu.CompilerParams(
            dimension_semantics=("parallel", "parallel", "parallel"), vmem_limit_bytes=V7X_VMEM_LIMIT),
        name="diff_attn",
    )(consts, g_col, q_t, k, v_t, bias)


def _dil_kernel(qt_ref, k_ref, vt_ref, bias_ref, o_ref, *scratch, nk):
    i = pl.program_id(1)
    t = BAND_T
    q_t = jnp.concatenate([qt_ref[a] for a in range(DIL_Q_TILES)], axis=1)
    row = lax.broadcasted_iota(jnp.int32, (PAIR, q_t.shape[1]), 0)
    j0 = i * DIL_Q_TILES
    j_lo = jnp.maximum(j0 - DIL_BAND, 0)
    j_hi = jnp.minimum(j0 + DIL_Q_TILES + DIL_BAND, nk)
    streams = []
    for head in range(N_HEADS):
        pr, hh = divmod(head, 2)

        def load_k(j, pr=pr):
            return k_ref[pl.ds(pl.multiple_of(j * t, t), t), pr * PAIR:(pr + 1) * PAIR]

        def load_vt(j, head=head):
            return vt_ref[j, head * HEAD_DIM:(head + 1) * HEAD_DIM, :]

        def bias(j, head=head):
            e0 = j - j0 + DIL_BAND + DIL_Q_TILES - 1
            return jnp.concatenate([bias_ref[e0 - a, head] for a in range(DIL_Q_TILES)], axis=1)

        q_pair = q_t[pr * PAIR:(pr + 1) * PAIR]
        qz = jnp.where((row // HEAD_DIM) == hh, q_pair, jnp.zeros_like(q_pair))
        streams.append(dict(qz=qz, load_k=load_k, load_vt=load_vt, shifts=(0.0,), bias_fns=(bias,)))
    heads = _flash_pipelined(streams, ((j_lo, j_hi),), _flash_bufs(scratch))
    o_ref[...] = jnp.concatenate(heads, axis=0).T.astype(o_ref.dtype)


def _dil_call(q_t, k, v_t, bias):
    B, nq, _, t = q_t.shape
    S = k.shape[1]
    nk = v_t.shape[1]
    nqt = DIL_Q_TILES
    return pl.pallas_call(
        functools.partial(_dil_kernel, nk=nk),
        grid=(B, nq // nqt),
        in_specs=[
            pl.BlockSpec((None, nqt, GROUP, t), lambda b, i: (b, i, 0, 0)),
            _resident((None, S, GROUP), lambda b, i: (b, 0, 0)),
            _resident((None, nk, GROUP, t), lambda b, i: (b, 0, 0, 0)),
            _resident(bias.shape, lambda b, i: (0, 0, 0, 0)),
        ],
        out_specs=pl.BlockSpec((None, nqt * t, GROUP), lambda b, i: (b, i, 0)),
        out_shape=jax.ShapeDtypeStruct((B, S, GROUP), BF16),
        scratch_shapes=_flash_scratch(N_HEADS, t, nqt * t),
        compiler_params=pltpu.CompilerParams(
            dimension_semantics=("parallel", "parallel"), vmem_limit_bytes=V7X_VMEM_LIMIT),
        name="dil_attn",
    )(q_t, k, v_t, bias)


def _mla_kernel(qt_ref, k_ref, vt_ref, o_ref, *scratch, nk):
    tk = vt_ref.shape[2]
    tq = qt_ref.shape[2]
    streams = []
    for qi in range(MLA_Q_BLOCKS):
        for hh in range(2):

            def load_k(j, hh=hh):
                return k_ref[pl.ds(pl.multiple_of(j * tk, tk), tk), hh * PAIR:(hh + 1) * PAIR]

            def load_vt(j, hh=hh):
                return vt_ref[j, hh * HEAD_DIM:(hh + 1) * HEAD_DIM, :]

            qz = qt_ref[qi, hh * PAIR:(hh + 1) * PAIR, :]
            streams.append(dict(qz=qz, load_k=load_k, load_vt=load_vt, shifts=(0.0,), bias_fns=(None,)))
    outs = _flash_pipelined(streams, ((0, nk),), _flash_bufs(scratch))
    for qi in range(MLA_Q_BLOCKS):
        o_t = jnp.concatenate(outs[2 * qi:2 * qi + 2], axis=0)
        o_ref[qi * tq:(qi + 1) * tq, :] = o_t.T.astype(o_ref.dtype)


def _mla_call(q_t, k, v_t):
    B, nq, _, tq = q_t.shape
    S = k.shape[1]
    nk, _, tk = v_t.shape[1:]
    nqb = MLA_Q_BLOCKS
    return pl.pallas_call(
        functools.partial(_mla_kernel, nk=nk),
        grid=(B, 2, nq // nqb),
        in_specs=[
            pl.BlockSpec((None, nqb, 2 * PAIR, tq), lambda b, p, i: (b, i, p, 0)),
            _resident((None, S, 2 * PAIR), lambda b, p, i: (b, 0, p)),
            _resident((None, nk, PAIR, tk), lambda b, p, i: (b, 0, p, 0)),
        ],
        out_specs=pl.BlockSpec((None, nqb * tq, PAIR), lambda b, p, i: (b, i, p)),
        out_shape=jax.ShapeDtypeStruct((B, S, GROUP), BF16),
        scratch_shapes=_flash_scratch(2 * nqb, tk, tq),
        compiler_params=pltpu.CompilerParams(
            dimension_semantics=("parallel", "parallel", "parallel"), vmem_limit_bytes=V7X_VMEM_LIMIT),
        name="mla_attn",
    )(q_t, k, v_t)


def _post_kernel(x_ref, oa_ref, ob_ref, oc_ref, od_ref, wo_ref, g_ref, wg_ref, wu_ref, wd_ref, gf_ref,
                 out_ref, acc_ref, h_ref, *, final):
    f = pl.program_id(1)

    @pl.when(f == 0)
    def _():
        x1 = x_ref[...]
        for gi, o_ref in enumerate((oa_ref, ob_ref, oc_ref, od_ref)):
            x1 = x1 + jnp.dot(o_ref[...], wo_ref[gi], preferred_element_type=F32)
        acc_ref[...] = x1
        h_ref[...] = _rms(x1, g_ref[...]).astype(BF16)

    h = h_ref[...]
    gate = jnp.dot(h, wg_ref[...], preferred_element_type=F32)
    up = jnp.dot(h, wu_ref[...], preferred_element_type=F32)
    act = (gate / (1.0 + jnp.exp(-gate))) * up
    acc_ref[...] += jnp.dot(act.astype(BF16), wd_ref[...], preferred_element_type=F32)

    @pl.when(f == pl.num_programs(1) - 1)
    def _():
        y = acc_ref[...]
        out_ref[...] = _rms(y, gf_ref[...]) if final else y


def _post_call(x, o_a, o_b, o_c, o_d, p, g_final, final):
    N, D = x.shape
    ts = TOK_TILE
    d_ff = p['w_gate'].shape[1]
    fc = d_ff // FF_SPLIT
    tok = lambda w: pl.BlockSpec((ts, w), lambda t, f: (t, 0))
    return pl.pallas_call(
        functools.partial(_post_kernel, final=final),
        grid=(N // ts, FF_SPLIT),
        in_specs=[tok(D), tok(GROUP), tok(GROUP), tok(GROUP), tok(GROUP),
                  pl.BlockSpec((4, GROUP, D), lambda t, f: (0, 0, 0)),
                  pl.BlockSpec((1, D), lambda t, f: (0, 0)),
                  pl.BlockSpec((D, fc), lambda t, f: (0, f)),
                  pl.BlockSpec((D, fc), lambda t, f: (0, f)),
                  pl.BlockSpec((fc, D), lambda t, f: (f, 0)),
                  pl.BlockSpec((1, D), lambda t, f: (0, 0))],
        out_specs=tok(D),
        out_shape=jax.ShapeDtypeStruct((N, D), F32),
        scratch_shapes=[pltpu.VMEM((ts, D), F32), pltpu.VMEM((ts, D), BF16)],
        compiler_params=pltpu.CompilerParams(
            dimension_semantics=("parallel", "arbitrary"), vmem_limit_bytes=V7X_VMEM_LIMIT),
        name="post_ffn",
    )(x, o_a, o_b, o_c, o_d, p['w_o'], p['g_ffn'], p['w_gate'], p['w_up'], p['w_down'], g_final)


def _t5_bucket(rel):
    nb = T5_BUCKETS // 2
    max_exact = nb // 2
    side = jnp.where(rel > 0, nb, 0)
    n = jnp.abs(rel)
    large = max_exact + (jnp.log(jnp.maximum(n, 1).astype(F32) / max_exact)
                         / math.log(T5_MAX_DIST / max_exact) * (nb - max_exact)).astype(jnp.int32)
    large = jnp.minimum(large, nb - 1)
    return side + jnp.where(n < max_exact, n, large)


def _lookup(table, idx):
    onehot = idx[..., None, None] == jnp.arange(table.shape[0])[:, None]
    return jnp.sum(jnp.where(onehot, table.astype(F32), 0.0), axis=-2)


def _toeplitz(w, t):
    lead = w.shape[:-1]
    w_pad = jnp.concatenate([w, jnp.zeros(lead + (1,), w.dtype)], axis=-1)
    m = jnp.tile(w_pad, t)[..., :t * (2 * t - 1)].reshape(lead + (t, 2 * t - 1))
    return m[..., t - 1:]


def _toeplitz_tiles(vec_fn, t, n_side):
    reach = n_side * t + t - 1
    w = jnp.moveaxis(vec_fn(jnp.arange(-reach, reach + 1)), -1, 0)
    tiles = [_toeplitz(w[:, (e + n_side) * t:(e + n_side + 2) * t - 1], t) for e in range(-n_side, n_side + 1)]
    return jnp.swapaxes(jnp.stack(tiles), -1, -2)


def _diff_bias(t5_diff):
    def vec(rel):
        return _lookup(t5_diff, _t5_bucket(rel)) * LOG2E
    tiles = _toeplitz_tiles(vec, BAND_T, 4)
    far = vec(jnp.array([-2 * T5_MAX_DIST, 2 * T5_MAX_DIST]))
    return tiles, far


def _dil_bias(t5_dil):
    def vec(rel):
        n = jnp.abs(rel)
        mult = jnp.zeros(rel.shape, F32)
        for window, dil in DIL_PATTERNS:
            mult = mult + ((rel % dil == 0) & (n <= window // 2)).astype(F32)
        b = _lookup(t5_dil, _t5_bucket(rel)) + jnp.log(jnp.maximum(mult, 1.0))[..., None]
        return jnp.where((mult > 0)[..., None], b * LOG2E, NEG)
    return _toeplitz_tiles(vec, BAND_T, DIL_BAND + DIL_Q_TILES - 1)


def _na_bias(rpb):
    d = jnp.arange(NA_ROWS)[:, None]
    ridx = jnp.arange(NA_ROWS)[None, :] - d + NA_ROWS - 1
    qc = jnp.arange(GRID_W)[:, None]
    kc = jnp.arange(GRID_W)[None, :]
    c0 = jnp.clip(qc - NA_COLS // 2, 0, GRID_W - NA_COLS)
    valid = (kc >= c0) & (kc < c0 + NA_COLS)
    rows = rpb.astype(F32)[:, ridx]
    side = GRID_W - NA_COLS
    vals = _toeplitz(jnp.pad(rows, ((0, 0),) * 3 + ((side, side),)), GRID_W)
    vals = jnp.where(valid, vals * LOG2E, NEG)
    vals = vals.transpose(1, 0, 3, 2, 4)
    return vals.reshape(NA_ROWS, N_HEADS, GRID_W, NA_ROWS * GRID_W)


def _rope_tables(S):
    inv_freq = ROPE_THETA ** (-jnp.arange(MLA_HALF, dtype=F32) / MLA_HALF)
    ang = jnp.arange(S).astype(F32)[:, None] * inv_freq[None, :]
    cos, sin = jnp.cos(ang), jnp.sin(ang)
    csn = jnp.concatenate([cos, cos, -sin, sin, jnp.zeros((S, PAIR - 4 * MLA_HALF), F32)], axis=1)
    nt = S // TOK_TILE
    to_t = lambda a: a.reshape(nt, TOK_TILE, MLA_HALF).transpose(0, 2, 1)
    return {'csn': csn, 'cos_t': to_t(cos), 'sin_t': to_t(sin)}


def _layer_weights(w_in, w_uq, w_ukv, w_o, w_gate, w_up, w_down, g_q, g_kv, g_ffn):
    G = GROUP
    col = lambda i: w_in[:, i * G:(i + 1) * G]
    cq, ckv, kr = w_in[:, 9 * G:10 * G], w_in[:, 10 * G:10 * G + 128], w_in[:, 10 * G + 128:]
    kr_swap = jnp.concatenate([kr[:, MLA_HALF:], kr[:, :MLA_HALF]], axis=1)
    kr2 = jnp.concatenate([kr, kr_swap, jnp.zeros((w_in.shape[0], PAIR - 2 * MLA_ROPE), w_in.dtype)], axis=1)
    w_nat = jnp.concatenate([col(0), col(1), col(2), col(4), col(7), cq, ckv, kr2], axis=1)
    w_t = jnp.concatenate([col(3), col(5), col(6), col(8)], axis=1).T
    uq = w_uq.reshape(w_uq.shape[0], N_HEADS, MLA_QK)
    uq = jnp.pad(uq, ((0, 0), (0, 0), (0, PAIR - MLA_QK))).reshape(w_uq.shape[0], N_HEADS * PAIR)
    ukv = w_ukv.reshape(w_ukv.shape[0], N_HEADS, 2 * HEAD_DIM)
    uk = ukv.at[:, :, HEAD_DIM:].set(0.0).reshape(w_ukv.shape[0], N_HEADS * PAIR)
    uv = ukv[:, :, HEAD_DIM:].reshape(w_ukv.shape[0], GROUP)
    place = jnp.zeros((PAIR, N_HEADS * PAIR), F32)
    r = jnp.arange(MLA_ROPE)
    for hd in range(N_HEADS):
        place = place.at[r, hd * PAIR + HEAD_DIM + r].set(1.0)
    b = lambda a: a.astype(BF16)
    row = lambda a: a.astype(F32)[None, :]
    return {
        'w_nat': b(w_nat), 'w_t': b(w_t), 'w_uq_t': b(uq.T), 'w_uk': b(uk), 'w_uv_t': b(uv.T),
        'place': b(place), 'g_q': row(g_q), 'g_kv': row(g_kv), 'g_ffn': row(g_ffn),
        'w_o': b(w_o.reshape(4, GROUP, w_o.shape[1])), 'w_gate': b(w_gate), 'w_up': b(w_up),
        'w_down': b(w_down),
    }


def kernel(x, attn_norm, w_in, na_rpb, diff_lambda, diff_subln, mla_q_norm, w_uq, mla_kv_norm, w_ukv,
           t5_table, w_o, ffn_norm, w_gate, w_up, w_down, final_norm):
    B, S, D = x.shape
    depth = w_in.shape[0]
    rope = _rope_tables(S)
    diff_tiles, diff_far = _diff_bias(t5_table[:, :N_HEADS])
    dil_tiles = _dil_bias(t5_table[:, N_HEADS:])
    g_final = final_norm.astype(F32)[None, :]

    for l in range(depth):
        p = _layer_weights(w_in[l], w_uq[l], w_ukv[l], w_o[l], w_gate[l], w_up[l], w_down[l],
                           mla_q_norm[l], mla_kv_norm[l], ffn_norm[l])
        (qa, ka, va, kb, kc, kd, qb_t, vb_t, qc_t, vc_t, qd_t, vd_t) = _proj_call(
            x, attn_norm[l].astype(F32)[None, :], p, rope)

        o_a = _na_call(qa, ka, va, _na_bias(na_rpb[l]))

        lam_init = 0.8 - 0.6 * math.exp(-0.3 * l)
        lq1, lk1, lq2, lk2 = [diff_lambda[l, j].astype(F32) for j in range(4)]
        lam = jnp.exp(jnp.sum(lq1 * lk1)) - jnp.exp(jnp.sum(lq2 * lk2)) + lam_init
        far = diff_far.reshape(2, 2, 2).transpose(1, 0, 2).reshape(2, 4)
        consts = jnp.concatenate([jnp.broadcast_to(lam, (2, 1)), far, jnp.zeros((2, 3), F32)], axis=1)
        consts = jnp.broadcast_to(consts[:, :, None], (2, 8, 128))
        g_col = jnp.broadcast_to(diff_subln[l].astype(F32)[:, None], (HEAD_DIM, 128))
        o_b = _diff_call(consts, g_col, qb_t, kb, vb_t, diff_tiles, 1.0 - lam_init)

        o_c = _dil_call(qc_t, kc, vc_t, dil_tiles)
        o_d = _mla_call(qd_t, kd, vd_t)

        flat = lambda a: a.reshape(B * S, a.shape[-1])
        x = _post_call(flat(x), flat(o_a), flat(o_b), flat(o_c), flat(o_d), p, g_final,
                       final=(l == depth - 1)).reshape(B, S, D)
    return x
```
